```python
import jax, jax.numpy as jnp
from jax import lax
import numpy as np

D_MODEL = 2048
BATCH = 8
SEQ = 4096
DEPTH = 4

N_MIXERS = 2
N_A = (DEPTH + 1) // 2
N_B = DEPTH // 2
CONF_KERNEL = 31
SHORT_KERNEL = 3
D_FF = 4 * D_MODEL
N_MOD = 6
RMS_EPS = 1e-6
LN_EPS = 1e-5

kernel_name = "hybrid_conformer_shortconv_adaln_trunk"


def rmsnorm(x, g):
    xf = x.astype(jnp.float32)
    y = xf * lax.rsqrt(jnp.mean(xf * xf, axis=-1, keepdims=True) + RMS_EPS)
    return (y * g.astype(jnp.float32)).astype(x.dtype)


def layernorm(x, g, b):
    xf = x.astype(jnp.float32)
    mu = jnp.mean(xf, axis=-1, keepdims=True)
    var = jnp.mean(jnp.square(xf - mu), axis=-1, keepdims=True)
    y = (xf - mu) * lax.rsqrt(var + LN_EPS)
    return (y * g.astype(jnp.float32) + b.astype(jnp.float32)).astype(x.dtype)


def causal_depthwise_conv(u, w):
    k = w.shape[0]
    return lax.conv_general_dilated(
        u, w[:, None, :].astype(u.dtype),
        window_strides=(1,), padding=((k - 1, 0),),
        dimension_numbers=("NWC", "WIO", "NWC"),
        feature_group_count=u.shape[-1])


def conformer_conv_module(h, w1, b1, dw, dwb, ln_g, ln_b, w2, b2):
    u = jnp.einsum("bsd,de->bse", h, w1) + b1
    a, g = jnp.split(u, 2, axis=-1)
    u = a * jax.nn.sigmoid(g)
    u = causal_depthwise_conv(u, dw) + dwb
    u = jax.nn.silu(layernorm(u, ln_g, ln_b))
    return jnp.einsum("bsd,de->bse", u, w2) + b2


def short_gated_conv(h, w_in, w_conv, w_out):
    z = jnp.einsum("bsd,de->bse", h, w_in)
    gate_b, gate_c, xin = jnp.split(z, 3, axis=-1)
    u = causal_depthwise_conv(gate_c * xin, w_conv)
    return jnp.einsum("bsd,de->bse", gate_b * u, w_out)


def sq_relu_mlp(h, w1, w2):
    u = jnp.square(jax.nn.relu(jnp.einsum("bsd,df->bsf", h, w1)))
    return jnp.einsum("bsf,fd->bsd", u, w2)


def _fwd_setup_inputs(seed: int = 0) -> dict:
    key = jax.random.key(seed)
    ks = jax.random.split(key, 24)
    D = D_MODEL
    f32 = jnp.float32

    def nrm(k, shape, scale):
        return jax.random.normal(k, shape, f32) * scale

    def gain(k, shape):
        return 1.0 + 0.05 * jax.random.normal(k, shape, f32)

    return {
        "x": nrm(ks[0], (BATCH, SEQ, D), 1.0),
        "c": nrm(ks[1], (BATCH, D), 1.0),
        "mod_w": nrm(ks[2], (DEPTH, D, N_MOD * D), 0.5 * D ** -0.5),
        "mod_b": nrm(ks[3], (DEPTH, N_MOD * D), 0.02),
        "pre_mix_g": gain(ks[4], (DEPTH, D)),
        "post_mix_g": gain(ks[5], (DEPTH, D)),
        "pre_ffn_g": gain(ks[6], (DEPTH, D)),
        "post_ffn_g": gain(ks[7], (DEPTH, D)),
        "a_w1": nrm(ks[8], (N_A, D, 2 * D), D ** -0.5),
        "a_b1": nrm(ks[9], (N_A, 2 * D), 0.02),
        "a_dw": nrm(ks[10], (N_A, CONF_KERNEL, D), CONF_KERNEL ** -0.5),
        "a_dwb": nrm(ks[11], (N_A, D), 0.02),
        "a_ln_g": gain(ks[12], (N_A, D)),
        "a_ln_b": nrm(ks[13], (N_A, D), 0.02),
        "a_w2": nrm(ks[14], (N_A, D, D), D ** -0.5),
        "a_b2": nrm(ks[15], (N_A, D), 0.02),
        "b_w_in": nrm(ks[16], (N_B, D, 3 * D), D ** -0.5),
        "b_conv": nrm(ks[17], (N_B, SHORT_KERNEL, D), SHORT_KERNEL ** -0.5),
        "b_w_out": nrm(ks[18], (N_B, D, D), D ** -0.5),
        "f_w1": nrm(ks[19], (DEPTH, D, D_FF), D ** -0.5),
        "f_w2": nrm(ks[20], (DEPTH, D_FF, D), D_FF ** -0.5),
    }


def _fwd_reference(x, c, mod_w, mod_b, pre_mix_g, post_mix_g, pre_ffn_g, post_ffn_g,
              a_w1, a_b1, a_dw, a_dwb, a_ln_g, a_ln_b, a_w2, a_b2,
              b_w_in, b_conv, b_w_out, f_w1, f_w2):
    c_act = jax.nn.silu(c)
    for i in range(DEPTH):
        mod = jnp.einsum("bd,de->be", c_act, mod_w[i]) + mod_b[i]
        sh_m, sc_m, gt_m, sh_f, sc_f, gt_f = [m[:, None, :] for m in jnp.split(mod, N_MOD, axis=-1)]

        h = rmsnorm(x, pre_mix_g[i]) * (1.0 + sc_m) + sh_m
        j = i // N_MIXERS
        if i % N_MIXERS == 0:
            y = conformer_conv_module(h, a_w1[j], a_b1[j], a_dw[j], a_dwb[j],
                                      a_ln_g[j], a_ln_b[j], a_w2[j], a_b2[j])
        else:
            y = short_gated_conv(h, b_w_in[j], b_conv[j], b_w_out[j])
        x = x + gt_m * rmsnorm(y, post_mix_g[i])

        h = rmsnorm(x, pre_ffn_g[i]) * (1.0 + sc_f) + sh_f
        y = sq_relu_mlp(h, f_w1[i], f_w2[i])
        x = x + gt_f * rmsnorm(y, post_ffn_g[i])
    return x


import jax as _jax
import jax.numpy as _jnp

TWIN_FORMAT = 'train_step'
FWD_PARAMS = ['x', 'c', 'mod_w', 'mod_b', 'pre_mix_g', 'post_mix_g', 'pre_ffn_g', 'post_ffn_g', 'a_w1', 'a_b1', 'a_dw', 'a_dwb', 'a_ln_g', 'a_ln_b', 'a_w2', 'a_b2', 'b_w_in', 'b_conv', 'b_w_out', 'f_w1', 'f_w2']
TWIN_WEIGHTS = ['mod_w', 'mod_b', 'pre_mix_g', 'post_mix_g', 'pre_ffn_g', 'post_ffn_g', 'a_w1', 'a_b1', 'a_dw', 'a_dwb', 'a_ln_g', 'a_ln_b', 'a_w2', 'a_b2', 'b_w_in', 'b_conv', 'b_w_out', 'f_w1', 'f_w2']
TWIN_DIFF_INPUT = 'x'
TWIN_INPUTS = ['x', 'c', 'mod_w', 'mod_b', 'pre_mix_g', 'post_mix_g', 'pre_ffn_g', 'post_ffn_g', 'a_w1', 'a_b1', 'a_dw', 'a_dwb', 'a_ln_g', 'a_ln_b', 'a_w2', 'a_b2', 'b_w_in', 'b_conv', 'b_w_out', 'f_w1', 'f_w2', 'loss_target', 'm_mod_w', 'm_mod_b', 'm_pre_mix_g', 'm_post_mix_g', 'm_pre_ffn_g', 'm_post_ffn_g', 'm_a_w1', 'm_a_b1', 'm_a_dw', 'm_a_dwb', 'm_a_ln_g', 'm_a_ln_b', 'm_a_w2', 'm_a_b2', 'm_b_w_in', 'm_b_conv', 'm_b_w_out', 'm_f_w1', 'm_f_w2', 'v_mod_w', 'v_mod_b', 'v_pre_mix_g', 'v_post_mix_g', 'v_pre_ffn_g', 'v_post_ffn_g', 'v_a_w1', 'v_a_b1', 'v_a_dw', 'v_a_dwb', 'v_a_ln_g', 'v_a_ln_b', 'v_a_w2', 'v_a_b2', 'v_b_w_in', 'v_b_conv', 'v_b_w_out', 'v_f_w1', 'v_f_w2']
TWIN_OUTPUTS = ['loss', 'grad_x', 'grad_mod_w', 'grad_mod_b', 'grad_pre_mix_g', 'grad_post_mix_g', 'grad_pre_ffn_g', 'grad_post_ffn_g', 'grad_a_w1', 'grad_a_b1', 'grad_a_dw', 'grad_a_dwb', 'grad_a_ln_g', 'grad_a_ln_b', 'grad_a_w2', 'grad_a_b2', 'grad_b_w_in', 'grad_b_conv', 'grad_b_w_out', 'grad_f_w1', 'grad_f_w2', 'delta_mod_w', 'delta_mod_b', 'delta_pre_mix_g', 'delta_post_mix_g', 'delta_pre_ffn_g', 'delta_post_ffn_g', 'delta_a_w1', 'delta_a_b1', 'delta_a_dw', 'delta_a_dwb', 'delta_a_ln_g', 'delta_a_ln_b', 'delta_a_w2', 'delta_a_b2', 'delta_b_w_in', 'delta_b_conv', 'delta_b_w_out', 'delta_f_w1', 'delta_f_w2', 'new_m_mod_w', 'new_m_mod_b', 'new_m_pre_mix_g', 'new_m_post_mix_g', 'new_m_pre_ffn_g', 'new_m_post_ffn_g', 'new_m_a_w1', 'new_m_a_b1', 'new_m_a_dw', 'new_m_a_dwb', 'new_m_a_ln_g', 'new_m_a_ln_b', 'new_m_a_w2', 'new_m_a_b2', 'new_m_b_w_in', 'new_m_b_conv', 'new_m_b_w_out', 'new_m_f_w1', 'new_m_f_w2', 'new_v_mod_w', 'new_v_mod_b', 'new_v_pre_mix_g', 'new_v_post_mix_g', 'new_v_pre_ffn_g', 'new_v_post_ffn_g', 'new_v_a_w1', 'new_v_a_b1', 'new_v_a_dw', 'new_v_a_dwb', 'new_v_a_ln_g', 'new_v_a_ln_b', 'new_v_a_w2', 'new_v_a_b2', 'new_v_b_w_in', 'new_v_b_conv', 'new_v_b_w_out', 'new_v_f_w1', 'new_v_f_w2']
TWIN_LEAF_KINDS = {'loss': 'loss', 'grad_x': 'grad_x', 'grad_mod_w': 'grad_w', 'grad_mod_b': 'grad_w', 'grad_pre_mix_g': 'grad_w', 'grad_post_mix_g': 'grad_w', 'grad_pre_ffn_g': 'grad_w', 'grad_post_ffn_g': 'grad_w', 'grad_a_w1': 'grad_w', 'grad_a_b1': 'grad_w', 'grad_a_dw': 'grad_w', 'grad_a_dwb': 'grad_w', 'grad_a_ln_g': 'grad_w', 'grad_a_ln_b': 'grad_w', 'grad_a_w2': 'grad_w', 'grad_a_b2': 'grad_w', 'grad_b_w_in': 'grad_w', 'grad_b_conv': 'grad_w', 'grad_b_w_out': 'grad_w', 'grad_f_w1': 'grad_w', 'grad_f_w2': 'grad_w', 'delta_mod_w': 'delta_w', 'delta_mod_b': 'delta_w', 'delta_pre_mix_g': 'delta_w', 'delta_post_mix_g': 'delta_w', 'delta_pre_ffn_g': 'delta_w', 'delta_post_ffn_g': 'delta_w', 'delta_a_w1': 'delta_w', 'delta_a_b1': 'delta_w', 'delta_a_dw': 'delta_w', 'delta_a_dwb': 'delta_w', 'delta_a_ln_g': 'delta_w', 'delta_a_ln_b': 'delta_w', 'delta_a_w2': 'delta_w', 'delta_a_b2': 'delta_w', 'delta_b_w_in': 'delta_w', 'delta_b_conv': 'delta_w', 'delta_b_w_out': 'delta_w', 'delta_f_w1': 'delta_w', 'delta_f_w2': 'delta_w', 'new_m_mod_w': 'new_m', 'new_m_mod_b': 'new_m', 'new_m_pre_mix_g': 'new_m', 'new_m_post_mix_g': 'new_m', 'new_m_pre_ffn_g': 'new_m', 'new_m_post_ffn_g': 'new_m', 'new_m_a_w1': 'new_m', 'new_m_a_b1': 'new_m', 'new_m_a_dw': 'new_m', 'new_m_a_dwb': 'new_m', 'new_m_a_ln_g': 'new_m', 'new_m_a_ln_b': 'new_m', 'new_m_a_w2': 'new_m', 'new_m_a_b2': 'new_m', 'new_m_b_w_in': 'new_m', 'new_m_b_conv': 'new_m', 'new_m_b_w_out': 'new_m', 'new_m_f_w1': 'new_m', 'new_m_f_w2': 'new_m', 'new_v_mod_w': 'new_v', 'new_v_mod_b': 'new_v', 'new_v_pre_mix_g': 'new_v', 'new_v_post_mix_g': 'new_v', 'new_v_pre_ffn_g': 'new_v', 'new_v_post_ffn_g': 'new_v', 'new_v_a_w1': 'new_v', 'new_v_a_b1': 'new_v', 'new_v_a_dw': 'new_v', 'new_v_a_dwb': 'new_v', 'new_v_a_ln_g': 'new_v', 'new_v_a_ln_b': 'new_v', 'new_v_a_w2': 'new_v', 'new_v_a_b2': 'new_v', 'new_v_b_w_in': 'new_v', 'new_v_b_conv': 'new_v', 'new_v_b_w_out': 'new_v', 'new_v_f_w1': 'new_v', 'new_v_f_w2': 'new_v'}


def _forward(args):
    return _fwd_reference(*[args[k] for k in FWD_PARAMS])


def _output_shape():
    def fwd():
        inp = _fwd_setup_inputs(0)
        return _fwd_reference(*[inp[k] for k in FWD_PARAMS])
    out = _jax.eval_shape(fwd)
    return out.shape, out.dtype

N_MICROBATCH = 1
ADAM_LR = 0.001
ADAM_B1 = 0.9
ADAM_B2 = 0.999
ADAM_EPS = 1e-08
ADAM_WD = 0.01
ADAM_STEP = 10
PER_EXAMPLE_BATCH_AXIS = {'x': 0, 'c': 0, 'loss_target': 0}
SHARED_INPUTS = []
_WEIGHT_DTYPES = {'mod_w': _jnp.float32, 'mod_b': _jnp.float32, 'pre_mix_g': _jnp.float32, 'post_mix_g': _jnp.float32, 'pre_ffn_g': _jnp.float32, 'post_ffn_g': _jnp.float32, 'a_w1': _jnp.float32, 'a_b1': _jnp.float32, 'a_dw': _jnp.float32, 'a_dwb': _jnp.float32, 'a_ln_g': _jnp.float32, 'a_ln_b': _jnp.float32, 'a_w2': _jnp.float32, 'a_b2': _jnp.float32, 'b_w_in': _jnp.float32, 'b_conv': _jnp.float32, 'b_w_out': _jnp.float32, 'f_w1': _jnp.float32, 'f_w2': _jnp.float32}
MOMENT_SCALE = {'mod_w': 6.798035e-01, 'mod_b': 1.458340e+00, 'pre_mix_g': 8.170205e-02, 'post_mix_g': 1.647911e+00, 'pre_ffn_g': 9.224159e-02, 'post_ffn_g': 1.704349e+00, 'a_w1': 1.001282e-01, 'a_b1': 3.468438e-01, 'a_dw': 1.526555e-01, 'a_dwb': 9.024628e-01, 'a_ln_g': 3.954942e-01, 'a_ln_b': 5.595896e-01, 'a_w2': 2.629439e-01, 'a_b2': 1.121898e+00, 'b_w_in': 5.472040e-02, 'b_conv': 5.559680e-02, 'b_w_out': 5.694529e-02, 'f_w1': 6.800292e-02, 'f_w2': 2.888430e-01}


def _to_microbatches(a, axis):
    t = _jnp.moveaxis(a, axis, 0)
    t = t.reshape((N_MICROBATCH, t.shape[0] // N_MICROBATCH) + t.shape[1:])
    return _jnp.moveaxis(t, 1, axis + 1)


def setup_inputs(seed: int = 0) -> dict:
    inp = _fwd_setup_inputs(seed)
    key = _jax.random.fold_in(_jax.random.key(seed), 7919)
    shape, _ = _output_shape()
    out = dict(inp)
    out["loss_target"] = _jax.random.normal(_jax.random.fold_in(key, 0), shape, _jnp.float32)
    for i, name in enumerate(TWIN_WEIGHTS):
        w = inp[name].astype(_jnp.float32)
        if MOMENT_SCALE is None:
            s = _jnp.sqrt(_jnp.mean(_jnp.square(w)) + 1e-30)
        else:
            s = MOMENT_SCALE[name]
        km, kv = _jax.random.split(_jax.random.fold_in(key, i + 1))
        out[name] = w
        out["m_" + name] = s * _jax.random.normal(km, w.shape, _jnp.float32)
        out["v_" + name] = (s * s) * _jax.random.uniform(kv, w.shape, _jnp.float32, 0.5, 1.5)
    if N_MICROBATCH > 1:
        for name, axis in PER_EXAMPLE_BATCH_AXIS.items():
            out[name] = _to_microbatches(out[name], axis)
    return {'x': out['x'], 'c': out['c'], 'mod_w': out['mod_w'], 'mod_b': out['mod_b'], 'pre_mix_g': out['pre_mix_g'], 'post_mix_g': out['post_mix_g'], 'pre_ffn_g': out['pre_ffn_g'], 'post_ffn_g': out['post_ffn_g'], 'a_w1': out['a_w1'], 'a_b1': out['a_b1'], 'a_dw': out['a_dw'], 'a_dwb': out['a_dwb'], 'a_ln_g': out['a_ln_g'], 'a_ln_b': out['a_ln_b'], 'a_w2': out['a_w2'], 'a_b2': out['a_b2'], 'b_w_in': out['b_w_in'], 'b_conv': out['b_conv'], 'b_w_out': out['b_w_out'], 'f_w1': out['f_w1'], 'f_w2': out['f_w2'], 'loss_target': out['loss_target'], 'm_mod_w': out['m_mod_w'], 'm_mod_b': out['m_mod_b'], 'm_pre_mix_g': out['m_pre_mix_g'], 'm_post_mix_g': out['m_post_mix_g'], 'm_pre_ffn_g': out['m_pre_ffn_g'], 'm_post_ffn_g': out['m_post_ffn_g'], 'm_a_w1': out['m_a_w1'], 'm_a_b1': out['m_a_b1'], 'm_a_dw': out['m_a_dw'], 'm_a_dwb': out['m_a_dwb'], 'm_a_ln_g': out['m_a_ln_g'], 'm_a_ln_b': out['m_a_ln_b'], 'm_a_w2': out['m_a_w2'], 'm_a_b2': out['m_a_b2'], 'm_b_w_in': out['m_b_w_in'], 'm_b_conv': out['m_b_conv'], 'm_b_w_out': out['m_b_w_out'], 'm_f_w1': out['m_f_w1'], 'm_f_w2': out['m_f_w2'], 'v_mod_w': out['v_mod_w'], 'v_mod_b': out['v_mod_b'], 'v_pre_mix_g': out['v_pre_mix_g'], 'v_post_mix_g': out['v_post_mix_g'], 'v_pre_ffn_g': out['v_pre_ffn_g'], 'v_post_ffn_g': out['v_post_ffn_g'], 'v_a_w1': out['v_a_w1'], 'v_a_b1': out['v_a_b1'], 'v_a_dw': out['v_a_dw'], 'v_a_dwb': out['v_a_dwb'], 'v_a_ln_g': out['v_a_ln_g'], 'v_a_ln_b': out['v_a_ln_b'], 'v_a_w2': out['v_a_w2'], 'v_a_b2': out['v_a_b2'], 'v_b_w_in': out['v_b_w_in'], 'v_b_conv': out['v_b_conv'], 'v_b_w_out': out['v_b_w_out'], 'v_f_w1': out['v_f_w1'], 'v_f_w2': out['v_f_w2']}


def _loss(weights, diff, rest, loss_target):
    with _jax.named_scope("forward"):
        args = {**rest, TWIN_DIFF_INPUT: diff, **{k: w.astype(_WEIGHT_DTYPES[k]) for k, w in weights.items()}}
        y = _forward(args)
    with _jax.named_scope("loss_head"):
        err = _jnp.square(y.astype(_jnp.float32) - loss_target)
        return 0.5 * _jnp.sum(_jnp.mean(err, axis=-1)) if err.ndim else 0.5 * err


def _adamw(w, g, m, v):
    m = ADAM_B1 * m + (1.0 - ADAM_B1) * g
    v = ADAM_B2 * v + (1.0 - ADAM_B2) * _jnp.square(g)
    m_hat = m / (1.0 - ADAM_B1 ** ADAM_STEP)
    v_hat = v / (1.0 - ADAM_B2 ** ADAM_STEP)
    delta = -ADAM_LR * (m_hat / (_jnp.sqrt(v_hat) + ADAM_EPS) + ADAM_WD * w)
    return delta, m, v


def reference(x, c, mod_w, mod_b, pre_mix_g, post_mix_g, pre_ffn_g, post_ffn_g, a_w1, a_b1, a_dw, a_dwb, a_ln_g, a_ln_b, a_w2, a_b2, b_w_in, b_conv, b_w_out, f_w1, f_w2, loss_target, m_mod_w, m_mod_b, m_pre_mix_g, m_post_mix_g, m_pre_ffn_g, m_post_ffn_g, m_a_w1, m_a_b1, m_a_dw, m_a_dwb, m_a_ln_g, m_a_ln_b, m_a_w2, m_a_b2, m_b_w_in, m_b_conv, m_b_w_out, m_f_w1, m_f_w2, v_mod_w, v_mod_b, v_pre_mix_g, v_post_mix_g, v_pre_ffn_g, v_post_ffn_g, v_a_w1, v_a_b1, v_a_dw, v_a_dwb, v_a_ln_g, v_a_ln_b, v_a_w2, v_a_b2, v_b_w_in, v_b_conv, v_b_w_out, v_f_w1, v_f_w2):
    given = dict(x=x, c=c, mod_w=mod_w, mod_b=mod_b, pre_mix_g=pre_mix_g, post_mix_g=post_mix_g, pre_ffn_g=pre_ffn_g, post_ffn_g=post_ffn_g, a_w1=a_w1, a_b1=a_b1, a_dw=a_dw, a_dwb=a_dwb, a_ln_g=a_ln_g, a_ln_b=a_ln_b, a_w2=a_w2, a_b2=a_b2, b_w_in=b_w_in, b_conv=b_conv, b_w_out=b_w_out, f_w1=f_w1, f_w2=f_w2, loss_target=loss_target, m_mod_w=m_mod_w, m_mod_b=m_mod_b, m_pre_mix_g=m_pre_mix_g, m_post_mix_g=m_post_mix_g, m_pre_ffn_g=m_pre_ffn_g, m_post_ffn_g=m_post_ffn_g, m_a_w1=m_a_w1, m_a_b1=m_a_b1, m_a_dw=m_a_dw, m_a_dwb=m_a_dwb, m_a_ln_g=m_a_ln_g, m_a_ln_b=m_a_ln_b, m_a_w2=m_a_w2, m_a_b2=m_a_b2, m_b_w_in=m_b_w_in, m_b_conv=m_b_conv, m_b_w_out=m_b_w_out, m_f_w1=m_f_w1, m_f_w2=m_f_w2, v_mod_w=v_mod_w, v_mod_b=v_mod_b, v_pre_mix_g=v_pre_mix_g, v_post_mix_g=v_post_mix_g, v_pre_ffn_g=v_pre_ffn_g, v_post_ffn_g=v_post_ffn_g, v_a_w1=v_a_w1, v_a_b1=v_a_b1, v_a_dw=v_a_dw, v_a_dwb=v_a_dwb, v_a_ln_g=v_a_ln_g, v_a_ln_b=v_a_ln_b, v_a_w2=v_a_w2, v_a_b2=v_a_b2, v_b_w_in=v_b_w_in, v_b_conv=v_b_conv, v_b_w_out=v_b_w_out, v_f_w1=v_f_w1, v_f_w2=v_f_w2)
    weights = {n: given[n] for n in TWIN_WEIGHTS}
    shared = {n: given[n] for n in SHARED_INPUTS}
    per_example = {n: given[n] for n in ['x', 'c']}
    grad_fn = _jax.value_and_grad(_loss, argnums=(0, 1))

    def one_microbatch(ex, loss_target):
        ex = dict(ex)
        diff = ex.pop(TWIN_DIFF_INPUT)
        return grad_fn(weights, diff, {**shared, **ex}, loss_target)

    if N_MICROBATCH == 1:
        loss, (grad_w, grad_x) = one_microbatch(per_example, given["loss_target"])
    else:
        def body(carry, xs):
            loss_sum, grad_sum = carry
            l_k, (gw_k, gx_k) = one_microbatch(xs[0], xs[1])
            with _jax.named_scope("update"):
                return (loss_sum + l_k, _jax.tree.map(_jnp.add, grad_sum, gw_k)), gx_k

        init = (_jnp.zeros((), _jnp.float32), _jax.tree.map(_jnp.zeros_like, weights))
        (loss, grad_w), grad_x = _jax.lax.scan(body, init, (per_example, given["loss_target"]))
    with _jax.named_scope("update"):
        delta_w, new_m, new_v = {}, {}, {}
        for n in TWIN_WEIGHTS:
            delta_w[n], new_m[n], new_v[n] = _adamw(weights[n], grad_w[n], given["m_" + n], given["v_" + n])
    return (loss, grad_x, *[grad_w[n] for n in TWIN_WEIGHTS], *[delta_w[n] for n in TWIN_WEIGHTS],
            *[new_m[n] for n in TWIN_WEIGHTS], *[new_v[n] for n in TWIN_WEIGHTS])
```

```python
import functools

import jax
import jax.numpy as jnp
from jax import lax
from jax.experimental import pallas as pl
from jax.experimental.pallas import tpu as pltpu

MXU_DTYPE = jnp.bfloat16
ACT_DTYPE = jnp.bfloat16
COMM_DTYPE = jnp.bfloat16

N_DEV = 8
N_CHIP = 4
RMS_EPS = 1e-6
LN_EPS = 1e-5
ADAM_LR = 0.001
ADAM_B1 = 0.9
ADAM_B2 = 0.999
ADAM_EPS = 1e-08
ADAM_WD = 0.01
ADAM_STEP = 10

V7X_VMEM_LIMIT_BYTES = 56 * 1024 * 1024
LANE = 128
ROW_TILE = 256
CONV_TILE = 128
CONV_HALO = 32
SHORT_HALO = 8
CONV_CHUNK = 256
MM_TM, MM_TN, MM_TK = 1024, 1024, 512

F32 = jnp.float32
MESH = pl.DeviceIdType.MESH


def _pick(dim, pref, mult=LANE):
    if dim <= pref:
        return dim
    t = pref - pref % mult
    while dim % t:
        t -= mult
    return t


def _params(*sem):
    return pltpu.CompilerParams(dimension_semantics=sem, vmem_limit_bytes=V7X_VMEM_LIMIT_BYTES)


def _pos():
    return lax.axis_index("x"), lax.axis_index("y"), lax.axis_index("c")


def _flip(v, bit):
    return 1 - v if bit else v


def _slot(j):
    return (j % 2) * N_CHIP + j // 2


def _small_gather(name, v):
    rows, cols = v.shape

    def body(v_ref, out_ref, send_sems, recv_sems, local_sem):
        x, y, c = _pos()
        me = 4 * x + 2 * y + c
        mine = pltpu.make_async_copy(v_ref, out_ref.at[me], local_sem)
        mine.start()

        def copy(k, block):
            peer = (_flip(x, k >> 2 & 1), _flip(y, k >> 1 & 1), _flip(c, k & 1))
            return pltpu.make_async_remote_copy(
                src_ref=v_ref, dst_ref=out_ref.at[block], send_sem=send_sems.at[k - 1],
                recv_sem=recv_sems.at[k - 1], device_id=peer, device_id_type=MESH)

        sends = [copy(k, me) for k in range(1, N_DEV)]
        for cp in sends:
            cp.start()
        for k in range(1, N_DEV):
            px, py, pc = _flip(x, k >> 2 & 1), _flip(y, k >> 1 & 1), _flip(c, k & 1)
            copy(k, 4 * px + 2 * py + pc).wait_recv()
        for cp in sends:
            cp.wait_send()
        mine.wait()

    return pl.pallas_call(
        body, name=name,
        out_shape=jax.ShapeDtypeStruct((N_DEV, rows, cols), v.dtype),
        in_specs=[pl.BlockSpec(memory_space=pltpu.VMEM)],
        out_specs=pl.BlockSpec(memory_space=pltpu.VMEM),
        scratch_shapes=[pltpu.SemaphoreType.DMA((N_DEV - 1,)), pltpu.SemaphoreType.DMA((N_DEV - 1,)),
                        pltpu.SemaphoreType.DMA],
        compiler_params=pltpu.CompilerParams(vmem_limit_bytes=V7X_VMEM_LIMIT_BYTES),
    )(v)


def _gather_weights(name, shards):
    n = len(shards)

    def body(*refs):
        w_refs, out_refs = refs[:n], refs[n:2 * n]
        send_sems, recv_sems, local_sems = refs[2 * n:]
        x, y, c = _pos()
        me = (x, y, c)
        sibling = (x, y, 1 - c)
        chips = [(1 - x, y), (x, 1 - y), (1 - x, 1 - y)]

        def block(i, px, py, pc):
            return out_refs[i].at[4 * px + 2 * py + pc]

        def copy(i, k, owner, to, src=None):
            dst = block(i, *owner)
            return pltpu.make_async_remote_copy(
                src_ref=dst if src is None else src, dst_ref=dst, send_sem=send_sems.at[7 * i + k],
                recv_sem=recv_sems.at[7 * i + k], device_id=to, device_id_type=MESH)

        mine = [pltpu.make_async_copy(w_refs[i], block(i, *me), local_sems.at[i]) for i in range(n)]
        for cp in mine:
            cp.start()
        first = []
        for i in range(n):
            first.append(copy(i, 0, me, sibling, src=w_refs[i]))
            first += [copy(i, 1 + j, me, (*chip, c), src=w_refs[i]) for j, chip in enumerate(chips)]
        for cp in first:
            cp.start()
        passed = []
        for i in range(n):
            for j, chip in enumerate(chips):
                copy(i, 1 + j, (*chip, c), me).wait_recv()
                fwd = copy(i, 4 + j, (*chip, c), sibling)
                fwd.start()
                passed.append(fwd)
        for i in range(n):
            copy(i, 0, sibling, me).wait_recv()
            for j, chip in enumerate(chips):
                copy(i, 4 + j, (*chip, 1 - c), me).wait_recv()
        for cp in first + passed:
            cp.wait_send()
        for cp in mine:
            cp.wait()

    any_spec = pl.BlockSpec(memory_space=pl.ANY)
    return pl.pallas_call(
        body, name=name,
        out_shape=[jax.ShapeDtypeStruct((N_DEV,) + s.shape, s.dtype) for s in shards],
        in_specs=[any_spec] * n, out_specs=[any_spec] * n,
        scratch_shapes=[pltpu.SemaphoreType.DMA((7 * n,)), pltpu.SemaphoreType.DMA((7 * n,)),
                        pltpu.SemaphoreType.DMA((n,))],
    )(*shards)


def _sibling_exchange(name, grads):
    n = len(grads)

    def body(*refs):
        g_refs, out_refs = refs[:n], refs[n:2 * n]
        send_sems, recv_sems = refs[2 * n:]
        x, y, c = _pos()
        copies = [pltpu.make_async_remote_copy(
            src_ref=g_refs[i].at[pl.ds((1 - c) * N_CHIP, N_CHIP)], dst_ref=out_refs[i],
            send_sem=send_sems.at[i], recv_sem=recv_sems.at[i],
            device_id=(x, y, 1 - c), device_id_type=MESH) for i in range(n)]
        for cp in copies:
            cp.start()
        for cp in copies:
            cp.wait()

    any_spec = pl.BlockSpec(memory_space=pl.ANY)
    return pl.pallas_call(
        body, name=name,
        out_shape=[jax.ShapeDtypeStruct((N_CHIP,) + g.shape[1:], g.dtype) for g in grads],
        in_specs=[any_spec] * n, out_specs=[any_spec] * n,
        scratch_shapes=[pltpu.SemaphoreType.DMA((n,)), pltpu.SemaphoreType.DMA((n,))],
    )(*grads)


def _chip_exchange(name, partials, stacks, layers):
    n = len(partials)

    def body(*refs):
        p_refs, out_refs = refs[:n], refs[2 * n:3 * n]
        send_sems, recv_sems, local_sems = refs[3 * n:]
        x, y, c = _pos()
        chip = 2 * x + y

        def copy(i, k, src_chip):
            px, py = _flip(x, k >> 1 & 1), _flip(y, k & 1)
            return pltpu.make_async_remote_copy(
                src_ref=p_refs[i].at[2 * px + py], dst_ref=out_refs[i].at[layers[i], src_chip],
                send_sem=send_sems.at[3 * i + k - 1], recv_sem=recv_sems.at[3 * i + k - 1],
                device_id=(px, py, c), device_id_type=MESH)

        mine = [pltpu.make_async_copy(p_refs[i].at[chip], out_refs[i].at[layers[i], chip], local_sems.at[i])
                for i in range(n)]
        for cp in mine:
            cp.start()
        sends = [copy(i, k, chip) for i in range(n) for k in range(1, N_CHIP)]
        for cp in sends:
            cp.start()
        for i in range(n):
            for k in range(1, N_CHIP):
                copy(i, k, 2 * _flip(x, k >> 1 & 1) + _flip(y, k & 1)).wait_recv()
        for cp in sends:
            cp.wait_send()
        for cp in mine:
            cp.wait()

    any_spec = pl.BlockSpec(memory_space=pl.ANY)
    return pl.pallas_call(
        body, name=name,
        out_shape=[jax.ShapeDtypeStruct(s.shape, s.dtype) for s in stacks],
        in_specs=[any_spec] * (2 * n), out_specs=[any_spec] * n,
        input_output_aliases={n + i: i for i in range(n)},
        scratch_shapes=[pltpu.SemaphoreType.DMA((3 * n,)), pltpu.SemaphoreType.DMA((3 * n,)),
                        pltpu.SemaphoreType.DMA((n,))],
    )(*partials, *stacks)


def _chip_partial(name, grad, theirs, core):
    _, rows, cols = grad.shape
    tr = _pick(rows, max(8, (1 << 20) // cols), 8)

    def body(core_ref, g_ref, t_ref, o_ref):
        o_ref[...] = (g_ref[...].astype(F32) + t_ref[...].astype(F32)).astype(o_ref.dtype)

    return pl.pallas_call(
        body, name=name,
        grid_spec=pltpu.PrefetchScalarGridSpec(
            num_scalar_prefetch=1, grid=(N_CHIP, rows // tr),
            in_specs=[pl.BlockSpec((None, tr, cols), lambda k, r, core_ref: (core_ref[0] * N_CHIP + k, r, 0)),
                      pl.BlockSpec((None, tr, cols), lambda k, r, core_ref: (k, r, 0))],
            out_specs=pl.BlockSpec((None, tr, cols), lambda k, r, core_ref: (k, r, 0))),
        out_shape=jax.ShapeDtypeStruct(theirs.shape, theirs.dtype),
        compiler_params=_params("parallel", "parallel"),
    )(core, grad, theirs)


def _matmul(name, a, b, a_spec, b_spec, grid, acc_shape, out_shape, out_specs, epilogue,
            ta=False, tb=False, extras=(), extra_specs=()):
    nk = grid[2]
    n_extra = len(extras)
    dims = (((0,) if ta else (1,), (1,) if tb else (0,)), ((), ()))

    def body(*refs):
        a_ref, b_ref = refs[:2]
        extra_refs = refs[2:2 + n_extra]
        out_refs, acc_ref = refs[2 + n_extra:-1], refs[-1]
        k = pl.program_id(2)

        @pl.when(k == 0)
        def _():
            acc_ref[...] = jnp.zeros_like(acc_ref)

        acc_ref[...] += lax.dot_general(a_ref[...].astype(MXU_DTYPE), b_ref[...].astype(MXU_DTYPE), dims,
                                        preferred_element_type=F32)

        @pl.when(k == nk - 1)
        def _():
            epilogue(acc_ref[...], extra_refs, out_refs)

    return pl.pallas_call(
        body, name=name, grid=grid,
        in_specs=[a_spec, b_spec, *extra_specs], out_specs=out_specs, out_shape=out_shape,
        scratch_shapes=[pltpu.VMEM(acc_shape, F32)],
        compiler_params=_params("parallel", "parallel", "arbitrary"),
    )(a, b, *extras)


def _store(acc, extra_refs, out_refs):
    out_refs[0][...] = acc.astype(out_refs[0].dtype)


def _store_bias(acc, extra_refs, out_refs):
    out_refs[0][...] = (acc + extra_refs[0][...]).astype(out_refs[0].dtype)


def _store_relu2(acc, extra_refs, out_refs):
    r = jnp.maximum(acc, 0.0)
    out_refs[0][...] = (r * r).astype(out_refs[0].dtype)
    out_refs[1][...] = acc.astype(out_refs[1].dtype)


def _store_relu2_grad(acc, extra_refs, out_refs):
    hid = extra_refs[0][...].astype(F32)
    out_refs[0][...] = (acc * (2.0 * jnp.maximum(hid, 0.0))).astype(out_refs[0].dtype)


def _mm_cols(name, a, wc, out_dtype, epilogue=_store, bias=None, n_out=1):
    s, kdim = a.shape
    _, _, n = wc.shape
    tm, tn, tk = _pick(s, MM_TM), _pick(n, MM_TN), _pick(kdim, MM_TK)
    per = n // tn
    extras, extra_specs = (), ()
    if bias is not None:
        extras, extra_specs = (bias,), (pl.BlockSpec((1, tn), lambda i, j, k: (0, j)),)
    out = jax.ShapeDtypeStruct((s, N_DEV * n), out_dtype)
    spec = pl.BlockSpec((tm, tn), lambda i, j, k: (i, j))
    return _matmul(
        name, a, wc,
        pl.BlockSpec((tm, tk), lambda i, j, k: (i, k)),
        pl.BlockSpec((None, tk, tn), lambda i, j, k: (j // per, k, j % per)),
        (s // tm, N_DEV * per, kdim // tk), (tm, tn),
        [out] * n_out, [spec] * n_out, epilogue, extras=extras, extra_specs=extra_specs)


def _mm_rows(name, a, wr, out_dtype, bias=None):
    s, kdim = a.shape
    w = wr.reshape(kdim, wr.shape[2])
    n = w.shape[1]
    tm, tn, tk = _pick(s, MM_TM), _pick(n, MM_TN), _pick(kdim, MM_TK)
    extras, extra_specs, epilogue = (), (), _store
    if bias is not None:
        extras, extra_specs, epilogue = (bias,), (pl.BlockSpec((1, tn), lambda i, j, k: (0, j)),), _store_bias
    return _matmul(
        name, a, w,
        pl.BlockSpec((tm, tk), lambda i, j, k: (i, k)),
        pl.BlockSpec((tk, tn), lambda i, j, k: (k, j)),
        (s // tm, n // tn, kdim // tk), (tm, tn),
        [jax.ShapeDtypeStruct((s, n), out_dtype)], [pl.BlockSpec((tm, tn), lambda i, j, k: (i, j))],
        epilogue, extras=extras, extra_specs=extra_specs)[0]


def _mm_cols_t(name, dy, wc, out_dtype):
    s, _ = dy.shape
    _, kdim, n = wc.shape
    tm, tn, tk = _pick(s, MM_TM), _pick(kdim, MM_TN), _pick(n, MM_TK)
    per = n // tk
    return _matmul(
        name, dy, wc,
        pl.BlockSpec((tm, tk), lambda i, j, k: (i, k)),
        pl.BlockSpec((None, tn, tk), lambda i, j, k: (k // per, j, k % per)),
        (s // tm, kdim // tn, N_DEV * per), (tm, tn),
        [jax.ShapeDtypeStruct((s, kdim), out_dtype)], [pl.BlockSpec((tm, tn), lambda i, j, k: (i, j))],
        _store, tb=True)[0]


def _mm_rows_t(name, dy, wr, out_dtype, epilogue=_store, extra=None):
    s, n = dy.shape
    w = wr.reshape(-1, n)
    kdim = w.shape[0]
    tm, tn, tk = _pick(s, MM_TM), _pick(kdim, MM_TN), _pick(n, MM_TK)
    extras, extra_specs = (), ()
    if extra is not None:
        extras, extra_specs = (extra,), (pl.BlockSpec((tm, tn), lambda i, j, k: (i, j)),)
    return _matmul(
        name, dy, w,
        pl.BlockSpec((tm, tk), lambda i, j, k: (i, k)),
        pl.BlockSpec((tn, tk), lambda i, j, k: (j, k)),
        (s // tm, kdim // tn, n // tk), (tm, tn),
        [jax.ShapeDtypeStruct((s, kdim), out_dtype)], [pl.BlockSpec((tm, tn), lambda i, j, k: (i, j))],
        epilogue, tb=True, extras=extras, extra_specs=extra_specs)[0]


def _grad_cols(name, h, dy, n):
    s, kdim = h.shape
    tm, tn, tk = _pick(kdim, MM_TM), _pick(n, MM_TN), _pick(s, MM_TK)
    per = n // tn
    return _matmul(
        name, h, dy,
        pl.BlockSpec((tk, tm), lambda i, j, k: (k, i)),
        pl.BlockSpec((tk, tn), lambda i, j, k: (k, j)),
        (kdim // tm, N_DEV * per, s // tk), (tm, tn),
        [jax.ShapeDtypeStruct((N_DEV, kdim, n), COMM_DTYPE)],
        [pl.BlockSpec((None, tm, tn), lambda i, j, k: (_slot(j // per), i, j % per))],
        _store, ta=True)[0]


def _grad_rows(name, v, dy, kk):
    s, kdim = v.shape
    n = dy.shape[1]
    tm, tn, tk = _pick(kk, MM_TM), _pick(n, MM_TN), _pick(s, MM_TK)
    per = kk // tm
    return _matmul(
        name, v, dy,
        pl.BlockSpec((tk, tm), lambda i, j, k: (k, i)),
        pl.BlockSpec((tk, tn), lambda i, j, k: (k, j)),
        (kdim // tm, n // tn, s // tk), (tm, tn),
        [jax.ShapeDtypeStruct((N_DEV, kk, n), COMM_DTYPE)],
        [pl.BlockSpec((None, tm, tn), lambda i, j, k: (_slot(i // per), i % per, j))],
        _store, ta=True)[0]


def _rms(v):
    return lax.rsqrt(jnp.mean(v * v, axis=-1, keepdims=True) + RMS_EPS)


def _colsum(v):
    return jnp.sum(v, axis=0, keepdims=True)


def _vec_spec(width):
    return pl.BlockSpec((1, width), lambda i: (0, 0))


def _residual_fwd(name, x, post=None, pre=None, target=None):
    s, d = x.shape
    ts = _pick(s, ROW_TILE, 8)
    row = pl.BlockSpec((ts, d), lambda i: (i, 0))
    ins, specs = [x], [row]
    if post is not None:
        ins += list(post)
        specs += [row, _vec_spec(d), _vec_spec(d)]
    if pre is not None:
        ins += list(pre)
        specs += [_vec_spec(d)] * 3
    if target is not None:
        ins.append(target)
        specs.append(row)
    outs, out_specs = [], []
    if post is not None:
        outs.append(jax.ShapeDtypeStruct((s, d), F32))
        out_specs.append(row)
    if pre is not None:
        outs.append(jax.ShapeDtypeStruct((s, d), ACT_DTYPE))
        out_specs.append(row)
    if target is not None:
        outs += [jax.ShapeDtypeStruct((s, d), F32), jax.ShapeDtypeStruct((8, LANE), F32)]
        out_specs += [row, pl.BlockSpec((8, LANE), lambda i: (0, 0))]

    def body(*refs):
        refs = list(refs)
        xv = refs.pop(0)[...]
        if post is not None:
            y_ref, gp_ref, gt_ref = refs[:3]
            del refs[:3]
        if pre is not None:
            g_ref, sc_ref, sh_ref = refs[:3]
            del refs[:3]
        if target is not None:
            t_ref = refs.pop(0)
        if post is not None:
            yv = y_ref[...]
            xv = xv + gt_ref[...] * ((yv * _rms(yv)) * gp_ref[...])
            refs.pop(0)[...] = xv
        if pre is not None:
            hv = ((xv * _rms(xv)) * g_ref[...]) * (1.0 + sc_ref[...]) + sh_ref[...]
            refs.pop(0)[...] = hv.astype(ACT_DTYPE)
        if target is not None:
            dx_ref, loss_ref = refs
            err = xv - t_ref[...]
            dx_ref[...] = err * (1.0 / d)

            @pl.when(pl.program_id(0) == 0)
            def _():
                loss_ref[...] = jnp.zeros_like(loss_ref)

            loss_ref[...] += (0.5 / d) * jnp.sum(err * err)

    return pl.pallas_call(
        body, name=name, grid=(s // ts,), in_specs=specs, out_specs=out_specs, out_shape=outs,
        compiler_params=_params("arbitrary"),
    )(*ins)


def _residual_bwd(name, g_out, pre=None, post=None):
    s, d = g_out.shape
    ts = _pick(s, ROW_TILE, 8)
    row = pl.BlockSpec((ts, d), lambda i: (i, 0))
    ins, specs = [g_out], [row]
    outs, out_specs = [], []
    if pre is not None:
        ins += list(pre)
        specs += [row, row, _vec_spec(d), _vec_spec(d)]
        outs.append(jax.ShapeDtypeStruct((s, d), F32))
        out_specs.append(row)
    if post is not None:
        ins += list(post)
        specs += [row, _vec_spec(d), _vec_spec(d)]
        outs.append(jax.ShapeDtypeStruct((s, d), ACT_DTYPE))
        out_specs.append(row)
    outs.append(jax.ShapeDtypeStruct((8, d), F32))
    out_specs.append(pl.BlockSpec((8, d), lambda i: (0, 0)))

    def body(*refs):
        refs = list(refs)
        g = refs.pop(0)[...]
        if pre is not None:
            dh_ref, x_ref, gpre_ref, sc_ref = refs[:4]
            del refs[:4]
        if post is not None:
            y_ref, gpost_ref, gt_ref = refs[:3]
            del refs[:3]
        sums_ref = refs[-1]

        @pl.when(pl.program_id(0) == 0)
        def _():
            sums_ref[...] = jnp.zeros_like(sums_ref)

        if pre is not None:
            dh, xv = dh_ref[...], x_ref[...]
            r = _rms(xv)
            nrm = xv * r
            d_rn = dh * (1.0 + sc_ref[...])
            sums_ref[0:1, :] += _colsum(dh)
            sums_ref[1:2, :] += _colsum(dh * (nrm * gpre_ref[...]))
            sums_ref[2:3, :] += _colsum(d_rn * nrm)
            dn = d_rn * gpre_ref[...]
            g = g + r * (dn - nrm * jnp.mean(dn * nrm, axis=-1, keepdims=True))
            refs.pop(0)[...] = g
        if post is not None:
            yv = y_ref[...]
            r = _rms(yv)
            nrm = yv * r
            sums_ref[3:4, :] += _colsum(g * (nrm * gpost_ref[...]))
            d_o = g * gt_ref[...]
            sums_ref[4:5, :] += _colsum(d_o * nrm)
            dn = d_o * gpost_ref[...]
            dy = r * (dn - nrm * jnp.mean(dn * nrm, axis=-1, keepdims=True))
            sums_ref[5:6, :] += _colsum(dy)
            refs.pop(0)[...] = dy.astype(ACT_DTYPE)

    return pl.pallas_call(
        body, name=name, grid=(s // ts,), in_specs=specs, out_specs=out_specs, out_shape=outs,
        compiler_params=_params("arbitrary"),
    )(*ins)


def _glu(name, u):
    s, d2 = u.shape
    d = d2 // 2
    ts = _pick(s, ROW_TILE, 8)

    def body(u_ref, o_ref):
        o_ref[...] = u_ref[:, :d] * jax.nn.sigmoid(u_ref[:, d:])

    return pl.pallas_call(
        body, name=name, grid=(s // ts,),
        in_specs=[pl.BlockSpec((ts, d2), lambda i: (i, 0))],
        out_specs=pl.BlockSpec((ts, d), lambda i: (i, 0)),
        out_shape=jax.ShapeDtypeStruct((s, d), F32), compiler_params=_params("parallel"),
    )(u)


def _chunks(d):
    cw = min(CONV_CHUNK, d)
    return [(c * cw, cw) for c in range(d // cw)]


def _conv_ln_swish(name, glu, dw, dwb, ln_g, ln_b):
    s, d = glu.shape
    taps = dw.shape[0]
    ts, halo = _pick(s, CONV_TILE, 8), CONV_HALO
    lead = halo - (taps - 1)
    per = ts // halo

    def body(cur_ref, prev_ref, dw_ref, dwb_ref, g_ref, b_ref, cv_ref, v_ref, buf):
        i = pl.program_id(0)
        buf[0:halo, :] = jnp.where(i > 0, prev_ref[...], 0.0)
        buf[halo:, :] = cur_ref[...]
        for c0, cw in _chunks(d):
            acc = jnp.zeros((ts, cw), F32)
            for k in range(taps):
                acc += dw_ref[k:k + 1, c0:c0 + cw] * buf[lead + k:lead + k + ts, c0:c0 + cw]
            cv_ref[:, c0:c0 + cw] = acc + dwb_ref[:, c0:c0 + cw]
        cv = cv_ref[...]
        mu = jnp.mean(cv, axis=-1, keepdims=True)
        xc = cv - mu
        var = jnp.mean(xc * xc, axis=-1, keepdims=True)
        ln = (xc * lax.rsqrt(var + LN_EPS)) * g_ref[...] + b_ref[...]
        v_ref[...] = (ln * jax.nn.sigmoid(ln)).astype(v_ref.dtype)

    row = pl.BlockSpec((ts, d), lambda i: (i, 0))
    return pl.pallas_call(
        body, name=name, grid=(s // ts,),
        in_specs=[row, pl.BlockSpec((halo, d), lambda i: (jnp.maximum(i * per - 1, 0), 0)),
                  pl.BlockSpec((taps, d), lambda i: (0, 0)), _vec_spec(d), _vec_spec(d), _vec_spec(d)],
        out_specs=[row, row],
        out_shape=[jax.ShapeDtypeStruct((s, d), F32), jax.ShapeDtypeStruct((s, d), ACT_DTYPE)],
        scratch_shapes=[pltpu.VMEM((ts + halo, d), F32)],
        compiler_params=_params("parallel"),
    )(glu, glu, dw, dwb, ln_g, ln_b)


def _ln_swish_bwd(name, dv, cv, ln_g, ln_b):
    s, d = cv.shape
    ts = _pick(s, ROW_TILE, 8)

    def body(dv_ref, cv_ref, g_ref, b_ref, dcv_ref, sums_ref):
        @pl.when(pl.program_id(0) == 0)
        def _():
            sums_ref[...] = jnp.zeros_like(sums_ref)

        cv = cv_ref[...]
        mu = jnp.mean(cv, axis=-1, keepdims=True)
        xc = cv - mu
        rstd = lax.rsqrt(jnp.mean(xc * xc, axis=-1, keepdims=True) + LN_EPS)
        nhat = xc * rstd
        ln = nhat * g_ref[...] + b_ref[...]
        sg = jax.nn.sigmoid(ln)
        dl = dv_ref[...] * (sg * (1.0 + ln * (1.0 - sg)))
        sums_ref[0:1, :] += _colsum(dl * nhat)
        sums_ref[1:2, :] += _colsum(dl)
        dn = dl * g_ref[...]
        dcv = rstd * (dn - jnp.mean(dn, axis=-1, keepdims=True)
                      - nhat * jnp.mean(dn * nhat, axis=-1, keepdims=True))
        sums_ref[2:3, :] += _colsum(dcv)
        dcv_ref[...] = dcv

    row = pl.BlockSpec((ts, d), lambda i: (i, 0))
    return pl.pallas_call(
        body, name=name, grid=(s // ts,),
        in_specs=[row, row, _vec_spec(d), _vec_spec(d)],
        out_specs=[row, pl.BlockSpec((8, d), lambda i: (0, 0))],
        out_shape=[jax.ShapeDtypeStruct((s, d), F32), jax.ShapeDtypeStruct((8, d), F32)],
        compiler_params=_params("arbitrary"),
    )(dv, cv, ln_g, ln_b)


def _conv_glu_bwd(name, dcv, glu, u, dw):
    s, d = glu.shape
    taps = dw.shape[0]
    taps8 = -(-taps // 8) * 8
    ts, halo = _pick(s, CONV_TILE, 8), CONV_HALO
    lead = halo - (taps - 1)
    per = ts // halo
    n_tiles = s // ts

    def body(dcv_ref, next_ref, glu_ref, prev_ref, u_ref, dw_ref, du_ref, ddw_ref, sums_ref, nbuf, pbuf):
        i = pl.program_id(0)

        @pl.when(i == 0)
        def _():
            ddw_ref[...] = jnp.zeros_like(ddw_ref)
            sums_ref[...] = jnp.zeros_like(sums_ref)

        nbuf[0:ts, :] = dcv_ref[...]
        nbuf[ts:, :] = jnp.where(i < n_tiles - 1, next_ref[...], 0.0)
        pbuf[0:halo, :] = jnp.where(i > 0, prev_ref[...], 0.0)
        pbuf[halo:, :] = glu_ref[...]
        for c0, cw in _chunks(d):
            dcv = dcv_ref[:, c0:c0 + cw]
            dglu = jnp.zeros((ts, cw), F32)
            for k in range(taps):
                dglu += dw_ref[k:k + 1, c0:c0 + cw] * nbuf[taps - 1 - k:taps - 1 - k + ts, c0:c0 + cw]
                ddw_ref[k:k + 1, c0:c0 + cw] += _colsum(dcv * pbuf[lead + k:lead + k + ts, c0:c0 + cw])
            a = u_ref[:, c0:c0 + cw]
            sg = jax.nn.sigmoid(u_ref[:, d + c0:d + c0 + cw])
            da = dglu * sg
            dg = dglu * a * (sg * (1.0 - sg))
            du_ref[:, c0:c0 + cw] = da.astype(du_ref.dtype)
            du_ref[:, d + c0:d + c0 + cw] = dg.astype(du_ref.dtype)
            sums_ref[0:1, c0:c0 + cw] += _colsum(da)
            sums_ref[0:1, d + c0:d + c0 + cw] += _colsum(dg)

    row = pl.BlockSpec((ts, d), lambda i: (i, 0))
    wide = pl.BlockSpec((ts, 2 * d), lambda i: (i, 0))
    return pl.pallas_call(
        body, name=name, grid=(n_tiles,),
        in_specs=[row, pl.BlockSpec((halo, d), lambda i: (jnp.minimum((i + 1) * per, s // halo - 1), 0)),
                  row, pl.BlockSpec((halo, d), lambda i: (jnp.maximum(i * per - 1, 0), 0)),
                  wide, pl.BlockSpec((taps, d), lambda i: (0, 0))],
        out_specs=[wide, pl.BlockSpec((taps8, d), lambda i: (0, 0)), pl.BlockSpec((8, 2 * d), lambda i: (0, 0))],
        out_shape=[jax.ShapeDtypeStruct((s, 2 * d), ACT_DTYPE), jax.ShapeDtypeStruct((taps8, d), F32),
                   jax.ShapeDtypeStruct((8, 2 * d), F32)],
        scratch_shapes=[pltpu.VMEM((ts + halo, d), F32), pltpu.VMEM((ts + halo, d), F32)],
        compiler_params=_params("arbitrary"),
    )(dcv, dcv, glu, glu, u, dw)


def _short_conv_fwd(name, z, w):
    s, d3 = z.shape
    d = d3 // 3
    taps = w.shape[0]
    ts, halo = _pick(s, CONV_TILE, 8), SHORT_HALO
    lead = halo - (taps - 1)
    per = ts // halo

    def body(z_ref, prev_ref, w_ref, q_ref, pbuf):
        i = pl.program_id(0)
        pbuf[0:halo, :] = jnp.where(i > 0, prev_ref[:, d:2 * d] * prev_ref[:, 2 * d:], 0.0)
        pbuf[halo:, :] = z_ref[:, d:2 * d] * z_ref[:, 2 * d:]
        for c0, cw in _chunks(d):
            acc = jnp.zeros((ts, cw), F32)
            for k in range(taps):
                acc += w_ref[k:k + 1, c0:c0 + cw] * pbuf[lead + k:lead + k + ts, c0:c0 + cw]
            q_ref[:, c0:c0 + cw] = (z_ref[:, c0:c0 + cw] * acc).astype(q_ref.dtype)

    return pl.pallas_call(
        body, name=name, grid=(s // ts,),
        in_specs=[pl.BlockSpec((ts, d3), lambda i: (i, 0)),
                  pl.BlockSpec((halo, d3), lambda i: (jnp.maximum(i * per - 1, 0), 0)),
                  pl.BlockSpec((taps, d), lambda i: (0, 0))],
        out_specs=pl.BlockSpec((ts, d), lambda i: (i, 0)),
        out_shape=jax.ShapeDtypeStruct((s, d), ACT_DTYPE),
        scratch_shapes=[pltpu.VMEM((ts + halo, d), F32)],
        compiler_params=_params("parallel"),
    )(z, z, w)


def _short_conv_bwd(name, dq, z, w):
    s, d3 = z.shape
    d = d3 // 3
    taps = w.shape[0]
    ts, halo = _pick(s, CONV_TILE, 8), SHORT_HALO
    lead = halo - (taps - 1)
    per = ts // halo
    n_tiles = s // ts

    def body(dq_ref, dqn_ref, z_ref, zp_ref, zn_ref, w_ref, dz_ref, sums_ref, pbuf, ubuf):
        i = pl.program_id(0)

        @pl.when(i == 0)
        def _():
            sums_ref[...] = jnp.zeros_like(sums_ref)

        pbuf[0:halo, :] = jnp.where(i > 0, zp_ref[:, d:2 * d] * zp_ref[:, 2 * d:], 0.0)
        pbuf[halo:, :] = z_ref[:, d:2 * d] * z_ref[:, 2 * d:]
        ubuf[0:ts, :] = dq_ref[...] * z_ref[:, 0:d]
        ubuf[ts:, :] = jnp.where(i < n_tiles - 1, dqn_ref[...] * zn_ref[:, 0:d], 0.0)
        for c0, cw in _chunks(d):
            du = ubuf[0:ts, c0:c0 + cw]
            conv = jnp.zeros((ts, cw), F32)
            dp = jnp.zeros((ts, cw), F32)
            for k in range(taps):
                wk = w_ref[k:k + 1, c0:c0 + cw]
                shifted = pbuf[lead + k:lead + k + ts, c0:c0 + cw]
                conv += wk * shifted
                dp += wk * ubuf[taps - 1 - k:taps - 1 - k + ts, c0:c0 + cw]
                sums_ref[k:k + 1, c0:c0 + cw] += _colsum(du * shifted)
            dz_ref[:, c0:c0 + cw] = (dq_ref[:, c0:c0 + cw] * conv).astype(dz_ref.dtype)
            dz_ref[:, d + c0:d + c0 + cw] = (dp * z_ref[:, 2 * d + c0:2 * d + c0 + cw]).astype(dz_ref.dtype)
            dz_ref[:, 2 * d + c0:2 * d + c0 + cw] = (dp * z_ref[:, d + c0:d + c0 + cw]).astype(dz_ref.dtype)

    last = s // halo - 1
    return pl.pallas_call(
        body, name=name, grid=(n_tiles,),
        in_specs=[pl.BlockSpec((ts, d), lambda i: (i, 0)),
                  pl.BlockSpec((halo, d), lambda i: (jnp.minimum((i + 1) * per, last), 0)),
                  pl.BlockSpec((ts, d3), lambda i: (i, 0)),
                  pl.BlockSpec((halo, d3), lambda i: (jnp.maximum(i * per - 1, 0), 0)),
                  pl.BlockSpec((halo, d3), lambda i: (jnp.minimum((i + 1) * per, last), 0)),
                  pl.BlockSpec((taps, d), lambda i: (0, 0))],
        out_specs=[pl.BlockSpec((ts, d3), lambda i: (i, 0)), pl.BlockSpec((8, d), lambda i: (0, 0))],
        out_shape=[jax.ShapeDtypeStruct((s, d3), ACT_DTYPE), jax.ShapeDtypeStruct((8, d), F32)],
        scratch_shapes=[pltpu.VMEM((ts + halo, d), F32), pltpu.VMEM((ts + halo, d), F32)],
        compiler_params=_params("arbitrary"),
    )(dq, dq, z, z, z, w)


def _silu(v):
    return v * jax.nn.sigmoid(v)


def _modulation(name, c_all, mod_w, mod_b_cols):
    nl, d, n = mod_w.shape
    b = c_all.shape[0]
    tn = _pick(n, 512)

    def body(c_ref, w_ref, b_ref, o_ref):
        ca = _silu(c_ref[...]).astype(MXU_DTYPE)
        o_ref[...] = jnp.dot(ca, w_ref[...].astype(MXU_DTYPE), preferred_element_type=F32) + b_ref[...]

    return pl.pallas_call(
        body, name=name, grid=(nl, n // tn),
        in_specs=[pl.BlockSpec((b, d), lambda l, j: (0, 0)),
                  pl.BlockSpec((None, d, tn), lambda l, j: (l, 0, j)),
                  pl.BlockSpec((None, 1, tn), lambda l, j: (l, 0, j))],
        out_specs=pl.BlockSpec((None, b, tn), lambda l, j: (l, 0, j)),
        out_shape=jax.ShapeDtypeStruct((nl, b, n), F32),
        compiler_params=_params("parallel", "parallel"),
    )(c_all, mod_w, mod_b_cols.reshape(nl, 1, n))


def _adamw(g, w, m, v):
    m = ADAM_B1 * m + (1.0 - ADAM_B1) * g
    v = ADAM_B2 * v + (1.0 - ADAM_B2) * (g * g)
    m_hat = m / (1.0 - ADAM_B1 ** ADAM_STEP)
    v_hat = v / (1.0 - ADAM_B2 ** ADAM_STEP)
    delta = -ADAM_LR * (m_hat / (jnp.sqrt(v_hat) + ADAM_EPS) + ADAM_WD * w)
    return delta, m, v


def _write_update(g, w_ref, m_ref, v_ref, outs):
    delta, m, v = _adamw(g, w_ref[...], m_ref[...], v_ref[...])
    outs[0][...] = g
    outs[1][...] = delta
    outs[2][...] = m
    outs[3][...] = v


def _modulation_update(name, c_all_t, dmod, w, m, v):
    nl, d, n = w.shape
    b = c_all_t.shape[1]
    tr = _pick(d, 256, 8)

    def body(c_ref, dm_ref, w_ref, m_ref, v_ref, *outs):
        ca = _silu(c_ref[...])
        dm = dm_ref[...]
        g = ca[:, 0:1] * dm[0:1, :]
        for i in range(1, b):
            g += ca[:, i:i + 1] * dm[i:i + 1, :]
        _write_update(g, w_ref, m_ref, v_ref, outs)

    blk = pl.BlockSpec((None, tr, n), lambda l, r: (l, r, 0))
    return pl.pallas_call(
        body, name=name, grid=(nl, d // tr),
        in_specs=[pl.BlockSpec((tr, b), lambda l, r: (r, 0)), pl.BlockSpec((None, b, n), lambda l, r: (l, 0, 0)),
                  blk, blk, blk],
        out_specs=[blk] * 4, out_shape=[jax.ShapeDtypeStruct(w.shape, F32)] * 4,
        compiler_params=_params("parallel", "parallel"),
    )(c_all_t, dmod, w, m, v)


def _weight_update(name, parts, w, m, v):
    nl, rows, cols = w.shape
    tr = _pick(rows, max(8, (1 << 19) // cols), 8)

    def body(p_ref, w_ref, m_ref, v_ref, *outs):
        g = p_ref[0].astype(F32)
        for k in range(1, N_CHIP):
            g += p_ref[k].astype(F32)
        _write_update(g, w_ref, m_ref, v_ref, outs)

    blk = pl.BlockSpec((None, tr, cols), lambda l, r: (l, r, 0))
    return pl.pallas_call(
        body, name=name, grid=(nl, rows // tr),
        in_specs=[pl.BlockSpec((None, N_CHIP, tr, cols), lambda l, r: (l, 0, r, 0)), blk, blk, blk],
        out_specs=[blk] * 4, out_shape=[jax.ShapeDtypeStruct(w.shape, F32)] * 4,
        compiler_params=_params("parallel", "parallel"),
    )(parts, w, m, v)


def _vector_update(name, gathered, w, m, v):
    _, rows, cols = gathered.shape
    rw = w.shape[0]

    def body(g_ref, w_ref, m_ref, v_ref, tot_ref, d_ref, m2_ref, v2_ref):
        tot = g_ref[0]
        for k in range(1, N_DEV):
            tot += g_ref[k]
        tot_ref[...] = tot
        delta, m2, v2 = _adamw(tot[0:rw], w_ref[...], m_ref[...], v_ref[...])
        d_ref[...] = delta
        m2_ref[...] = m2
        v2_ref[...] = v2

    vm = pl.BlockSpec(memory_space=pltpu.VMEM)
    return pl.pallas_call(
        body, name=name, in_specs=[vm] * 4, out_specs=[vm] * 4,
        out_shape=[jax.ShapeDtypeStruct((rows, cols), F32)] + [jax.ShapeDtypeStruct((rw, cols), F32)] * 3,
        compiler_params=pltpu.CompilerParams(vmem_limit_bytes=V7X_VMEM_LIMIT_BYTES),
    )(gathered, w, m, v)


def _plain_update(name, g, w, m, v):
    def body(g_ref, w_ref, m_ref, v_ref, d_ref, m2_ref, v2_ref):
        delta, m2, v2 = _adamw(g_ref[...], w_ref[...], m_ref[...], v_ref[...])
        d_ref[...] = delta
        m2_ref[...] = m2
        v2_ref[...] = v2

    vm = pl.BlockSpec(memory_space=pltpu.VMEM)
    return pl.pallas_call(
        body, name=name, in_specs=[vm] * 4, out_specs=[vm] * 3,
        out_shape=[jax.ShapeDtypeStruct(w.shape, F32)] * 3,
    )(g, w, m, v)


def _rows(a, width, mult=8):
    r = a.reshape(-1, width)
    pad = -r.shape[0] % mult
    return jnp.pad(r, ((0, pad), (0, 0))) if pad else r


def _pack(blocks, width):
    parts, spans, at = [], [], 0
    for a in blocks:
        n = a.size // width
        p = _rows(a, width)
        parts.append(p)
        spans.append((at, n))
        at += p.shape[0]
    return jnp.concatenate(parts, axis=0), spans


def kernel(x, c, mod_w, mod_b, pre_mix_g, post_mix_g, pre_ffn_g, post_ffn_g, a_w1, a_b1, a_dw, a_dwb, a_ln_g, a_ln_b, a_w2, a_b2, b_w_in, b_conv, b_w_out, f_w1, f_w2, loss_target, m_mod_w, m_mod_b, m_pre_mix_g, m_post_mix_g, m_pre_ffn_g, m_post_ffn_g, m_a_w1, m_a_b1, m_a_dw, m_a_dwb, m_a_ln_g, m_a_ln_b, m_a_w2, m_a_b2, m_b_w_in, m_b_conv, m_b_w_out, m_f_w1, m_f_w2, v_mod_w, v_mod_b, v_pre_mix_g, v_post_mix_g, v_pre_ffn_g, v_post_ffn_g, v_a_w1, v_a_b1, v_a_dw, v_a_dwb, v_a_ln_g, v_a_ln_b, v_a_w2, v_a_b2, v_b_w_in, v_b_conv, v_b_w_out, v_f_w1, v_f_w2):
    depth, d = pre_mix_g.shape
    n_a, n_b = a_w1.shape[0], b_w_in.shape[0]
    s = x.shape[1]
    dsh = d // N_DEV
    taps_a, taps_b = a_dw.shape[1], b_conv.shape[1]
    px, py, pc = _pos()
    me = 4 * px + 2 * py + pc
    core = jnp.reshape(pc, (1,)).astype(jnp.int32)
    x0 = x.reshape(s, d)
    target = loss_target.reshape(s, d)

    packed0, spans0 = _pack([c, a_dw, b_conv], dsh)
    got0 = _small_gather("gather_cond", packed0)

    def full_width(span):
        at, n = span
        return jnp.transpose(got0[:, at:at + n, :], (1, 0, 2)).reshape(n, d)

    c_all = got0[:, spans0[0][0]:spans0[0][0] + spans0[0][1], :].reshape(N_DEV, d)
    a_dw_full = full_width(spans0[1]).reshape(n_a, taps_a, d)
    b_conv_full = full_width(spans0[2]).reshape(n_b, taps_b, d)

    n_mod = mod_w.shape[2]
    mod_b_cols = lax.dynamic_slice_in_dim(mod_b, me * n_mod, n_mod, axis=1)
    mod_local = _modulation("modulation", c_all, mod_w, mod_b_cols)
    got1 = _small_gather("gather_mod", mod_local.reshape(depth * N_DEV, n_mod))
    mod_me = lax.dynamic_index_in_dim(got1.reshape(N_DEV, depth, N_DEV, n_mod), me, axis=2, keepdims=False)
    mod_me = jnp.transpose(mod_me, (1, 0, 2)).reshape(depth, 6, 1, d)

    def vec(a, i):
        return a[i].reshape(1, -1)

    cast = lambda w: w.astype(COMM_DTYPE)
    saved = []
    xs = x0
    h = _residual_fwd("fwd_in", xs, pre=(vec(pre_mix_g, 0), mod_me[0, 1], mod_me[0, 0]))[0]
    for i in range(depth):
        j = i // 2
        sh_m, sc_m, gt_m, sh_f, sc_f, gt_f = [mod_me[i, q] for q in range(6)]
        keep = {"x_mix": xs, "h_mix": h}
        if i % 2 == 0:
            w1g, w2g, fw1g, fw2g = _gather_weights(
                "gather_weights_a", [cast(a_w1[j]), cast(a_w2[j]), cast(f_w1[i]), cast(f_w2[i])])
            u = _mm_cols("a_w1", h, w1g, F32, _store_bias, bias=vec(a_b1, j))[0]
            glu = _glu("a_glu", u)
            cv, v = _conv_ln_swish("a_conv", glu, a_dw_full[j], vec(a_dwb, j), vec(a_ln_g, j), vec(a_ln_b, j))
            y = _mm_rows("a_w2", v, w2g, F32, bias=vec(a_b2, j))
            keep.update(u=u, glu=glu, cv=cv, v=v)
        else:
            w1g, w2g, fw1g, fw2g = _gather_weights(
                "gather_weights_b", [cast(b_w_in[j]), cast(b_w_out[j]), cast(f_w1[i]), cast(f_w2[i])])
            z = _mm_cols("b_w_in", h, w1g, F32)[0]
            q = _short_conv_fwd("b_conv", z, b_conv_full[j])
            y = _mm_rows("b_w_out", q, w2g, F32)
            keep.update(z=z, q=q)
        keep.update(w1g=w1g, w2g=w2g, fw1g=fw1g, fw2g=fw2g, y_mix=y)
        xs, h = _residual_fwd("fwd_mid", xs, post=(y, vec(post_mix_g, i), gt_m),
                              pre=(vec(pre_ffn_g, i), sc_f, sh_f))
        keep.update(x_ffn=xs, h_ffn=h)
        act, hid = _mm_cols("f_w1", h, fw1g, ACT_DTYPE, _store_relu2, n_out=2)
        y = _mm_rows("f_w2", act, fw2g, F32)
        keep.update(act=act, hid=hid, y_ffn=y)
        saved.append(keep)
        if i + 1 < depth:
            nxt = mod_me[i + 1]
            xs, h = _residual_fwd("fwd_next", xs, post=(y, vec(post_ffn_g, i), gt_f),
                                  pre=(vec(pre_mix_g, i + 1), nxt[1], nxt[0]))
        else:
            _, g, loss_blk = _residual_fwd("fwd_loss", xs, post=(y, vec(post_ffn_g, i), gt_f), target=target)

    def landing(w):
        return jnp.zeros((w.shape[0], N_CHIP) + w.shape[1:], COMM_DTYPE)

    land = {"a_w1": landing(a_w1), "a_w2": landing(a_w2), "b_w_in": landing(b_w_in),
            "b_w_out": landing(b_w_out), "f_w1": landing(f_w1), "f_w2": landing(f_w2)}
    zero_vec = jnp.zeros((d,), F32)
    dmod = [[zero_vec] * 6 for _ in range(depth)]
    small = {name: [None] * depth for name in ("pre_mix_g", "post_mix_g", "pre_ffn_g", "post_ffn_g")}
    small_a = {name: [None] * n_a for name in ("a_b1", "a_dwb", "a_ln_g", "a_ln_b", "a_b2", "a_dw")}
    small_b = {"b_conv": [None] * n_b}

    dy, sums = _residual_bwd("bwd_top", g, post=(saved[-1]["y_ffn"], vec(post_ffn_g, depth - 1), mod_me[depth - 1, 5]))
    for i in reversed(range(depth)):
        j = i // 2
        kp = saved[i]
        sh_m, sc_m, gt_m, sh_f, sc_f, gt_f = [mod_me[i, q] for q in range(6)]
        dmod[i][5] = sums[3]
        small["post_ffn_g"][i] = sums[4]
        dhid = _mm_rows_t("f_w2_t", dy, kp["fw2g"], ACT_DTYPE, _store_relu2_grad, extra=kp["hid"])
        g_fw2 = _grad_rows("f_w2_grad", kp["act"], dy, f_w2.shape[1])
        dh = _mm_cols_t("f_w1_t", dhid, kp["fw1g"], F32)
        g_fw1 = _grad_cols("f_w1_grad", kp["h_ffn"], dhid, f_w1.shape[2])
        g, dy, sums = _residual_bwd("bwd_mid", g, pre=(dh, kp["x_ffn"], vec(pre_ffn_g, i), sc_f),
                                    post=(kp["y_mix"], vec(post_mix_g, i), gt_m))
        dmod[i][3], dmod[i][4], dmod[i][2] = sums[0], sums[1], sums[3]
        small["pre_ffn_g"][i], small["post_mix_g"][i] = sums[2], sums[4]
        if i % 2 == 0:
            small_a["a_b2"][j] = sums[5]
            dv = _mm_rows_t("a_w2_t", dy, kp["w2g"], F32)
            g_w2 = _grad_rows("a_w2_grad", kp["v"], dy, a_w2.shape[1])
            dcv, lsum = _ln_swish_bwd("a_ln_bwd", dv, kp["cv"], vec(a_ln_g, j), vec(a_ln_b, j))
            small_a["a_ln_g"][j], small_a["a_ln_b"][j], small_a["a_dwb"][j] = lsum[0], lsum[1], lsum[2]
            du, ddw, usum = _conv_glu_bwd("a_conv_bwd", dcv, kp["glu"], kp["u"], a_dw_full[j])
            small_a["a_dw"][j], small_a["a_b1"][j] = ddw[:taps_a], usum[0]
            dh = _mm_cols_t("a_w1_t", du, kp["w1g"], F32)
            g_w1 = _grad_cols("a_w1_grad", kp["h_mix"], du, a_w1.shape[2])
            names = ("a_w1", "a_w2")
        else:
            dq = _mm_rows_t("b_w_out_t", dy, kp["w2g"], F32)
            g_w2 = _grad_rows("b_w_out_grad", kp["q"], dy, b_w_out.shape[1])
            dz, wsum = _short_conv_bwd("b_conv_bwd", dq, kp["z"], b_conv_full[j])
            small_b["b_conv"][j] = wsum[:taps_b]
            dh = _mm_cols_t("b_w_in_t", dz, kp["w1g"], F32)
            g_w1 = _grad_cols("b_w_in_grad", kp["h_mix"], dz, b_w_in.shape[2])
            names = ("b_w_in", "b_w_out")
        if i > 0:
            prev = saved[i - 1]
            g, dy, sums = _residual_bwd("bwd_next", g, pre=(dh, kp["x_mix"], vec(pre_mix_g, i), sc_m),
                                        post=(prev["y_ffn"], vec(post_ffn_g, i - 1), mod_me[i - 1, 5]))
        else:
            g, sums = _residual_bwd("bwd_in", g, pre=(dh, kp["x_mix"], vec(pre_mix_g, i), sc_m))
        dmod[i][0], dmod[i][1] = sums[0], sums[1]
        small["pre_mix_g"][i] = sums[2]
        grads = [g_w1, g_w2, g_fw1, g_fw2]
        kind = "a" if i % 2 == 0 else "b"
        theirs = _sibling_exchange("sibling_exchange_" + kind, grads)
        partials = [_chip_partial("chip_partial_%s%d" % (kind, q), gq, tq, core)
                    for q, (gq, tq) in enumerate(zip(grads, theirs))]
        keys = (names[0], names[1], "f_w1", "f_w2")
        new = _chip_exchange("chip_exchange_%s%d" % (kind, i), partials, [land[k] for k in keys], (j, j, i, i))
        for k, buf in zip(keys, new):
            land[k] = buf
    grad_x = g.reshape(x.shape)

    reps = [jnp.stack([jnp.stack(r) for r in dmod]),
            jnp.stack(small["pre_mix_g"]), jnp.stack(small["post_mix_g"]),
            jnp.stack(small["pre_ffn_g"]), jnp.stack(small["post_ffn_g"]),
            jnp.stack(small_a["a_b1"]), jnp.stack(small_a["a_dwb"]), jnp.stack(small_a["a_ln_g"]),
            jnp.stack(small_a["a_ln_b"]), jnp.stack(small_a["a_b2"])]
    rep_w = [mod_b, pre_mix_g, post_mix_g, pre_ffn_g, post_ffn_g, a_b1, a_dwb, a_ln_g, a_ln_b, a_b2]
    rep_m = [m_mod_b, m_pre_mix_g, m_post_mix_g, m_pre_ffn_g, m_post_ffn_g, m_a_b1, m_a_dwb, m_a_ln_g, m_a_ln_b, m_a_b2]
    rep_v = [v_mod_b, v_pre_mix_g, v_post_mix_g, v_pre_ffn_g, v_post_ffn_g, v_a_b1, v_a_dwb, v_a_ln_g, v_a_ln_b, v_a_b2]
    w_small, spans = _pack(rep_w, d)
    m_small, _ = _pack(rep_m, d)
    v_small, _ = _pack(rep_v, d)
    loss_row = jnp.pad(loss_blk[0:1, 0:1], ((0, 0), (0, d - 1)))
    part_small, spans_g = _pack(reps + [jnp.stack(small_a["a_dw"]), jnp.stack(small_b["b_conv"]), loss_row], d)
    got2 = _small_gather("gather_small", part_small)
    total, d_small, m2_small, v2_small = _vector_update("vector_update", got2, w_small, m_small, v_small)

    def unpack(buf, span, like):
        return buf[span[0]:span[0] + span[1]].reshape(like.shape)

    rep_out = [[unpack(buf, sp, w) for sp, w in zip(spans, rep_w)] for buf in (total, d_small, m2_small, v2_small)]
    loss = total[spans_g[-1][0], 0]

    conv_g = [lax.dynamic_slice_in_dim(total[sp[0]:sp[0] + sp[1]], me * dsh, dsh, axis=1)
              for sp in spans_g[len(reps):len(reps) + 2]]
    gc, _ = _pack(conv_g, dsh)
    wc, spans_c = _pack([a_dw, b_conv], dsh)
    mc, _ = _pack([m_a_dw, m_b_conv], dsh)
    vc, _ = _pack([v_a_dw, v_b_conv], dsh)
    conv_out = [gc] + list(_plain_update("conv_update", gc, wc, mc, vc))
    conv_out = [[unpack(buf, sp, w) for sp, w in zip(spans_c, (a_dw, b_conv))] for buf in conv_out]

    at, n = spans_g[0]
    dmod_all = got2[:, at:at + n, :].reshape(N_DEV, depth, 6 * d)
    dmod_cols = jnp.transpose(lax.dynamic_slice_in_dim(dmod_all, me * n_mod, n_mod, axis=2), (1, 0, 2))
    mod_out = _modulation_update("modulation_update", jnp.transpose(c_all), dmod_cols, mod_w, m_mod_w, v_mod_w)

    big = {"a_w1": (a_w1, m_a_w1, v_a_w1), "a_w2": (a_w2, m_a_w2, v_a_w2),
           "b_w_in": (b_w_in, m_b_w_in, v_b_w_in), "b_w_out": (b_w_out, m_b_w_out, v_b_w_out),
           "f_w1": (f_w1, m_f_w1, v_f_w1), "f_w2": (f_w2, m_f_w2, v_f_w2)}
    big_out = {k: _weight_update("update_" + k, land[k], *wmv) for k, wmv in big.items()}

    def family(q):
        rep = dict(zip(("mod_b", "pre_mix_g", "post_mix_g", "pre_ffn_g", "post_ffn_g",
                        "a_b1", "a_dwb", "a_ln_g", "a_ln_b", "a_b2"), rep_out[q]))
        return (mod_out[q], rep["mod_b"], rep["pre_mix_g"], rep["post_mix_g"], rep["pre_ffn_g"], rep["post_ffn_g"],
                big_out["a_w1"][q], rep["a_b1"], conv_out[q][0], rep["a_dwb"], rep["a_ln_g"], rep["a_ln_b"],
                big_out["a_w2"][q], rep["a_b2"], big_out["b_w_in"][q], conv_out[q][1], big_out["b_w_out"][q],
                big_out["f_w1"][q], big_out["f_w2"][q])

    return (loss, grad_x, *family(0), *family(1), *family(2), *family(3))
```

```python
import jax
import jax.numpy as jnp
from jax import lax
from jax.experimental import pallas as pl
from jax.experimental.pallas import tpu as pltpu

MXU_DTYPE = jnp.bfloat16
ACT_DTYPE = jnp.bfloat16
COMM_DTYPE = jnp.bfloat16

N_DEV = 8
N_CHIP = 4
RMS_EPS = 1e-6
LN_EPS = 1e-5
ADAM_LR = 0.001
ADAM_B1 = 0.9
ADAM_B2 = 0.999
ADAM_EPS = 1e-08
ADAM_WD = 0.01
ADAM_STEP = 10

V7X_VMEM_LIMIT_BYTES = 56 * 1024 * 1024
LANE = 128
ROW_TILE = 256
CONV_TILE = 128
CONV_HALO = 32
SHORT_HALO = 8
CONV_CHUNK = 256
MM_TM, MM_TN, MM_TK = 1024, 1024, 2048

F32 = jnp.float32
MESH = pl.DeviceIdType.MESH


def _pick(dim, pref, mult=LANE):
    if dim <= pref:
        return dim
    t = pref - pref % mult
    while dim % t:
        t -= mult
    return t


def _params(*sem):
    return pltpu.CompilerParams(dimension_semantics=sem, vmem_limit_bytes=V7X_VMEM_LIMIT_BYTES)


def _pos():
    return lax.axis_index("x"), lax.axis_index("y"), lax.axis_index("c")


def _flip(v, bit):
    return 1 - v if bit else v


def _slot(j):
    return (j % 2) * N_CHIP + j // 2


def _small_gather(name, v, deps=()):
    rows, cols = v.shape
    n_deps = len(deps)

    def body(v_ref, *rest):
        out_ref, send_sems, recv_sems, local_sem = rest[n_deps:]
        x, y, c = _pos()
        me, sibling = (x, y, c), (x, y, 1 - c)
        chips = [(1 - x, y), (x, 1 - y), (1 - x, 1 - y)]

        def block(px, py, pc):
            return out_ref.at[4 * px + 2 * py + pc]

        def copy(k, owner, to, src=None):
            return pltpu.make_async_remote_copy(
                src_ref=block(*owner) if src is None else src, dst_ref=block(*owner), send_sem=send_sems.at[k],
                recv_sem=recv_sems.at[k], device_id=to, device_id_type=MESH)

        mine = pltpu.make_async_copy(v_ref, block(*me), local_sem)
        mine.start()
        first = [copy(0, me, sibling, src=v_ref)]
        first += [copy(1 + j, me, (*chip, c), src=v_ref) for j, chip in enumerate(chips)]
        for cp in first:
            cp.start()
        passed = [copy(4 + j, (*chip, c), sibling) for j, chip in enumerate(chips)]
        for j, chip in enumerate(chips):
            copy(1 + j, (*chip, c), me).wait_recv()
            passed[j].start()
        copy(0, sibling, me).wait_recv()
        for j, chip in enumerate(chips):
            copy(4 + j, (*chip, 1 - c), me).wait_recv()
        for cp in first + passed:
            cp.wait_send()
        mine.wait()

    return pl.pallas_call(
        body, name=name,
        out_shape=jax.ShapeDtypeStruct((N_DEV, rows, cols), v.dtype),
        in_specs=[pl.BlockSpec(memory_space=pltpu.VMEM)] + [pl.BlockSpec(memory_space=pl.ANY)] * n_deps,
        out_specs=pl.BlockSpec(memory_space=pltpu.VMEM),
        scratch_shapes=[pltpu.SemaphoreType.DMA((N_DEV - 1,)), pltpu.SemaphoreType.DMA((N_DEV - 1,)),
                        pltpu.SemaphoreType.DMA],
        compiler_params=pltpu.CompilerParams(vmem_limit_bytes=V7X_VMEM_LIMIT_BYTES),
    )(v, *deps)


HBM_SPEC = pl.BlockSpec(memory_space=pltpu.HBM)
SEM_SPEC = pl.BlockSpec(memory_space=pltpu.SEMAPHORE)
ANY_SPEC = pl.BlockSpec(memory_space=pl.ANY)
TOKEN = jax.ShapeDtypeStruct((8, LANE), F32)
TOKEN_SPEC = pl.BlockSpec(memory_space=pltpu.VMEM)
SPLIT_COPY = pltpu.CompilerParams(has_side_effects=pltpu.SideEffectType.DATAFLOW_SIDE_EFFECTING)


def _hbm(a):
    return pltpu.with_memory_space_constraint(a, pltpu.HBM)


def _dma_sems(n):
    return pltpu.SemaphoreType.DMA((n,))


def _place(name, w, layer, me):
    _, rows, cols = w.shape
    tr = _pick(rows, max(8, (1 << 20) // cols), 8)

    def body(me_ref, w_ref, o_ref):
        o_ref[...] = w_ref[...].astype(o_ref.dtype)

    return pl.pallas_call(
        body, name=name,
        grid_spec=pltpu.PrefetchScalarGridSpec(
            num_scalar_prefetch=1, grid=(rows // tr,),
            in_specs=[pl.BlockSpec((None, tr, cols), lambda r, me_ref: (layer, r, 0))],
            out_specs=pl.BlockSpec((None, tr, cols), lambda r, me_ref: (me_ref[0], r, 0))),
        out_shape=jax.ShapeDtypeStruct((N_DEV, rows, cols), COMM_DTYPE),
        compiler_params=_params("parallel"),
    )(me, w)


def _block_copy(buf, owner, to, send_sem, recv_sem):
    blk = buf.at[4 * owner[0] + 2 * owner[1] + owner[2]]
    return pltpu.make_async_remote_copy(src_ref=blk, dst_ref=blk, send_sem=send_sem, recv_sem=recv_sem,
                                        device_id=to, device_id_type=MESH)


def _other_chips(x, y):
    return [(1 - x, y), (x, 1 - y), (1 - x, 1 - y)]


def _gather_start(tag, bufs, after=()):
    n = len(bufs)
    n_in = n + len(after)

    def body(*refs):
        b = refs[:n]
        send, recv_ici, recv_sib = refs[n_in:n_in + 3]
        token = refs[-1]
        x, y, c = _pos()
        me = (x, y, c)
        for i in range(n):
            _block_copy(b[i], me, (x, y, 1 - c), send.at[4 * i], recv_sib.at[i]).start()
            for j, chip in enumerate(_other_chips(x, y)):
                _block_copy(b[i], me, (*chip, c), send.at[4 * i + 1 + j], recv_ici.at[3 * i + j]).start()
        token[...] = jnp.zeros_like(token)

    outs = pl.pallas_call(
        body, name="gather_start_" + tag,
        out_shape=(_dma_sems(4 * n), _dma_sems(3 * n), _dma_sems(n),
                   *[pltpu.HBM(b.shape, b.dtype) for b in bufs], TOKEN),
        in_specs=[HBM_SPEC] * n + [ANY_SPEC] * len(after),
        out_specs=(SEM_SPEC, SEM_SPEC, SEM_SPEC, *[HBM_SPEC] * n, TOKEN_SPEC),
        input_output_aliases={i: 3 + i for i in range(n)}, compiler_params=SPLIT_COPY,
    )(*[_hbm(b) for b in bufs], *after)
    return dict(send=outs[0], recv_ici=outs[1], recv_sib=outs[2], bufs=list(outs[3:3 + n]), token=outs[-1])


def _gather_pass(tag, h, after):
    bufs = h["bufs"]
    n = len(bufs)

    def body(*refs):
        b = refs[:n]
        recv_ici = refs[n]
        fsend, frecv = refs[n + 2:n + 4]
        token = refs[-1]
        x, y, c = _pos()
        for i in range(n):
            for j, chip in enumerate(_other_chips(x, y)):
                _block_copy(b[i], (*chip, c), (x, y, c), fsend.at[3 * i + j], recv_ici.at[3 * i + j]).wait_recv()
                _block_copy(b[i], (*chip, c), (x, y, 1 - c), fsend.at[3 * i + j], frecv.at[3 * i + j]).start()
        token[...] = jnp.zeros_like(token)

    outs = pl.pallas_call(
        body, name="gather_pass_" + tag,
        out_shape=(_dma_sems(3 * n), _dma_sems(3 * n), *[pltpu.HBM(b.shape, b.dtype) for b in bufs], TOKEN),
        in_specs=[HBM_SPEC] * n + [SEM_SPEC, ANY_SPEC],
        out_specs=(SEM_SPEC, SEM_SPEC, *[HBM_SPEC] * n, TOKEN_SPEC),
        input_output_aliases={i: 2 + i for i in range(n)}, compiler_params=SPLIT_COPY,
    )(*bufs, h["recv_ici"], after)
    return dict(h, fsend=outs[0], frecv=outs[1], bufs=list(outs[2:2 + n]), token=outs[-1])


def _gather_wait(tag, h, after):
    bufs = h["bufs"]
    n = len(bufs)

    def body(*refs):
        b = refs[:n]
        send, recv_sib, fsend, frecv = refs[n:n + 4]
        x, y, c = _pos()
        me, sibling = (x, y, c), (x, y, 1 - c)
        for i in range(n):
            _block_copy(b[i], sibling, me, send.at[4 * i], recv_sib.at[i]).wait_recv()
            for j, chip in enumerate(_other_chips(x, y)):
                _block_copy(b[i], (*chip, 1 - c), me, fsend.at[3 * i + j], frecv.at[3 * i + j]).wait_recv()
            for k in range(4):
                _block_copy(b[i], me, sibling, send.at[4 * i + k], recv_sib.at[i]).wait_send()
            for j, chip in enumerate(_other_chips(x, y)):
                _block_copy(b[i], (*chip, c), sibling, fsend.at[3 * i + j], frecv.at[3 * i + j]).wait_send()

    outs = pl.pallas_call(
        body, name="gather_wait_" + tag,
        out_shape=tuple(pltpu.HBM(b.shape, b.dtype) for b in bufs),
        in_specs=[HBM_SPEC] * n + [SEM_SPEC] * 4 + [ANY_SPEC], out_specs=tuple([HBM_SPEC] * n),
        input_output_aliases={i: i for i in range(n)}, compiler_params=SPLIT_COPY,
    )(*bufs, h["send"], h["recv_sib"], h["fsend"], h["frecv"], after)
    return list(outs)


def _sibling_copy(grad, land, send_sem, recv_sem):
    x, y, c = _pos()
    return pltpu.make_async_remote_copy(
        src_ref=grad.at[pl.ds((1 - c) * N_CHIP, N_CHIP)], dst_ref=land, send_sem=send_sem, recv_sem=recv_sem,
        device_id=(x, y, 1 - c), device_id_type=MESH)


def _sibling_start(tag, grads):
    n = len(grads)
    landings = [lax.empty((N_CHIP,) + g.shape[1:], g.dtype) for g in grads]

    def body(*refs):
        g, land = refs[:n], refs[n:2 * n]
        send, recv = refs[2 * n:2 * n + 2]
        token = refs[-1]
        for i in range(n):
            _sibling_copy(g[i], land[i], send.at[i], recv.at[i]).start()
        token[...] = jnp.zeros_like(token)

    both = list(grads) + landings
    outs = pl.pallas_call(
        body, name="sibling_start_" + tag,
        out_shape=(_dma_sems(n), _dma_sems(n), *[pltpu.HBM(b.shape, b.dtype) for b in both], TOKEN),
        in_specs=[HBM_SPEC] * (2 * n), out_specs=(SEM_SPEC, SEM_SPEC, *[HBM_SPEC] * (2 * n), TOKEN_SPEC),
        input_output_aliases={i: 2 + i for i in range(2 * n)}, compiler_params=SPLIT_COPY,
    )(*[_hbm(b) for b in both])
    return dict(send=outs[0], recv=outs[1], grads=list(outs[2:2 + n]), landings=list(outs[2 + n:2 + 2 * n]),
                token=outs[-1])


def _sibling_wait(tag, h, after):
    n = len(h["grads"])

    def body(*refs):
        g, land = refs[:n], refs[n:2 * n]
        send, recv = refs[2 * n:2 * n + 2]
        for i in range(n):
            _sibling_copy(g[i], land[i], send.at[i], recv.at[i]).wait()

    both = h["grads"] + h["landings"]
    outs = pl.pallas_call(
        body, name="sibling_wait_" + tag,
        out_shape=tuple(pltpu.HBM(b.shape, b.dtype) for b in both),
        in_specs=[HBM_SPEC] * (2 * n) + [SEM_SPEC, SEM_SPEC, ANY_SPEC], out_specs=tuple([HBM_SPEC] * (2 * n)),
        input_output_aliases={i: i for i in range(2 * n)}, compiler_params=SPLIT_COPY,
    )(*both, h["send"], h["recv"], after)
    return list(outs[:n]), list(outs[n:])


def _reduce_copy(part, land, x, y, c, k, src_chip, send_sem, recv_sem):
    px, py = _flip(x, k >> 1 & 1), _flip(y, k & 1)
    return pltpu.make_async_remote_copy(src_ref=part.at[2 * px + py], dst_ref=land.at[src_chip], send_sem=send_sem,
                                        recv_sem=recv_sem, device_id=(px, py, c), device_id_type=MESH)


def _reduce_start(tag, partials, landings):
    n = len(partials)

    def body(*refs):
        p, land = refs[:n], refs[n:2 * n]
        send, recv = refs[2 * n:2 * n + 2]
        token = refs[-1]
        x, y, c = _pos()
        for i in range(n):
            for k in range(1, N_CHIP):
                _reduce_copy(p[i], land[i], x, y, c, k, 2 * x + y, send.at[3 * i + k - 1], recv.at[3 * i + k - 1]).start()
        token[...] = jnp.zeros_like(token)

    both = list(partials) + list(landings)
    outs = pl.pallas_call(
        body, name="reduce_start_" + tag,
        out_shape=(_dma_sems(3 * n), _dma_sems(3 * n), *[pltpu.HBM(b.shape, b.dtype) for b in both], TOKEN),
        in_specs=[HBM_SPEC] * (2 * n), out_specs=(SEM_SPEC, SEM_SPEC, *[HBM_SPEC] * (2 * n), TOKEN_SPEC),
        input_output_aliases={i: 2 + i for i in range(2 * n)}, compiler_params=SPLIT_COPY,
    )(*[_hbm(b) for b in both])
    return dict(send=outs[0], recv=outs[1], partials=list(outs[2:2 + n]), landings=list(outs[2 + n:2 + 2 * n]),
                token=outs[-1])


def _reduce_wait(tag, h, after):
    n = len(h["partials"])

    def body(*refs):
        p, land = refs[:n], refs[n:2 * n]
        send, recv = refs[2 * n:2 * n + 2]
        x, y, c = _pos()
        for i in range(n):
            for k in range(1, N_CHIP):
                src_chip = 2 * _flip(x, k >> 1 & 1) + _flip(y, k & 1)
                cp = _reduce_copy(p[i], land[i], x, y, c, k, src_chip, send.at[3 * i + k - 1], recv.at[3 * i + k - 1])
                cp.wait_recv()
                cp.wait_send()

    both = h["partials"] + h["landings"]
    outs = pl.pallas_call(
        body, name="reduce_wait_" + tag,
        out_shape=tuple(pltpu.HBM(b.shape, b.dtype) for b in both),
        in_specs=[HBM_SPEC] * (2 * n) + [SEM_SPEC, SEM_SPEC, ANY_SPEC], out_specs=tuple([HBM_SPEC] * (2 * n)),
        input_output_aliases={i: i for i in range(2 * n)}, compiler_params=SPLIT_COPY,
    )(*both, h["send"], h["recv"], after)
    return list(outs[n:])


def _chip_partial(name, grad, theirs, where):
    _, rows, cols = grad.shape
    tr = _pick(rows, max(8, (1 << 20) // cols), 8)
    landing = jnp.zeros(theirs.shape, theirs.dtype)

    def body(where_ref, g_ref, t_ref, land_in, o_ref, land_ref):
        total = (g_ref[...].astype(F32) + t_ref[...].astype(F32)).astype(o_ref.dtype)
        o_ref[...] = total

        @pl.when(pl.program_id(1) == where_ref[1])
        def _():
            land_ref[...] = total

    return pl.pallas_call(
        body, name=name,
        grid_spec=pltpu.PrefetchScalarGridSpec(
            num_scalar_prefetch=1, grid=(rows // tr, N_CHIP),
            in_specs=[pl.BlockSpec((None, tr, cols), lambda r, k, w: (w[0] * N_CHIP + k, r, 0)),
                      pl.BlockSpec((None, tr, cols), lambda r, k, w: (k, r, 0)), ANY_SPEC],
            out_specs=[pl.BlockSpec((None, tr, cols), lambda r, k, w: (k, r, 0)),
                       pl.BlockSpec((None, tr, cols), lambda r, k, w: (w[1], r, 0))]),
        out_shape=[jax.ShapeDtypeStruct(theirs.shape, theirs.dtype)] * 2,
        input_output_aliases={3: 1},
        compiler_params=_params("parallel", "arbitrary"),
    )(where, grad, theirs, landing)


def _matmul(name, a, b, a_spec, b_spec, grid, acc_shape, out_shape, out_specs, epilogue,
            ta=False, tb=False, extras=(), extra_specs=(), deps=()):
    nk = grid[2]
    n_extra = len(extras)
    n_in = 2 + n_extra + len(deps)
    dims = (((0,) if ta else (1,), (1,) if tb else (0,)), ((), ()))

    def body(*refs):
        a_ref, b_ref = refs[:2]
        extra_refs = refs[2:2 + n_extra]
        out_refs = refs[n_in:n_in + len(out_shape)]

        def product():
            return lax.dot_general(a_ref[...].astype(MXU_DTYPE), b_ref[...].astype(MXU_DTYPE), dims,
                                   preferred_element_type=F32)

        if nk == 1:
            epilogue(product(), extra_refs, out_refs)
            return
        acc_ref = refs[-1]
        k = pl.program_id(2)

        @pl.when(k == 0)
        def _():
            acc_ref[...] = product()

        if nk > 2:
            @pl.when((k > 0) & (k < nk - 1))
            def _():
                acc_ref[...] += product()

        @pl.when(k == nk - 1)
        def _():
            epilogue(acc_ref[...] + product(), extra_refs, out_refs)

    return pl.pallas_call(
        body, name=name, grid=grid,
        in_specs=[a_spec, b_spec, *extra_specs, *[ANY_SPEC] * len(deps)], out_specs=out_specs, out_shape=out_shape,
        scratch_shapes=[pltpu.VMEM(acc_shape, F32)] if nk > 1 else [],
        compiler_params=_params("parallel", "parallel", "arbitrary"),
    )(a, b, *extras, *deps)


def _store(acc, extra_refs, out_refs):
    out_refs[0][...] = acc.astype(out_refs[0].dtype)


def _store_bias(acc, extra_refs, out_refs):
    out_refs[0][...] = (acc + extra_refs[0][...]).astype(out_refs[0].dtype)


def _store_relu2(acc, extra_refs, out_refs):
    r = jnp.maximum(acc, 0.0)
    out_refs[0][...] = (r * r).astype(out_refs[0].dtype)
    out_refs[1][...] = acc.astype(out_refs[1].dtype)


def _store_relu2_grad(acc, extra_refs, out_refs):
    hid = extra_refs[0][...].astype(F32)
    out_refs[0][...] = (acc * (2.0 * jnp.maximum(hid, 0.0))).astype(out_refs[0].dtype)


def _mm_cols(name, a, wc, out_dtype, epilogue=_store, bias=None, n_out=1, deps=()):
    s, kdim = a.shape
    _, _, n = wc.shape
    tm, tn, tk = _pick(s, MM_TM), _pick(n, MM_TN), _pick(kdim, MM_TK)
    per = n // tn
    extras, extra_specs = (), ()
    if bias is not None:
        extras, extra_specs = (bias,), (pl.BlockSpec((1, tn), lambda i, j, k: (0, j)),)
    out = jax.ShapeDtypeStruct((s, N_DEV * n), out_dtype)
    spec = pl.BlockSpec((tm, tn), lambda i, j, k: (i, j))
    return _matmul(
        name, a, wc,
        pl.BlockSpec((tm, tk), lambda i, j, k: (i, k)),
        pl.BlockSpec((None, tk, tn), lambda i, j, k: (j // per, k, j % per)),
        (s // tm, N_DEV * per, kdim // tk), (tm, tn),
        [out] * n_out, [spec] * n_out, epilogue, extras=extras, extra_specs=extra_specs, deps=deps)


def _mm_rows(name, a, wr, out_dtype, bias=None, deps=()):
    s, kdim = a.shape
    w = wr.reshape(kdim, wr.shape[2])
    n = w.shape[1]
    tm, tn, tk = _pick(s, MM_TM), _pick(n, MM_TN), _pick(kdim, MM_TK)
    extras, extra_specs, epilogue = (), (), _store
    if bias is not None:
        extras, extra_specs, epilogue = (bias,), (pl.BlockSpec((1, tn), lambda i, j, k: (0, j)),), _store_bias
    return _matmul(
        name, a, w,
        pl.BlockSpec((tm, tk), lambda i, j, k: (i, k)),
        pl.BlockSpec((tk, tn), lambda i, j, k: (k, j)),
        (s // tm, n // tn, kdim // tk), (tm, tn),
        [jax.ShapeDtypeStruct((s, n), out_dtype)], [pl.BlockSpec((tm, tn), lambda i, j, k: (i, j))],
        epilogue, extras=extras, extra_specs=extra_specs, deps=deps)[0]


def _mm_cols_t(name, dy, wc, out_dtype, deps=()):
    s, _ = dy.shape
    _, kdim, n = wc.shape
    tm, tn, tk = _pick(s, MM_TM), _pick(kdim, MM_TN), _pick(n, MM_TK)
    per = n // tk
    return _matmul(
        name, dy, wc,
        pl.BlockSpec((tm, tk), lambda i, j, k: (i, k)),
        pl.BlockSpec((None, tn, tk), lambda i, j, k: (k // per, j, k % per)),
        (s // tm, kdim // tn, N_DEV * per), (tm, tn),
        [jax.ShapeDtypeStruct((s, kdim), out_dtype)], [pl.BlockSpec((tm, tn), lambda i, j, k: (i, j))],
        _store, tb=True, deps=deps)[0]


def _mm_rows_t(name, dy, wr, out_dtype, epilogue=_store, extra=None, deps=()):
    s, n = dy.shape
    w = wr.reshape(-1, n)
    kdim = w.shape[0]
    tm, tn, tk = _pick(s, MM_TM), _pick(kdim, MM_TN), _pick(n, MM_TK)
    extras, extra_specs = (), ()
    if extra is not None:
        extras, extra_specs = (extra,), (pl.BlockSpec((tm, tn), lambda i, j, k: (i, j)),)
    return _matmul(
        name, dy, w,
        pl.BlockSpec((tm, tk), lambda i, j, k: (i, k)),
        pl.BlockSpec((tn, tk), lambda i, j, k: (j, k)),
        (s // tm, kdim // tn, n // tk), (tm, tn),
        [jax.ShapeDtypeStruct((s, kdim), out_dtype)], [pl.BlockSpec((tm, tn), lambda i, j, k: (i, j))],
        epilogue, tb=True, extras=extras, extra_specs=extra_specs, deps=deps)[0]


def _grad_cols(name, h, dy, n):
    s, kdim = h.shape
    tm, tn, tk = _pick(kdim, MM_TM), _pick(n, MM_TN), _pick(s, MM_TK)
    per = n // tn
    return _matmul(
        name, h, dy,
        pl.BlockSpec((tk, tm), lambda i, j, k: (k, i)),
        pl.BlockSpec((tk, tn), lambda i, j, k: (k, j)),
        (kdim // tm, N_DEV * per, s // tk), (tm, tn),
        [jax.ShapeDtypeStruct((N_DEV, kdim, n), COMM_DTYPE)],
        [pl.BlockSpec((None, tm, tn), lambda i, j, k: (_slot(j // per), i, j % per))],
        _store, ta=True)[0]


def _grad_rows(name, v, dy, kk):
    s, kdim = v.shape
    n = dy.shape[1]
    tm, tn, tk = _pick(kk, MM_TM), _pick(n, MM_TN), _pick(s, MM_TK)
    per = kk // tm
    return _matmul(
        name, v, dy,
        pl.BlockSpec((tk, tm), lambda i, j, k: (k, i)),
        pl.BlockSpec((tk, tn), lambda i, j, k: (k, j)),
        (kdim // tm, n // tn, s // tk), (tm, tn),
        [jax.ShapeDtypeStruct((N_DEV, kk, n), COMM_DTYPE)],
        [pl.BlockSpec((None, tm, tn), lambda i, j, k: (_slot(i // per), i % per, j))],
        _store, ta=True)[0]


def _rms(v):
    return lax.rsqrt(jnp.mean(v * v, axis=-1, keepdims=True) + RMS_EPS)


def _colsum(v):
    return jnp.sum(v, axis=0, keepdims=True)


def _vec_spec(width):
    return pl.BlockSpec((1, width), lambda i: (0, 0))


def _residual_fwd(name, x, post=None, pre=None, target=None, deps=()):
    s, d = x.shape
    ts = _pick(s, ROW_TILE, 8)
    row = pl.BlockSpec((ts, d), lambda i: (i, 0))
    ins, specs = [x], [row]
    if post is not None:
        ins += list(post)
        specs += [row, _vec_spec(d), _vec_spec(d)]
    if pre is not None:
        ins += list(pre)
        specs += [_vec_spec(d)] * 3
    if target is not None:
        ins.append(target)
        specs.append(row)
    ins += list(deps)
    specs += [ANY_SPEC] * len(deps)
    outs, out_specs = [], []
    if post is not None:
        outs.append(jax.ShapeDtypeStruct((s, d), F32))
        out_specs.append(row)
    if pre is not None:
        outs.append(jax.ShapeDtypeStruct((s, d), ACT_DTYPE))
        out_specs.append(row)
    if target is not None:
        outs += [jax.ShapeDtypeStruct((s, d), F32), jax.ShapeDtypeStruct((8, LANE), F32)]
        out_specs += [row, pl.BlockSpec((8, LANE), lambda i: (0, 0))]

    def body(*refs):
        refs = list(refs)
        xv = refs.pop(0)[...]
        if post is not None:
            y_ref, gp_ref, gt_ref = refs[:3]
            del refs[:3]
        if pre is not None:
            g_ref, sc_ref, sh_ref = refs[:3]
            del refs[:3]
        if target is not None:
            t_ref = refs.pop(0)
        del refs[:len(deps)]
        if post is not None:
            yv = y_ref[...]
            xv = xv + gt_ref[...] * ((yv * _rms(yv)) * gp_ref[...])
            refs.pop(0)[...] = xv
        if pre is not None:
            hv = ((xv * _rms(xv)) * g_ref[...]) * (1.0 + sc_ref[...]) + sh_ref[...]
            refs.pop(0)[...] = hv.astype(ACT_DTYPE)
        if target is not None:
            dx_ref, loss_ref = refs
            err = xv - t_ref[...]
            dx_ref[...] = err * (1.0 / d)

            @pl.when(pl.program_id(0) == 0)
            def _():
                loss_ref[...] = jnp.zeros_like(loss_ref)

            loss_ref[...] += (0.5 / d) * jnp.sum(err * err)

    return pl.pallas_call(
        body, name=name, grid=(s // ts,), in_specs=specs, out_specs=out_specs, out_shape=outs,
        compiler_params=_params("arbitrary"),
    )(*ins)


def _residual_bwd(name, g_out, pre=None, post=None):
    s, d = g_out.shape
    ts = _pick(s, ROW_TILE, 8)
    row = pl.BlockSpec((ts, d), lambda i: (i, 0))
    ins, specs = [g_out], [row]
    outs, out_specs = [], []
    if pre is not None:
        ins += list(pre)
        specs += [row, row, _vec_spec(d), _vec_spec(d)]
        outs.append(jax.ShapeDtypeStruct((s, d), F32))
        out_specs.append(row)
    if post is not None:
        ins += list(post)
        specs += [row, _vec_spec(d), _vec_spec(d)]
        outs.append(jax.ShapeDtypeStruct((s, d), ACT_DTYPE))
        out_specs.append(row)
    outs.append(jax.ShapeDtypeStruct((8, d), F32))
    out_specs.append(pl.BlockSpec((8, d), lambda i: (0, 0)))

    def body(*refs):
        refs = list(refs)
        g = refs.pop(0)[...]
        if pre is not None:
            dh_ref, x_ref, gpre_ref, sc_ref = refs[:4]
            del refs[:4]
        if post is not None:
            y_ref, gpost_ref, gt_ref = refs[:3]
            del refs[:3]
        sums_ref = refs[-1]

        @pl.when(pl.program_id(0) == 0)
        def _():
            sums_ref[...] = jnp.zeros_like(sums_ref)

        if pre is not None:
            dh, xv = dh_ref[...], x_ref[...]
            r = _rms(xv)
            nrm = xv * r
            d_rn = dh * (1.0 + sc_ref[...])
            sums_ref[0:1, :] += _colsum(dh)
            sums_ref[1:2, :] += _colsum(dh * (nrm * gpre_ref[...]))
            sums_ref[2:3, :] += _colsum(d_rn * nrm)
            dn = d_rn * gpre_ref[...]
            g = g + r * (dn - nrm * jnp.mean(dn * nrm, axis=-1, keepdims=True))
            refs.pop(0)[...] = g
        if post is not None:
            yv = y_ref[...]
            r = _rms(yv)
            nrm = yv * r
            sums_ref[3:4, :] += _colsum(g * (nrm * gpost_ref[...]))
            d_o = g * gt_ref[...]
            sums_ref[4:5, :] += _colsum(d_o * nrm)
            dn = d_o * gpost_ref[...]
            dy = r * (dn - nrm * jnp.mean(dn * nrm, axis=-1, keepdims=True))
            sums_ref[5:6, :] += _colsum(dy)
            refs.pop(0)[...] = dy.astype(ACT_DTYPE)

    return pl.pallas_call(
        body, name=name, grid=(s // ts,), in_specs=specs, out_specs=out_specs, out_shape=outs,
        compiler_params=_params("arbitrary"),
    )(*ins)


def _glu(name, u):
    s, d2 = u.shape
    d = d2 // 2
    ts = _pick(s, ROW_TILE, 8)

    def body(u_ref, o_ref):
        o_ref[...] = u_ref[:, :d] * jax.nn.sigmoid(u_ref[:, d:])

    return pl.pallas_call(
        body, name=name, grid=(s // ts,),
        in_specs=[pl.BlockSpec((ts, d2), lambda i: (i, 0))],
        out_specs=pl.BlockSpec((ts, d), lambda i: (i, 0)),
        out_shape=jax.ShapeDtypeStruct((s, d), F32), compiler_params=_params("parallel"),
    )(u)


def _chunks(d):
    cw = min(CONV_CHUNK, d)
    return [(c * cw, cw) for c in range(d // cw)]


def _conv_ln_swish(name, glu, dw, dwb, ln_g, ln_b):
    s, d = glu.shape
    taps = dw.shape[0]
    ts, halo = _pick(s, CONV_TILE, 8), CONV_HALO
    lead = halo - (taps - 1)
    per = ts // halo

    def body(cur_ref, prev_ref, dw_ref, dwb_ref, g_ref, b_ref, cv_ref, v_ref, buf):
        i = pl.program_id(0)
        buf[0:halo, :] = jnp.where(i > 0, prev_ref[...], 0.0)
        buf[halo:, :] = cur_ref[...]
        for c0, cw in _chunks(d):
            acc = jnp.zeros((ts, cw), F32)
            for k in range(taps):
                acc += dw_ref[k:k + 1, c0:c0 + cw] * buf[lead + k:lead + k + ts, c0:c0 + cw]
            cv_ref[:, c0:c0 + cw] = acc + dwb_ref[:, c0:c0 + cw]
        cv = cv_ref[...]
        mu = jnp.mean(cv, axis=-1, keepdims=True)
        xc = cv - mu
        var = jnp.mean(xc * xc, axis=-1, keepdims=True)
        ln = (xc * lax.rsqrt(var + LN_EPS)) * g_ref[...] + b_ref[...]
        v_ref[...] = (ln * jax.nn.sigmoid(ln)).astype(v_ref.dtype)

    row = pl.BlockSpec((ts, d), lambda i: (i, 0))
    return pl.pallas_call(
        body, name=name, grid=(s // ts,),
        in_specs=[row, pl.BlockSpec((halo, d), lambda i: (jnp.maximum(i * per - 1, 0), 0)),
                  pl.BlockSpec((taps, d), lambda i: (0, 0)), _vec_spec(d), _vec_spec(d), _vec_spec(d)],
        out_specs=[row, row],
        out_shape=[jax.ShapeDtypeStruct((s, d), F32), jax.ShapeDtypeStruct((s, d), ACT_DTYPE)],
        scratch_shapes=[pltpu.VMEM((ts + halo, d), F32)],
        compiler_params=_params("parallel"),
    )(glu, glu, dw, dwb, ln_g, ln_b)


def _ln_swish_bwd(name, dv, cv, ln_g, ln_b, deps=()):
    s, d = cv.shape
    ts = _pick(s, ROW_TILE, 8)

    def body(dv_ref, cv_ref, g_ref, b_ref, *rest):
        dcv_ref, sums_ref = rest[len(deps):]

        @pl.when(pl.program_id(0) == 0)
        def _():
            sums_ref[...] = jnp.zeros_like(sums_ref)

        cv = cv_ref[...]
        mu = jnp.mean(cv, axis=-1, keepdims=True)
        xc = cv - mu
        rstd = lax.rsqrt(jnp.mean(xc * xc, axis=-1, keepdims=True) + LN_EPS)
        nhat = xc * rstd
        ln = nhat * g_ref[...] + b_ref[...]
        sg = jax.nn.sigmoid(ln)
        dl = dv_ref[...] * (sg * (1.0 + ln * (1.0 - sg)))
        sums_ref[0:1, :] += _colsum(dl * nhat)
        sums_ref[1:2, :] += _colsum(dl)
        dn = dl * g_ref[...]
        dcv = rstd * (dn - jnp.mean(dn, axis=-1, keepdims=True)
                      - nhat * jnp.mean(dn * nhat, axis=-1, keepdims=True))
        sums_ref[2:3, :] += _colsum(dcv)
        dcv_ref[...] = dcv

    row = pl.BlockSpec((ts, d), lambda i: (i, 0))
    return pl.pallas_call(
        body, name=name, grid=(s // ts,),
        in_specs=[row, row, _vec_spec(d), _vec_spec(d)] + [ANY_SPEC] * len(deps),
        out_specs=[row, pl.BlockSpec((8, d), lambda i: (0, 0))],
        out_shape=[jax.ShapeDtypeStruct((s, d), F32), jax.ShapeDtypeStruct((8, d), F32)],
        compiler_params=_params("arbitrary"),
    )(dv, cv, ln_g, ln_b, *deps)


def _conv_glu_bwd(name, dcv, glu, u, dw):
    s, d = glu.shape
    taps = dw.shape[0]
    taps8 = -(-taps // 8) * 8
    ts, halo = _pick(s, CONV_TILE, 8), CONV_HALO
    lead = halo - (taps - 1)
    per = ts // halo
    n_tiles = s // ts

    def body(dcv_ref, next_ref, glu_ref, prev_ref, u_ref, dw_ref, du_ref, ddw_ref, sums_ref, nbuf, pbuf):
        i = pl.program_id(0)

        @pl.when(i == 0)
        def _():
            ddw_ref[...] = jnp.zeros_like(ddw_ref)
            sums_ref[...] = jnp.zeros_like(sums_ref)

        nbuf[0:ts, :] = dcv_ref[...]
        nbuf[ts:, :] = jnp.where(i < n_tiles - 1, next_ref[...], 0.0)
        pbuf[0:halo, :] = jnp.where(i > 0, prev_ref[...], 0.0)
        pbuf[halo:, :] = glu_ref[...]
        for c0, cw in _chunks(d):
            dcv = dcv_ref[:, c0:c0 + cw]
            dglu = jnp.zeros((ts, cw), F32)
            for k in range(taps):
                dglu += dw_ref[k:k + 1, c0:c0 + cw] * nbuf[taps - 1 - k:taps - 1 - k + ts, c0:c0 + cw]
                ddw_ref[k:k + 1, c0:c0 + cw] += _colsum(dcv * pbuf[lead + k:lead + k + ts, c0:c0 + cw])
            a = u_ref[:, c0:c0 + cw]
            sg = jax.nn.sigmoid(u_ref[:, d + c0:d + c0 + cw])
            da = dglu * sg
            dg = dglu * a * (sg * (1.0 - sg))
            du_ref[:, c0:c0 + cw] = da.astype(du_ref.dtype)
            du_ref[:, d + c0:d + c0 + cw] = dg.astype(du_ref.dtype)
            sums_ref[0:1, c0:c0 + cw] += _colsum(da)
            sums_ref[0:1, d + c0:d + c0 + cw] += _colsum(dg)

    row = pl.BlockSpec((ts, d), lambda i: (i, 0))
    wide = pl.BlockSpec((ts, 2 * d), lambda i: (i, 0))
    return pl.pallas_call(
        body, name=name, grid=(n_tiles,),
        in_specs=[row, pl.BlockSpec((halo, d), lambda i: (jnp.minimum((i + 1) * per, s // halo - 1), 0)),
                  row, pl.BlockSpec((halo, d), lambda i: (jnp.maximum(i * per - 1, 0), 0)),
                  wide, pl.BlockSpec((taps, d), lambda i: (0, 0))],
        out_specs=[wide, pl.BlockSpec((taps8, d), lambda i: (0, 0)), pl.BlockSpec((8, 2 * d), lambda i: (0, 0))],
        out_shape=[jax.ShapeDtypeStruct((s, 2 * d), ACT_DTYPE), jax.ShapeDtypeStruct((taps8, d), F32),
                   jax.ShapeDtypeStruct((8, 2 * d), F32)],
        scratch_shapes=[pltpu.VMEM((ts + halo, d), F32), pltpu.VMEM((ts + halo, d), F32)],
        compiler_params=_params("arbitrary"),
    )(dcv, dcv, glu, glu, u, dw)


def _short_conv_fwd(name, z, w):
    s, d3 = z.shape
    d = d3 // 3
    taps = w.shape[0]
    ts, halo = _pick(s, CONV_TILE, 8), SHORT_HALO
    lead = halo - (taps - 1)
    per = ts // halo

    def body(z_ref, prev_ref, w_ref, q_ref, pbuf):
        i = pl.program_id(0)
        pbuf[0:halo, :] = jnp.where(i > 0, prev_ref[:, d:2 * d] * prev_ref[:, 2 * d:], 0.0)
        pbuf[halo:, :] = z_ref[:, d:2 * d] * z_ref[:, 2 * d:]
        for c0, cw in _chunks(d):
            acc = jnp.zeros((ts, cw), F32)
            for k in range(taps):
                acc += w_ref[k:k + 1, c0:c0 + cw] * pbuf[lead + k:lead + k + ts, c0:c0 + cw]
            q_ref[:, c0:c0 + cw] = (z_ref[:, c0:c0 + cw] * acc).astype(q_ref.dtype)

    return pl.pallas_call(
        body, name=name, grid=(s // ts,),
        in_specs=[pl.BlockSpec((ts, d3), lambda i: (i, 0)),
                  pl.BlockSpec((halo, d3), lambda i: (jnp.maximum(i * per - 1, 0), 0)),
                  pl.BlockSpec((taps, d), lambda i: (0, 0))],
        out_specs=pl.BlockSpec((ts, d), lambda i: (i, 0)),
        out_shape=jax.ShapeDtypeStruct((s, d), ACT_DTYPE),
        scratch_shapes=[pltpu.VMEM((ts + halo, d), F32)],
        compiler_params=_params("parallel"),
    )(z, z, w)


def _short_conv_bwd(name, dq, z, w, deps=()):
    s, d3 = z.shape
    d = d3 // 3
    taps = w.shape[0]
    ts, halo = _pick(s, CONV_TILE, 8), SHORT_HALO
    lead = halo - (taps - 1)
    per = ts // halo
    n_tiles = s // ts

    def body(dq_ref, dqn_ref, z_ref, zp_ref, zn_ref, w_ref, *rest):
        dz_ref, sums_ref, pbuf, ubuf = rest[len(deps):]
        i = pl.program_id(0)

        @pl.when(i == 0)
        def _():
            sums_ref[...] = jnp.zeros_like(sums_ref)

        pbuf[0:halo, :] = jnp.where(i > 0, zp_ref[:, d:2 * d] * zp_ref[:, 2 * d:], 0.0)
        pbuf[halo:, :] = z_ref[:, d:2 * d] * z_ref[:, 2 * d:]
        ubuf[0:ts, :] = dq_ref[...] * z_ref[:, 0:d]
        ubuf[ts:, :] = jnp.where(i < n_tiles - 1, dqn_ref[...] * zn_ref[:, 0:d], 0.0)
        for c0, cw in _chunks(d):
            du = ubuf[0:ts, c0:c0 + cw]
            conv = jnp.zeros((ts, cw), F32)
            dp = jnp.zeros((ts, cw), F32)
            for k in range(taps):
                wk = w_ref[k:k + 1, c0:c0 + cw]
                shifted = pbuf[lead + k:lead + k + ts, c0:c0 + cw]
                conv += wk * shifted
                dp += wk * ubuf[taps - 1 - k:taps - 1 - k + ts, c0:c0 + cw]
                sums_ref[k:k + 1, c0:c0 + cw] += _colsum(du * shifted)
            dz_ref[:, c0:c0 + cw] = (dq_ref[:, c0:c0 + cw] * conv).astype(dz_ref.dtype)
            dz_ref[:, d + c0:d + c0 + cw] = (dp * z_ref[:, 2 * d + c0:2 * d + c0 + cw]).astype(dz_ref.dtype)
            dz_ref[:, 2 * d + c0:2 * d + c0 + cw] = (dp * z_ref[:, d + c0:d + c0 + cw]).astype(dz_ref.dtype)

    last = s // halo - 1
    return pl.pallas_call(
        body, name=name, grid=(n_tiles,),
        in_specs=[pl.BlockSpec((ts, d), lambda i: (i, 0)),
                  pl.BlockSpec((halo, d), lambda i: (jnp.minimum((i + 1) * per, last), 0)),
                  pl.BlockSpec((ts, d3), lambda i: (i, 0)),
                  pl.BlockSpec((halo, d3), lambda i: (jnp.maximum(i * per - 1, 0), 0)),
                  pl.BlockSpec((halo, d3), lambda i: (jnp.minimum((i + 1) * per, last), 0)),
                  pl.BlockSpec((taps, d), lambda i: (0, 0))] + [ANY_SPEC] * len(deps),
        out_specs=[pl.BlockSpec((ts, d3), lambda i: (i, 0)), pl.BlockSpec((8, d), lambda i: (0, 0))],
        out_shape=[jax.ShapeDtypeStruct((s, d3), ACT_DTYPE), jax.ShapeDtypeStruct((8, d), F32)],
        scratch_shapes=[pltpu.VMEM((ts + halo, d), F32), pltpu.VMEM((ts + halo, d), F32)],
        compiler_params=_params("arbitrary"),
    )(dq, dq, z, z, z, w, *deps)


def _silu(v):
    return v * jax.nn.sigmoid(v)


def _modulation(name, c_all, mod_w, mod_b_cols):
    nl, d, n = mod_w.shape
    b = c_all.shape[0]
    tn = _pick(n, 512)

    def body(c_ref, w_ref, b_ref, o_ref):
        ca = _silu(c_ref[...]).astype(MXU_DTYPE)
        o_ref[...] = jnp.dot(ca, w_ref[...].astype(MXU_DTYPE), preferred_element_type=F32) + b_ref[...]

    return pl.pallas_call(
        body, name=name, grid=(nl, n // tn),
        in_specs=[pl.BlockSpec((b, d), lambda l, j: (0, 0)),
                  pl.BlockSpec((None, d, tn), lambda l, j: (l, 0, j)),
                  pl.BlockSpec((None, 1, tn), lambda l, j: (l, 0, j))],
        out_specs=pl.BlockSpec((None, b, tn), lambda l, j: (l, 0, j)),
        out_shape=jax.ShapeDtypeStruct((nl, b, n), F32),
        compiler_params=_params("parallel", "parallel"),
    )(c_all, mod_w, mod_b_cols.reshape(nl, 1, n))


def _adamw(g, w, m, v):
    m = ADAM_B1 * m + (1.0 - ADAM_B1) * g
    v = ADAM_B2 * v + (1.0 - ADAM_B2) * (g * g)
    m_hat = m / (1.0 - ADAM_B1 ** ADAM_STEP)
    v_hat = v / (1.0 - ADAM_B2 ** ADAM_STEP)
    delta = -ADAM_LR * (m_hat / (jnp.sqrt(v_hat) + ADAM_EPS) + ADAM_WD * w)
    return delta, m, v


def _write_update(g, w_ref, m_ref, v_ref, outs):
    delta, m, v = _adamw(g, w_ref[...], m_ref[...], v_ref[...])
    outs[0][...] = g
    outs[1][...] = delta
    outs[2][...] = m
    outs[3][...] = v


def _modulation_update(name, c_all_t, dmod, w, m, v, deps=()):
    nl, d, n = w.shape
    b = c_all_t.shape[1]
    tr = _pick(d, 256, 8)

    def body(c_ref, dm_ref, w_ref, m_ref, v_ref, *rest):
        outs = rest[len(deps):]
        ca = _silu(c_ref[...])
        dm = dm_ref[...]
        g = ca[:, 0:1] * dm[0:1, :]
        for i in range(1, b):
            g += ca[:, i:i + 1] * dm[i:i + 1, :]
        _write_update(g, w_ref, m_ref, v_ref, outs)

    blk = pl.BlockSpec((None, tr, n), lambda l, r: (l, r, 0))
    return pl.pallas_call(
        body, name=name, grid=(nl, d // tr),
        in_specs=[pl.BlockSpec((tr, b), lambda l, r: (r, 0)), pl.BlockSpec((None, b, n), lambda l, r: (l, 0, 0)),
                  blk, blk, blk] + [ANY_SPEC] * len(deps),
        out_specs=[blk] * 4, out_shape=[jax.ShapeDtypeStruct(w.shape, F32)] * 4,
        compiler_params=_params("parallel", "parallel"),
    )(c_all_t, dmod, w, m, v, *deps)


def _weight_update(name, parts, w, m, v):
    nl, rows, cols = w.shape
    tr = _pick(rows, max(8, (1 << 18) // cols), 8)

    def body(*refs):
        p_refs = refs[:nl]
        w_ref, m_ref, v_ref = refs[nl:nl + 3]
        outs = refs[nl + 3:]
        for q in range(nl):
            @pl.when(pl.program_id(0) == q)
            def _(q=q):
                g = p_refs[q][0].astype(F32)
                for k in range(1, N_CHIP):
                    g += p_refs[q][k].astype(F32)
                _write_update(g, w_ref, m_ref, v_ref, outs)

    def part_spec(q):
        return pl.BlockSpec((N_CHIP, tr, cols), lambda l, r: (0, jnp.where(l == q, r, 0), 0))

    blk = pl.BlockSpec((None, tr, cols), lambda l, r: (l, r, 0))
    return pl.pallas_call(
        body, name=name, grid=(nl, rows // tr),
        in_specs=[part_spec(q) for q in range(nl)] + [blk, blk, blk],
        out_specs=[blk] * 4, out_shape=[jax.ShapeDtypeStruct(w.shape, F32)] * 4,
        compiler_params=_params("arbitrary", "arbitrary"),
    )(*parts, w, m, v)


def _vector_update(name, gathered, w, m, v):
    _, rows, cols = gathered.shape
    rw = w.shape[0]

    def body(g_ref, w_ref, m_ref, v_ref, tot_ref, d_ref, m2_ref, v2_ref):
        tot = g_ref[0]
        for k in range(1, N_DEV):
            tot += g_ref[k]
        tot_ref[...] = tot
        delta, m2, v2 = _adamw(tot[0:rw], w_ref[...], m_ref[...], v_ref[...])
        d_ref[...] = delta
        m2_ref[...] = m2
        v2_ref[...] = v2

    vm = pl.BlockSpec(memory_space=pltpu.VMEM)
    return pl.pallas_call(
        body, name=name, in_specs=[vm] * 4, out_specs=[vm] * 4,
        out_shape=[jax.ShapeDtypeStruct((rows, cols), F32)] + [jax.ShapeDtypeStruct((rw, cols), F32)] * 3,
        compiler_params=pltpu.CompilerParams(vmem_limit_bytes=V7X_VMEM_LIMIT_BYTES),
    )(gathered, w, m, v)


def _plain_update(name, g, w, m, v):
    def body(g_ref, w_ref, m_ref, v_ref, d_ref, m2_ref, v2_ref):
        delta, m2, v2 = _adamw(g_ref[...], w_ref[...], m_ref[...], v_ref[...])
        d_ref[...] = delta
        m2_ref[...] = m2
        v2_ref[...] = v2

    vm = pl.BlockSpec(memory_space=pltpu.VMEM)
    return pl.pallas_call(
        body, name=name, in_specs=[vm] * 4, out_specs=[vm] * 3,
        out_shape=[jax.ShapeDtypeStruct(w.shape, F32)] * 3,
    )(g, w, m, v)


def _rows(a, width, mult=8):
    r = a.reshape(-1, width)
    pad = -r.shape[0] % mult
    return jnp.pad(r, ((0, pad), (0, 0))) if pad else r


def _pack(blocks, width):
    parts, spans, at = [], [], 0
    for a in blocks:
        n = a.size // width
        p = _rows(a, width)
        parts.append(p)
        spans.append((at, n))
        at += p.shape[0]
    return jnp.concatenate(parts, axis=0), spans


def kernel(x, c, mod_w, mod_b, pre_mix_g, post_mix_g, pre_ffn_g, post_ffn_g, a_w1, a_b1, a_dw, a_dwb, a_ln_g, a_ln_b, a_w2, a_b2, b_w_in, b_conv, b_w_out, f_w1, f_w2, loss_target, m_mod_w, m_mod_b, m_pre_mix_g, m_post_mix_g, m_pre_ffn_g, m_post_ffn_g, m_a_w1, m_a_b1, m_a_dw, m_a_dwb, m_a_ln_g, m_a_ln_b, m_a_w2, m_a_b2, m_b_w_in, m_b_conv, m_b_w_out, m_f_w1, m_f_w2, v_mod_w, v_mod_b, v_pre_mix_g, v_post_mix_g, v_pre_ffn_g, v_post_ffn_g, v_a_w1, v_a_b1, v_a_dw, v_a_dwb, v_a_ln_g, v_a_ln_b, v_a_w2, v_a_b2, v_b_w_in, v_b_conv, v_b_w_out, v_f_w1, v_f_w2):
    depth, d = pre_mix_g.shape
    n_a, n_b = a_w1.shape[0], b_w_in.shape[0]
    s = x.shape[1]
    dsh = d // N_DEV
    taps_a, taps_b = a_dw.shape[1], b_conv.shape[1]
    px, py, pc = _pos()
    me = 4 * px + 2 * py + pc
    x0 = x.reshape(s, d)
    target = loss_target.reshape(s, d)

    me1 = jnp.reshape(me, (1,)).astype(jnp.int32)

    def members(i):
        j = i // 2
        mix = [(a_w1, j), (a_w2, j)] if i % 2 == 0 else [(b_w_in, j), (b_w_out, j)]
        return mix, [(f_w1, i), (f_w2, i)]

    def start_group(tag, group, after=()):
        return _gather_start(tag, [_place("place_%s_%d" % (tag, q), w, layer, me1)
                                   for q, (w, layer) in enumerate(group)], after)

    packed0, spans0 = _pack([c, a_dw, b_conv], dsh)
    got0 = _small_gather("gather_cond", packed0)

    def full_width(span):
        at, n = span
        return jnp.transpose(got0[:, at:at + n, :], (1, 0, 2)).reshape(n, d)

    c_all = got0[:, spans0[0][0]:spans0[0][0] + spans0[0][1], :].reshape(N_DEV, d)
    a_dw_full = full_width(spans0[1]).reshape(n_a, taps_a, d)
    b_conv_full = full_width(spans0[2]).reshape(n_b, taps_b, d)

    n_mod = mod_w.shape[2]
    mod_b_cols = lax.dynamic_slice_in_dim(mod_b, me * n_mod, n_mod, axis=1)
    mod_local = _modulation("modulation", c_all, mod_w, mod_b_cols)
    got1 = _small_gather("gather_mod", mod_local.reshape(depth * N_DEV, n_mod))
    mod_me = lax.dynamic_index_in_dim(got1.reshape(N_DEV, depth, N_DEV, n_mod), me, axis=2, keepdims=False)
    mod_me = jnp.transpose(mod_me, (1, 0, 2)).reshape(depth, 6, 1, d)

    def vec(a, i):
        return a[i].reshape(1, -1)

    def pass_groups(tag, handles, after):
        for q in range(len(handles)):
            handles[q] = _gather_pass("%s%d" % (tag, q), handles[q], after)
            after = handles[q]["token"]
        return after

    def wait_groups(tag, handles, after):
        return sum([_gather_wait("%s%d" % (tag, q), hq, after) for q, hq in enumerate(handles)], [])

    mix_copy = [start_group("l0m", members(0)[0], (got1,))]
    ffn_copy = [start_group("l0u", members(0)[1][:1], (mix_copy[0]["token"],))]
    ffn_copy.append(start_group("l0d", members(0)[1][1:], (ffn_copy[0]["token"],)))
    saved = []
    xs = x0
    h = _residual_fwd("fwd_in", xs, pre=(vec(pre_mix_g, 0), mod_me[0, 1], mod_me[0, 0]),
                      deps=[hq["token"] for hq in ffn_copy])[0]
    w_mix = wait_groups("l0m", mix_copy, pass_groups("l0m", mix_copy, h))
    for i in range(depth):
        j = i // 2
        sh_m, sc_m, gt_m, sh_f, sc_f, gt_f = [mod_me[i, q] for q in range(6)]
        keep = {"x_mix": xs, "h_mix": h}
        w1g, w2g = w_mix
        early = i > 0
        if i % 2 == 0:
            u = _mm_cols("a_w1", h, w1g, F32, _store_bias, bias=vec(a_b1, j))[0]
            deps = [pass_groups("l%df" % i, ffn_copy, u)] if early else []
            glu = _glu("a_glu", u)
            cv, v = _conv_ln_swish("a_conv", glu, a_dw_full[j], vec(a_dwb, j), vec(a_ln_g, j), vec(a_ln_b, j))
            y = _mm_rows("a_w2", v, w2g, F32, bias=vec(a_b2, j), deps=deps)
            keep.update(u=u, glu=glu, cv=cv, v=v)
        else:
            z = _mm_cols("b_w_in", h, w1g, F32)[0]
            deps = [pass_groups("l%df" % i, ffn_copy, z)] if early else []
            q = _short_conv_fwd("b_conv", z, b_conv_full[j])
            y = _mm_rows("b_w_out", q, w2g, F32, deps=deps)
            keep.update(z=z, q=q)
        after = y if early else pass_groups("l%df" % i, ffn_copy, y)
        deps = []
        if i + 1 < depth:
            mix_copy = [start_group("l%dm" % (i + 1), members(i + 1)[0])]
            ffn_next = [start_group("l%df" % (i + 1), members(i + 1)[1])]
            deps = [mix_copy[0]["token"], ffn_next[0]["token"]]
        fw1g, fw2g = wait_groups("l%df" % i, ffn_copy, after)
        keep.update(w1g=w1g, w2g=w2g, fw1g=fw1g, fw2g=fw2g, y_mix=y)
        xs, h = _residual_fwd("fwd_mid", xs, post=(y, vec(post_mix_g, i), gt_m),
                              pre=(vec(pre_ffn_g, i), sc_f, sh_f))
        keep.update(x_ffn=xs, h_ffn=h)
        act, hid = _mm_cols("f_w1", h, fw1g, ACT_DTYPE, _store_relu2, n_out=2, deps=deps)
        deps = [pass_groups("l%dm" % (i + 1), mix_copy, act)] if i + 1 < depth else []
        y = _mm_rows("f_w2", act, fw2g, F32, deps=deps)
        keep.update(act=act, hid=hid, y_ffn=y)
        saved.append(keep)
        if i + 1 < depth:
            w_mix = wait_groups("l%dm" % (i + 1), mix_copy, y)
            ffn_copy = ffn_next
            nxt = mod_me[i + 1]
            xs, h = _residual_fwd("fwd_next", xs, post=(y, vec(post_ffn_g, i), gt_f),
                                  pre=(vec(pre_mix_g, i + 1), nxt[1], nxt[0]))
        else:
            _, g, loss_blk = _residual_fwd("fwd_loss", xs, post=(y, vec(post_ffn_g, i), gt_f), target=target)

    where = jnp.stack([pc, 2 * px + py]).astype(jnp.int32)
    land = {"a_w1": [None] * n_a, "a_w2": [None] * n_a, "b_w_in": [None] * n_b, "b_w_out": [None] * n_b,
            "f_w1": [None] * depth, "f_w2": [None] * depth}
    reductions = []

    def reduce_group(tag, slots, swap, after):
        grads, theirs = _sibling_wait(tag, swap, after)
        pairs = [_chip_partial("chip_partial_%s_%d" % (tag, q), gq, tq, where)
                 for q, (gq, tq) in enumerate(zip(grads, theirs))]
        handle = _reduce_start(tag, [p for p, _ in pairs], [l for _, l in pairs])
        reductions.append((tag, slots, handle))
        return handle["token"]

    zero_vec = jnp.zeros((d,), F32)
    dmod = [[zero_vec] * 6 for _ in range(depth)]
    small = {name: [None] * depth for name in ("pre_mix_g", "post_mix_g", "pre_ffn_g", "post_ffn_g")}
    small_a = {name: [None] * n_a for name in ("a_b1", "a_dwb", "a_ln_g", "a_ln_b", "a_b2", "a_dw")}
    small_b = {"b_conv": [None] * n_b}

    dy, sums = _residual_bwd("bwd_top", g, post=(saved[-1]["y_ffn"], vec(post_ffn_g, depth - 1), mod_me[depth - 1, 5]))
    deps = []
    for i in reversed(range(depth)):
        j = i // 2
        kp = saved[i]
        sh_m, sc_m, gt_m, sh_f, sc_f, gt_f = [mod_me[i, q] for q in range(6)]
        dmod[i][5] = sums[3]
        small["post_ffn_g"][i] = sums[4]
        dhid = _mm_rows_t("f_w2_t", dy, kp["fw2g"], ACT_DTYPE, _store_relu2_grad, extra=kp["hid"], deps=deps)
        g_fw2 = _grad_rows("f_w2_grad", kp["act"], dy, f_w2.shape[1])
        dh = _mm_cols_t("f_w1_t", dhid, kp["fw1g"], F32)
        g_fw1 = _grad_cols("f_w1_grad", kp["h_ffn"], dhid, f_w1.shape[2])
        ffn_swap = _sibling_start("r%df" % i, [g_fw1, g_fw2])
        deps = [ffn_swap["token"]]
        g, dy, sums = _residual_bwd("bwd_mid", g, pre=(dh, kp["x_ffn"], vec(pre_ffn_g, i), sc_f),
                                    post=(kp["y_mix"], vec(post_mix_g, i), gt_m))
        dmod[i][3], dmod[i][4], dmod[i][2] = sums[0], sums[1], sums[3]
        small["pre_ffn_g"][i], small["post_mix_g"][i] = sums[2], sums[4]
        if i % 2 == 0:
            small_a["a_b2"][j] = sums[5]
            dv = _mm_rows_t("a_w2_t", dy, kp["w2g"], F32, deps=deps)
            g_w2 = _grad_rows("a_w2_grad", kp["v"], dy, a_w2.shape[1])
            deps = [reduce_group("r%df" % i, (("f_w1", i), ("f_w2", i)), ffn_swap, g_w2)]
            dcv, lsum = _ln_swish_bwd("a_ln_bwd", dv, kp["cv"], vec(a_ln_g, j), vec(a_ln_b, j), deps=deps)
            small_a["a_ln_g"][j], small_a["a_ln_b"][j], small_a["a_dwb"][j] = lsum[0], lsum[1], lsum[2]
            du, ddw, usum = _conv_glu_bwd("a_conv_bwd", dcv, kp["glu"], kp["u"], a_dw_full[j])
            small_a["a_dw"][j], small_a["a_b1"][j] = ddw[:taps_a], usum[0]
            dh = _mm_cols_t("a_w1_t", du, kp["w1g"], F32)
            g_w1 = _grad_cols("a_w1_grad", kp["h_mix"], du, a_w1.shape[2])
            names = ("a_w1", "a_w2")
        else:
            dq = _mm_rows_t("b_w_out_t", dy, kp["w2g"], F32, deps=deps)
            g_w2 = _grad_rows("b_w_out_grad", kp["q"], dy, b_w_out.shape[1])
            deps = [reduce_group("r%df" % i, (("f_w1", i), ("f_w2", i)), ffn_swap, g_w2)]
            dz, wsum = _short_conv_bwd("b_conv_bwd", dq, kp["z"], b_conv_full[j], deps=deps)
            small_b["b_conv"][j] = wsum[:taps_b]
            dh = _mm_cols_t("b_w_in_t", dz, kp["w1g"], F32)
            g_w1 = _grad_cols("b_w_in_grad", kp["h_mix"], dz, b_w_in.shape[2])
            names = ("b_w_in", "b_w_out")
        mix_swap = _sibling_start("r%dm" % i, [g_w1, g_w2])
        deps = [mix_swap["token"]]
        if i > 0:
            prev = saved[i - 1]
            g, dy, sums = _residual_bwd("bwd_next", g, pre=(dh, kp["x_mix"], vec(pre_mix_g, i), sc_m),
                                        post=(prev["y_ffn"], vec(post_ffn_g, i - 1), mod_me[i - 1, 5]))
        else:
            g, sums = _residual_bwd("bwd_in", g, pre=(dh, kp["x_mix"], vec(pre_mix_g, i), sc_m))
        dmod[i][0], dmod[i][1] = sums[0], sums[1]
        small["pre_mix_g"][i] = sums[2]
        if i > 0:
            deps.append(reduce_group("r%dm" % i, ((names[0], j), (names[1], j)), mix_swap, g))
        else:
            last_swap = (((names[0], j), (names[1], j)), mix_swap)
    grad_x = g.reshape(x.shape)

    reps = [jnp.stack([jnp.stack(r) for r in dmod]),
            jnp.stack(small["pre_mix_g"]), jnp.stack(small["post_mix_g"]),
            jnp.stack(small["pre_ffn_g"]), jnp.stack(small["post_ffn_g"]),
            jnp.stack(small_a["a_b1"]), jnp.stack(small_a["a_dwb"]), jnp.stack(small_a["a_ln_g"]),
            jnp.stack(small_a["a_ln_b"]), jnp.stack(small_a["a_b2"])]
    rep_w = [mod_b, pre_mix_g, post_mix_g, pre_ffn_g, post_ffn_g, a_b1, a_dwb, a_ln_g, a_ln_b, a_b2]
    rep_m = [m_mod_b, m_pre_mix_g, m_post_mix_g, m_pre_ffn_g, m_post_ffn_g, m_a_b1, m_a_dwb, m_a_ln_g, m_a_ln_b, m_a_b2]
    rep_v = [v_mod_b, v_pre_mix_g, v_post_mix_g, v_pre_ffn_g, v_post_ffn_g, v_a_b1, v_a_dwb, v_a_ln_g, v_a_ln_b, v_a_b2]
    w_small, spans = _pack(rep_w, d)
    m_small, _ = _pack(rep_m, d)
    v_small, _ = _pack(rep_v, d)
    loss_row = jnp.pad(loss_blk[0:1, 0:1], ((0, 0), (0, d - 1)))
    part_small, spans_g = _pack(reps + [jnp.stack(small_a["a_dw"]), jnp.stack(small_b["b_conv"]), loss_row], d)
    got2 = _small_gather("gather_small", part_small, deps=[last_swap[1]["token"]])
    last_token = reduce_group("r0m", last_swap[0], last_swap[1], got2)
    total, d_small, m2_small, v2_small = _vector_update("vector_update", got2, w_small, m_small, v_small)

    def unpack(buf, span, like):
        return buf[span[0]:span[0] + span[1]].reshape(like.shape)

    rep_out = [[unpack(buf, sp, w) for sp, w in zip(spans, rep_w)] for buf in (total, d_small, m2_small, v2_small)]
    loss = total[spans_g[-1][0], 0]

    conv_g = [lax.dynamic_slice_in_dim(total[sp[0]:sp[0] + sp[1]], me * dsh, dsh, axis=1)
              for sp in spans_g[len(reps):len(reps) + 2]]
    gc, _ = _pack(conv_g, dsh)
    wc, spans_c = _pack([a_dw, b_conv], dsh)
    mc, _ = _pack([m_a_dw, m_b_conv], dsh)
    vc, _ = _pack([v_a_dw, v_b_conv], dsh)
    conv_out = [gc] + list(_plain_update("conv_update", gc, wc, mc, vc))
    conv_out = [[unpack(buf, sp, w) for sp, w in zip(spans_c, (a_dw, b_conv))] for buf in conv_out]

    at, n = spans_g[0]
    dmod_all = got2[:, at:at + n, :].reshape(N_DEV, depth, 6 * d)
    dmod_cols = jnp.transpose(lax.dynamic_slice_in_dim(dmod_all, me * n_mod, n_mod, axis=2), (1, 0, 2))
    mod_out = _modulation_update("modulation_update", jnp.transpose(c_all), dmod_cols, mod_w, m_mod_w, v_mod_w,
                                 deps=[last_token])

    big = {"f_w1": (f_w1, m_f_w1, v_f_w1), "f_w2": (f_w2, m_f_w2, v_f_w2),
           "b_w_in": (b_w_in, m_b_w_in, v_b_w_in), "b_w_out": (b_w_out, m_b_w_out, v_b_w_out),
           "a_w1": (a_w1, m_a_w1, v_a_w1), "a_w2": (a_w2, m_a_w2, v_a_w2)}
    big_out = {}
    after = mod_out[3]

    def update_complete():
        nonlocal after
        for k, wmv in big.items():
            if k not in big_out and all(b is not None for b in land[k]):
                big_out[k] = _weight_update("update_" + k, land[k], *wmv)
                after = big_out[k][3]

    for tag, slots, handle in reductions:
        if tag == reductions[-1][0]:
            update_complete()
        landed = _reduce_wait(tag, handle, after)
        after = landed[0]
        for (key, idx), buf in zip(slots, landed):
            land[key][idx] = buf
    update_complete()

    def family(q):
        rep = dict(zip(("mod_b", "pre_mix_g", "post_mix_g", "pre_ffn_g", "post_ffn_g",
                        "a_b1", "a_dwb", "a_ln_g", "a_ln_b", "a_b2"), rep_out[q]))
        return (mod_out[q], rep["mod_b"], rep["pre_mix_g"], rep["post_mix_g"], rep["pre_ffn_g"], rep["post_ffn_g"],
                big_out["a_w1"][q], rep["a_b1"], conv_out[q][0], rep["a_dwb"], rep["a_ln_g"], rep["a_ln_b"],
                big_out["a_w2"][q], rep["a_b2"], big_out["b_w_in"][q], conv_out[q][1], big_out["b_w_out"][q],
                big_out["f_w1"][q], big_out["f_w2"][q])

    return (loss, grad_x, *family(0), *family(1), *family(2), *family(3))
```

```python
import jax
import jax.numpy as jnp
from jax import lax
from jax.experimental import pallas as pl
from jax.experimental.pallas import tpu as pltpu

MXU_DTYPE = jnp.bfloat16
ACT_DTYPE = jnp.bfloat16
COMM_DTYPE = jnp.bfloat16

N_DEV = 8
N_CHIP = 4
RMS_EPS = 1e-6
LN_EPS = 1e-5
ADAM_LR = 0.001
ADAM_B1 = 0.9
ADAM_B2 = 0.999
ADAM_EPS = 1e-08
ADAM_WD = 0.01
ADAM_STEP = 10

V7X_VMEM_LIMIT_BYTES = 56 * 1024 * 1024
LANE = 128
ROW_TILE = 256
CONV_TILE = 128
CONV_HALO = 32
SHORT_HALO = 8
CONV_CHUNK = 256
MM_TM, MM_TN, MM_TK = 1024, 1024, 2048
MM_RESIDENT = 4 * 1024 * 1024

F32 = jnp.float32
MESH = pl.DeviceIdType.MESH


def _pick(dim, pref, mult=LANE):
    if dim <= pref:
        return dim
    t = pref - pref % mult
    while dim % t:
        t -= mult
    return t


def _params(*sem):
    return pltpu.CompilerParams(dimension_semantics=sem, vmem_limit_bytes=V7X_VMEM_LIMIT_BYTES)


def _pos():
    return lax.axis_index("x"), lax.axis_index("y"), lax.axis_index("c")


def _flip(v, bit):
    return 1 - v if bit else v


def _small_gather(name, v, deps=()):
    rows, cols = v.shape
    n_deps = len(deps)

    def body(v_ref, *rest):
        out_ref, send_sems, recv_sems, local_sem = rest[n_deps:]
        x, y, c = _pos()
        me, sibling = (x, y, c), (x, y, 1 - c)
        chips = [(1 - x, y), (x, 1 - y), (1 - x, 1 - y)]

        def block(px, py, pc):
            return out_ref.at[4 * px + 2 * py + pc]

        def copy(k, owner, to, src=None):
            return pltpu.make_async_remote_copy(
                src_ref=block(*owner) if src is None else src, dst_ref=block(*owner), send_sem=send_sems.at[k],
                recv_sem=recv_sems.at[k], device_id=to, device_id_type=MESH)

        mine = pltpu.make_async_copy(v_ref, block(*me), local_sem)
        mine.start()
        first = [copy(0, me, sibling, src=v_ref)]
        first += [copy(1 + j, me, (*chip, c), src=v_ref) for j, chip in enumerate(chips)]
        for cp in first:
            cp.start()
        passed = [copy(4 + j, (*chip, c), sibling) for j, chip in enumerate(chips)]
        for j, chip in enumerate(chips):
            copy(1 + j, (*chip, c), me).wait_recv()
            passed[j].start()
        copy(0, sibling, me).wait_recv()
        for j, chip in enumerate(chips):
            copy(4 + j, (*chip, 1 - c), me).wait_recv()
        for cp in first + passed:
            cp.wait_send()
        mine.wait()

    return pl.pallas_call(
        body, name=name,
        out_shape=jax.ShapeDtypeStruct((N_DEV, rows, cols), v.dtype),
        in_specs=[pl.BlockSpec(memory_space=pltpu.VMEM)] + [pl.BlockSpec(memory_space=pl.ANY)] * n_deps,
        out_specs=pl.BlockSpec(memory_space=pltpu.VMEM),
        scratch_shapes=[pltpu.SemaphoreType.DMA((N_DEV - 1,)), pltpu.SemaphoreType.DMA((N_DEV - 1,)),
                        pltpu.SemaphoreType.DMA],
        compiler_params=pltpu.CompilerParams(vmem_limit_bytes=V7X_VMEM_LIMIT_BYTES),
    )(v, *deps)


HBM_SPEC = pl.BlockSpec(memory_space=pltpu.HBM)
SEM_SPEC = pl.BlockSpec(memory_space=pltpu.SEMAPHORE)
ANY_SPEC = pl.BlockSpec(memory_space=pl.ANY)
TOKEN = jax.ShapeDtypeStruct((8, LANE), F32)
TOKEN_SPEC = pl.BlockSpec(memory_space=pltpu.VMEM)
SPLIT_COPY = pltpu.CompilerParams(has_side_effects=pltpu.SideEffectType.DATAFLOW_SIDE_EFFECTING)


def _hbm(a):
    return pltpu.with_memory_space_constraint(a, pltpu.HBM)


def _dma_sems(n):
    return pltpu.SemaphoreType.DMA((n,))


def _place(name, w, layer, me):
    _, rows, cols = w.shape
    tr = _pick(rows, max(8, (1 << 20) // cols), 8)

    def body(me_ref, w_ref, o_ref):
        o_ref[...] = w_ref[...].astype(o_ref.dtype)

    return pl.pallas_call(
        body, name=name,
        grid_spec=pltpu.PrefetchScalarGridSpec(
            num_scalar_prefetch=1, grid=(rows // tr,),
            in_specs=[pl.BlockSpec((None, tr, cols), lambda r, me_ref: (layer, r, 0))],
            out_specs=pl.BlockSpec((None, tr, cols), lambda r, me_ref: (me_ref[0], r, 0))),
        out_shape=jax.ShapeDtypeStruct((N_DEV, rows, cols), COMM_DTYPE),
        compiler_params=_params("parallel"),
    )(me, w)


def _block_copy(buf, owner, to, send_sem, recv_sem):
    blk = buf.at[4 * owner[0] + 2 * owner[1] + owner[2]]
    return pltpu.make_async_remote_copy(src_ref=blk, dst_ref=blk, send_sem=send_sem, recv_sem=recv_sem,
                                        device_id=to, device_id_type=MESH)


def _other_chips(x, y):
    return [(1 - x, y), (x, 1 - y), (1 - x, 1 - y)]


def _gather_start(tag, bufs, after=()):
    n = len(bufs)
    n_in = n + len(after)

    def body(*refs):
        b = refs[:n]
        send, recv_ici, recv_sib = refs[n_in:n_in + 3]
        token = refs[-1]
        x, y, c = _pos()
        me = (x, y, c)
        for i in range(n):
            _block_copy(b[i], me, (x, y, 1 - c), send.at[4 * i], recv_sib.at[i]).start()
            for j, chip in enumerate(_other_chips(x, y)):
                _block_copy(b[i], me, (*chip, c), send.at[4 * i + 1 + j], recv_ici.at[3 * i + j]).start()
        token[...] = jnp.zeros_like(token)

    outs = pl.pallas_call(
        body, name="gather_start_" + tag,
        out_shape=(_dma_sems(4 * n), _dma_sems(3 * n), _dma_sems(n),
                   *[pltpu.HBM(b.shape, b.dtype) for b in bufs], TOKEN),
        in_specs=[HBM_SPEC] * n + [ANY_SPEC] * len(after),
        out_specs=(SEM_SPEC, SEM_SPEC, SEM_SPEC, *[HBM_SPEC] * n, TOKEN_SPEC),
        input_output_aliases={i: 3 + i for i in range(n)}, compiler_params=SPLIT_COPY,
    )(*[_hbm(b) for b in bufs], *after)
    return dict(send=outs[0], recv_ici=outs[1], recv_sib=outs[2], bufs=list(outs[3:3 + n]), token=outs[-1])


def _gather_pass(tag, h, after):
    bufs = h["bufs"]
    n = len(bufs)

    def body(*refs):
        b = refs[:n]
        recv_ici = refs[n]
        fsend, frecv = refs[n + 2:n + 4]
        token = refs[-1]
        x, y, c = _pos()
        for i in range(n):
            for j, chip in enumerate(_other_chips(x, y)):
                _block_copy(b[i], (*chip, c), (x, y, c), fsend.at[3 * i + j], recv_ici.at[3 * i + j]).wait_recv()
                _block_copy(b[i], (*chip, c), (x, y, 1 - c), fsend.at[3 * i + j], frecv.at[3 * i + j]).start()
        token[...] = jnp.zeros_like(token)

    outs = pl.pallas_call(
        body, name="gather_pass_" + tag,
        out_shape=(_dma_sems(3 * n), _dma_sems(3 * n), *[pltpu.HBM(b.shape, b.dtype) for b in bufs], TOKEN),
        in_specs=[HBM_SPEC] * n + [SEM_SPEC, ANY_SPEC],
        out_specs=(SEM_SPEC, SEM_SPEC, *[HBM_SPEC] * n, TOKEN_SPEC),
        input_output_aliases={i: 2 + i for i in range(n)}, compiler_params=SPLIT_COPY,
    )(*bufs, h["recv_ici"], after)
    return dict(h, fsend=outs[0], frecv=outs[1], bufs=list(outs[2:2 + n]), token=outs[-1])


def _gather_wait(tag, h, after):
    bufs = h["bufs"]
    n = len(bufs)

    def body(*refs):
        b = refs[:n]
        send, recv_sib, fsend, frecv = refs[n:n + 4]
        x, y, c = _pos()
        me, sibling = (x, y, c), (x, y, 1 - c)
        for i in range(n):
            _block_copy(b[i], sibling, me, send.at[4 * i], recv_sib.at[i]).wait_recv()
            for j, chip in enumerate(_other_chips(x, y)):
                _block_copy(b[i], (*chip, 1 - c), me, fsend.at[3 * i + j], frecv.at[3 * i + j]).wait_recv()
            for k in range(4):
                _block_copy(b[i], me, sibling, send.at[4 * i + k], recv_sib.at[i]).wait_send()
            for j, chip in enumerate(_other_chips(x, y)):
                _block_copy(b[i], (*chip, c), sibling, fsend.at[3 * i + j], frecv.at[3 * i + j]).wait_send()

    outs = pl.pallas_call(
        body, name="gather_wait_" + tag,
        out_shape=tuple(pltpu.HBM(b.shape, b.dtype) for b in bufs),
        in_specs=[HBM_SPEC] * n + [SEM_SPEC] * 4 + [ANY_SPEC], out_specs=tuple([HBM_SPEC] * n),
        input_output_aliases={i: i for i in range(n)}, compiler_params=SPLIT_COPY,
    )(*bufs, h["send"], h["recv_sib"], h["fsend"], h["frecv"], after)
    return list(outs)


def _sibling_copy(grad, land, send_sem, recv_sem):
    x, y, c = _pos()
    return pltpu.make_async_remote_copy(
        src_ref=grad.at[:, 1 - c], dst_ref=land, send_sem=send_sem, recv_sem=recv_sem,
        device_id=(x, y, 1 - c), device_id_type=MESH)


def _sibling_start(tag, grads):
    n = len(grads)
    landings = [lax.empty((N_CHIP,) + g.shape[2:], g.dtype) for g in grads]

    def body(*refs):
        g, land = refs[:n], refs[n:2 * n]
        send, recv = refs[2 * n:2 * n + 2]
        token = refs[-1]
        for i in range(n):
            _sibling_copy(g[i], land[i], send.at[i], recv.at[i]).start()
        token[...] = jnp.zeros_like(token)

    both = list(grads) + landings
    outs = pl.pallas_call(
        body, name="sibling_start_" + tag,
        out_shape=(_dma_sems(n), _dma_sems(n), *[pltpu.HBM(b.shape, b.dtype) for b in both], TOKEN),
        in_specs=[HBM_SPEC] * (2 * n), out_specs=(SEM_SPEC, SEM_SPEC, *[HBM_SPEC] * (2 * n), TOKEN_SPEC),
        input_output_aliases={i: 2 + i for i in range(2 * n)}, compiler_params=SPLIT_COPY,
    )(*[_hbm(b) for b in both])
    return dict(send=outs[0], recv=outs[1], grads=list(outs[2:2 + n]), landings=list(outs[2 + n:2 + 2 * n]),
                token=outs[-1])


def _sibling_wait(tag, h, after):
    n = len(h["grads"])

    def body(*refs):
        g, land = refs[:n], refs[n:2 * n]
        send, recv = refs[2 * n:2 * n + 2]
        for i in range(n):
            _sibling_copy(g[i], land[i], send.at[i], recv.at[i]).wait()

    both = h["grads"] + h["landings"]
    outs = pl.pallas_call(
        body, name="sibling_wait_" + tag,
        out_shape=tuple(pltpu.HBM(b.shape, b.dtype) for b in both),
        in_specs=[HBM_SPEC] * (2 * n) + [SEM_SPEC, SEM_SPEC, ANY_SPEC], out_specs=tuple([HBM_SPEC] * (2 * n)),
        input_output_aliases={i: i for i in range(2 * n)}, compiler_params=SPLIT_COPY,
    )(*both, h["send"], h["recv"], after)
    return list(outs[:n]), list(outs[n:])


def _reduce_copy(part, land, x, y, c, k, src_chip, send_sem, recv_sem):
    px, py = _flip(x, k >> 1 & 1), _flip(y, k & 1)
    return pltpu.make_async_remote_copy(src_ref=part.at[2 * px + py], dst_ref=land.at[src_chip], send_sem=send_sem,
                                        recv_sem=recv_sem, device_id=(px, py, c), device_id_type=MESH)


def _reduce_start(tag, partials, landings):
    n = len(partials)

    def body(*refs):
        p, land = refs[:n], refs[n:2 * n]
        send, recv = refs[2 * n:2 * n + 2]
        token = refs[-1]
        x, y, c = _pos()
        for i in range(n):
            for k in range(1, N_CHIP):
                _reduce_copy(p[i], land[i], x, y, c, k, 2 * x + y, send.at[3 * i + k - 1], recv.at[3 * i + k - 1]).start()
        token[...] = jnp.zeros_like(token)

    both = list(partials) + list(landings)
    outs = pl.pallas_call(
        body, name="reduce_start_" + tag,
        out_shape=(_dma_sems(3 * n), _dma_sems(3 * n), *[pltpu.HBM(b.shape, b.dtype) for b in both], TOKEN),
        in_specs=[HBM_SPEC] * (2 * n), out_specs=(SEM_SPEC, SEM_SPEC, *[HBM_SPEC] * (2 * n), TOKEN_SPEC),
        input_output_aliases={i: 2 + i for i in range(2 * n)}, compiler_params=SPLIT_COPY,
    )(*[_hbm(b) for b in both])
    return dict(send=outs[0], recv=outs[1], partials=list(outs[2:2 + n]), landings=list(outs[2 + n:2 + 2 * n]),
                token=outs[-1])


def _reduce_wait(tag, h, after):
    n = len(h["partials"])

    def body(*refs):
        p, land = refs[:n], refs[n:2 * n]
        send, recv = refs[2 * n:2 * n + 2]
        x, y, c = _pos()
        for i in range(n):
            for k in range(1, N_CHIP):
                src_chip = 2 * _flip(x, k >> 1 & 1) + _flip(y, k & 1)
                cp = _reduce_copy(p[i], land[i], x, y, c, k, src_chip, send.at[3 * i + k - 1], recv.at[3 * i + k - 1])
                cp.wait_recv()
                cp.wait_send()

    both = h["partials"] + h["landings"]
    outs = pl.pallas_call(
        body, name="reduce_wait_" + tag,
        out_shape=tuple(pltpu.HBM(b.shape, b.dtype) for b in both),
        in_specs=[HBM_SPEC] * (2 * n) + [SEM_SPEC, SEM_SPEC, ANY_SPEC], out_specs=tuple([HBM_SPEC] * (2 * n)),
        input_output_aliases={i: i for i in range(2 * n)}, compiler_params=SPLIT_COPY,
    )(*both, h["send"], h["recv"], after)
    return list(outs[n:])


def _chip_partial(name, grad, theirs, where):
    _, _, rows, cols = grad.shape
    tr = _pick(rows, max(8, (1 << 20) // cols), 8)
    landing = lax.empty(theirs.shape, theirs.dtype)

    def body(where_ref, g_ref, t_ref, land_in, o_ref, land_ref):
        total = (g_ref[...].astype(F32) + t_ref[...].astype(F32)).astype(o_ref.dtype)
        o_ref[...] = total

        @pl.when(pl.program_id(1) == where_ref[1])
        def _():
            land_ref[...] = total

    return pl.pallas_call(
        body, name=name,
        grid_spec=pltpu.PrefetchScalarGridSpec(
            num_scalar_prefetch=1, grid=(rows // tr, N_CHIP),
            in_specs=[pl.BlockSpec((None, None, tr, cols), lambda r, k, w: (k, w[0], r, 0)),
                      pl.BlockSpec((None, tr, cols), lambda r, k, w: (k, r, 0)), ANY_SPEC],
            out_specs=[pl.BlockSpec((None, tr, cols), lambda r, k, w: (k, r, 0)),
                       pl.BlockSpec((None, tr, cols), lambda r, k, w: (w[1], r, 0))]),
        out_shape=[jax.ShapeDtypeStruct(theirs.shape, theirs.dtype)] * 2,
        input_output_aliases={3: 1},
        compiler_params=_params("parallel", "arbitrary"),
    )(where, grad, theirs, landing)


def _matmul(name, a, b, a_spec, b_spec, grid, acc_shape, out_shape, out_specs, epilogue,
            ta=False, tb=False, extras=(), extra_specs=(), deps=()):
    nk = grid[2]
    n_extra = len(extras)
    n_in = 2 + n_extra + len(deps)
    dims = (((0,) if ta else (1,), (1,) if tb else (0,)), ((), ()))

    def body(*refs):
        a_ref, b_ref = refs[:2]
        extra_refs = refs[2:2 + n_extra]
        out_refs = refs[n_in:n_in + len(out_shape)]

        def product():
            return lax.dot_general(a_ref[...].astype(MXU_DTYPE), b_ref[...].astype(MXU_DTYPE), dims,
                                   preferred_element_type=F32)

        if nk == 1:
            epilogue(product(), extra_refs, out_refs)
            return
        acc_ref = refs[-1]
        k = pl.program_id(2)

        @pl.when(k == 0)
        def _():
            acc_ref[...] = product()

        if nk > 2:
            @pl.when((k > 0) & (k < nk - 1))
            def _():
                acc_ref[...] += product()

        @pl.when(k == nk - 1)
        def _():
            epilogue(acc_ref[...] + product(), extra_refs, out_refs)

    return pl.pallas_call(
        body, name=name, grid=grid,
        in_specs=[a_spec, b_spec, *extra_specs, *[ANY_SPEC] * len(deps)], out_specs=out_specs, out_shape=out_shape,
        scratch_shapes=[pltpu.VMEM(acc_shape, F32)] if nk > 1 else [],
        compiler_params=_params("parallel", "parallel", "arbitrary"),
    )(a, b, *extras, *deps)


def _store(acc, extra_refs, out_refs):
    out_refs[0][...] = acc.astype(out_refs[0].dtype)


def _store_bias(acc, extra_refs, out_refs):
    out_refs[0][...] = (acc + extra_refs[0][...]).astype(out_refs[0].dtype)


def _store_relu2(acc, extra_refs, out_refs):
    r = jnp.maximum(acc, 0.0)
    out_refs[0][...] = (r * r).astype(out_refs[0].dtype)
    out_refs[1][...] = acc.astype(out_refs[1].dtype)


def _store_relu2_grad(acc, extra_refs, out_refs):
    hid = extra_refs[0][...].astype(F32)
    out_refs[0][...] = (acc * (2.0 * jnp.maximum(hid, 0.0))).astype(out_refs[0].dtype)


def _mm_cols(name, a, wc, out_dtype, epilogue=_store, bias=None, n_out=1, deps=()):
    s, kdim = a.shape
    _, _, n = wc.shape
    tm, tn, tk = _pick(s, MM_TM), _pick(n, MM_TN), _pick(kdim, MM_TK)
    per = n // tn
    extras, extra_specs = (), ()
    if bias is not None:
        extras, extra_specs = (bias,), (pl.BlockSpec((1, tn), lambda i, j, k: (0, j)),)
    out = jax.ShapeDtypeStruct((s, N_DEV * n), out_dtype)
    spec = pl.BlockSpec((tm, tn), lambda i, j, k: (i, j))
    return _matmul(
        name, a, wc,
        pl.BlockSpec((tm, tk), lambda i, j, k: (i, k)),
        pl.BlockSpec((None, tk, tn), lambda i, j, k: (j // per, k, j % per)),
        (s // tm, N_DEV * per, kdim // tk), (tm, tn),
        [out] * n_out, [spec] * n_out, epilogue, extras=extras, extra_specs=extra_specs, deps=deps)


def _mm_rows(name, a, wr, out_dtype, bias=None, deps=()):
    s, kdim = a.shape
    w = wr.reshape(kdim, wr.shape[2])
    n = w.shape[1]
    tm, tn, tk = _pick(s, MM_TM), _pick(n, MM_TN), _pick(kdim, MM_TK)
    if kdim * n <= MM_RESIDENT:
        tm, tn = _pick(s, MM_TM // 2), n
    extras, extra_specs, epilogue = (), (), _store
    if bias is not None:
        extras, extra_specs, epilogue = (bias,), (pl.BlockSpec((1, tn), lambda i, j, k: (0, j)),), _store_bias
    return _matmul(
        name, a, w,
        pl.BlockSpec((tm, tk), lambda i, j, k: (i, k)),
        pl.BlockSpec((tk, tn), lambda i, j, k: (k, j)),
        (s // tm, n // tn, kdim // tk), (tm, tn),
        [jax.ShapeDtypeStruct((s, n), out_dtype)], [pl.BlockSpec((tm, tn), lambda i, j, k: (i, j))],
        epilogue, extras=extras, extra_specs=extra_specs, deps=deps)[0]


def _mm_cols_t(name, dy, wc, out_dtype, deps=()):
    s, _ = dy.shape
    _, kdim, n = wc.shape
    tm, tn, tk = _pick(s, MM_TM), _pick(kdim, MM_TN), _pick(n, MM_TK)
    per = n // tk
    return _matmul(
        name, dy, wc,
        pl.BlockSpec((tm, tk), lambda i, j, k: (i, k)),
        pl.BlockSpec((None, tn, tk), lambda i, j, k: (k // per, j, k % per)),
        (s // tm, kdim // tn, N_DEV * per), (tm, tn),
        [jax.ShapeDtypeStruct((s, kdim), out_dtype)], [pl.BlockSpec((tm, tn), lambda i, j, k: (i, j))],
        _store, tb=True, deps=deps)[0]


def _mm_rows_t(name, dy, wr, out_dtype, epilogue=_store, extra=None, deps=()):
    s, n = dy.shape
    w = wr.reshape(-1, n)
    kdim = w.shape[0]
    tm, tn, tk = _pick(s, MM_TM), _pick(kdim, MM_TN), _pick(n, MM_TK)
    if kdim * n <= MM_RESIDENT:
        tm, tn = _pick(s, MM_TM // 2), kdim
    extras, extra_specs = (), ()
    if extra is not None:
        extras, extra_specs = (extra,), (pl.BlockSpec((tm, tn), lambda i, j, k: (i, j)),)
    return _matmul(
        name, dy, w,
        pl.BlockSpec((tm, tk), lambda i, j, k: (i, k)),
        pl.BlockSpec((tn, tk), lambda i, j, k: (j, k)),
        (s // tm, kdim // tn, n // tk), (tm, tn),
        [jax.ShapeDtypeStruct((s, kdim), out_dtype)], [pl.BlockSpec((tm, tn), lambda i, j, k: (i, j))],
        epilogue, tb=True, extras=extras, extra_specs=extra_specs, deps=deps)[0]


def _grad_cols(name, h, dy, n):
    s, kdim = h.shape
    tm, tn, tk = _pick(kdim, MM_TM), _pick(n, MM_TN), _pick(s, MM_TK)
    per = n // tn
    return _matmul(
        name, h, dy,
        pl.BlockSpec((tk, tm), lambda i, j, k: (k, i)),
        pl.BlockSpec((tk, tn), lambda i, j, k: (k, j)),
        (kdim // tm, N_DEV * per, s // tk), (tm, tn),
        [jax.ShapeDtypeStruct((N_DEV, kdim, n), COMM_DTYPE)],
        [pl.BlockSpec((None, tm, tn), lambda i, j, k: (j // per, i, j % per))],
        _store, ta=True)[0].reshape(N_CHIP, 2, kdim, n)


def _grad_rows(name, v, dy, kk):
    s, kdim = v.shape
    n = dy.shape[1]
    tm, tn, tk = _pick(kdim, MM_TM), _pick(n, MM_TN), _pick(s, MM_TK)
    return _matmul(
        name, v, dy,
        pl.BlockSpec((tk, tm), lambda i, j, k: (k, i)),
        pl.BlockSpec((tk, tn), lambda i, j, k: (k, j)),
        (kdim // tm, n // tn, s // tk), (tm, tn),
        [jax.ShapeDtypeStruct((kdim, n), COMM_DTYPE)],
        [pl.BlockSpec((tm, tn), lambda i, j, k: (i, j))],
        _store, ta=True)[0].reshape(N_CHIP, 2, kk, n)


def _rms(v):
    return lax.rsqrt(jnp.mean(v * v, axis=-1, keepdims=True) + RMS_EPS)


def _colsum(v):
    return jnp.sum(v, axis=0, keepdims=True)


def _vec_spec(width):
    return pl.BlockSpec((1, width), lambda i: (0, 0))


def _residual_fwd(name, x, post=None, pre=None, target=None, deps=()):
    s, d = x.shape
    ts = _pick(s, ROW_TILE, 8)
    row = pl.BlockSpec((ts, d), lambda i: (i, 0))
    ins, specs = [x], [row]
    if post is not None:
        ins += list(post)
        specs += [row, _vec_spec(d), _vec_spec(d)]
    if pre is not None:
        ins += list(pre)
        specs += [_vec_spec(d)] * 3
    if target is not None:
        ins.append(target)
        specs.append(row)
    ins += list(deps)
    specs += [ANY_SPEC] * len(deps)
    outs, out_specs = [], []
    if post is not None:
        outs.append(jax.ShapeDtypeStruct((s, d), F32))
        out_specs.append(row)
    if pre is not None:
        outs.append(jax.ShapeDtypeStruct((s, d), ACT_DTYPE))
        out_specs.append(row)
    if target is not None:
        outs += [jax.ShapeDtypeStruct((s, d), F32), jax.ShapeDtypeStruct((8, LANE), F32)]
        out_specs += [row, pl.BlockSpec((8, LANE), lambda i: (0, 0))]

    def body(*refs):
        refs = list(refs)
        xv = refs.pop(0)[...]
        if post is not None:
            y_ref, gp_ref, gt_ref = refs[:3]
            del refs[:3]
        if pre is not None:
            g_ref, sc_ref, sh_ref = refs[:3]
            del refs[:3]
        if target is not None:
            t_ref = refs.pop(0)
        del refs[:len(deps)]
        if post is not None:
            yv = y_ref[...]
            xv = xv + gt_ref[...] * ((yv * _rms(yv)) * gp_ref[...])
            refs.pop(0)[...] = xv
        if pre is not None:
            hv = ((xv * _rms(xv)) * g_ref[...]) * (1.0 + sc_ref[...]) + sh_ref[...]
            refs.pop(0)[...] = hv.astype(ACT_DTYPE)
        if target is not None:
            dx_ref, loss_ref = refs
            err = xv - t_ref[...]
            dx_ref[...] = err * (1.0 / d)

            @pl.when(pl.program_id(0) == 0)
            def _():
                loss_ref[...] = jnp.zeros_like(loss_ref)

            loss_ref[...] += (0.5 / d) * jnp.sum(err * err)

    return pl.pallas_call(
        body, name=name, grid=(s // ts,), in_specs=specs, out_specs=out_specs, out_shape=outs,
        compiler_params=_params("arbitrary"),
    )(*ins)


def _residual_bwd(name, g_out, pre=None, post=None):
    s, d = g_out.shape
    ts = _pick(s, ROW_TILE, 8)
    row = pl.BlockSpec((ts, d), lambda i: (i, 0))
    ins, specs = [g_out], [row]
    outs, out_specs = [], []
    if pre is not None:
        ins += list(pre)
        specs += [row, row, _vec_spec(d), _vec_spec(d)]
        outs.append(jax.ShapeDtypeStruct((s, d), F32))
        out_specs.append(row)
    if post is not None:
        ins += list(post)
        specs += [row, _vec_spec(d), _vec_spec(d)]
        outs.append(jax.ShapeDtypeStruct((s, d), ACT_DTYPE))
        out_specs.append(row)
    outs.append(jax.ShapeDtypeStruct((8, d), F32))
    out_specs.append(pl.BlockSpec((8, d), lambda i: (0, 0)))

    def body(*refs):
        refs = list(refs)
        g = refs.pop(0)[...]
        if pre is not None:
            dh_ref, x_ref, gpre_ref, sc_ref = refs[:4]
            del refs[:4]
        if post is not None:
            y_ref, gpost_ref, gt_ref = refs[:3]
            del refs[:3]
        sums_ref = refs[-1]

        @pl.when(pl.program_id(0) == 0)
        def _():
            sums_ref[...] = jnp.zeros_like(sums_ref)

        if pre is not None:
            dh, xv = dh_ref[...], x_ref[...]
            r = _rms(xv)
            nrm = xv * r
            d_rn = dh * (1.0 + sc_ref[...])
            sums_ref[0:1, :] += _colsum(dh)
            sums_ref[1:2, :] += _colsum(dh * (nrm * gpre_ref[...]))
            sums_ref[2:3, :] += _colsum(d_rn * nrm)
            dn = d_rn * gpre_ref[...]
            g = g + r * (dn - nrm * jnp.mean(dn * nrm, axis=-1, keepdims=True))
            refs.pop(0)[...] = g
        if post is not None:
            yv = y_ref[...]
            r = _rms(yv)
            nrm = yv * r
            sums_ref[3:4, :] += _colsum(g * (nrm * gpost_ref[...]))
            d_o = g * gt_ref[...]
            sums_ref[4:5, :] += _colsum(d_o * nrm)
            dn = d_o * gpost_ref[...]
            dy = r * (dn - nrm * jnp.mean(dn * nrm, axis=-1, keepdims=True))
            sums_ref[5:6, :] += _colsum(dy)
            refs.pop(0)[...] = dy.astype(ACT_DTYPE)

    return pl.pallas_call(
        body, name=name, grid=(s // ts,), in_specs=specs, out_specs=out_specs, out_shape=outs,
        compiler_params=_params("arbitrary"),
    )(*ins)


def _chunks(d):
    cw = min(CONV_CHUNK, d)
    return [(c * cw, cw) for c in range(d // cw)]


def _shift_rows(buf, sh, c0, cw, rows):
    for r in range(1, 8):
        sh[r - 1] = buf[r:r + rows, c0:c0 + cw]


def _window(buf, sh, c0, cw, offset, ts):
    q, r = divmod(offset, 8)
    if r == 0:
        return buf[8 * q:8 * q + ts, c0:c0 + cw]
    return sh[r - 1, 8 * q:8 * q + ts, :]


def _glu_conv_ln_swish(name, u, dw, dwb, ln_g, ln_b):
    s, d2 = u.shape
    d = d2 // 2
    taps = dw.shape[0]
    ts, halo = _pick(s, CONV_TILE, 8), CONV_HALO
    lead = halo - (taps - 1)
    per = ts // halo
    cw = min(CONV_CHUNK, d)

    def body(cur_ref, prev_ref, dw_ref, dwb_ref, g_ref, b_ref, cv_ref, v_ref, buf, sh):
        i = pl.program_id(0)
        buf[0:halo, :] = jnp.where(i > 0, prev_ref[:, :d] * jax.nn.sigmoid(prev_ref[:, d:]), 0.0)
        buf[halo:, :] = cur_ref[:, :d] * jax.nn.sigmoid(cur_ref[:, d:])
        for c0, _ in _chunks(d):
            _shift_rows(buf, sh, c0, cw, ts + halo - 8)
            acc = jnp.zeros((ts, cw), F32)
            for k in range(taps):
                acc += dw_ref[k:k + 1, c0:c0 + cw] * _window(buf, sh, c0, cw, lead + k, ts)
            cv_ref[:, c0:c0 + cw] = acc + dwb_ref[:, c0:c0 + cw]
        cv = cv_ref[...]
        mu = jnp.mean(cv, axis=-1, keepdims=True)
        xc = cv - mu
        var = jnp.mean(xc * xc, axis=-1, keepdims=True)
        ln = (xc * lax.rsqrt(var + LN_EPS)) * g_ref[...] + b_ref[...]
        v_ref[...] = (ln * jax.nn.sigmoid(ln)).astype(v_ref.dtype)

    row = pl.BlockSpec((ts, d), lambda i: (i, 0))
    return pl.pallas_call(
        body, name=name, grid=(s // ts,),
        in_specs=[pl.BlockSpec((ts, d2), lambda i: (i, 0)),
                  pl.BlockSpec((halo, d2), lambda i: (jnp.maximum(i * per - 1, 0), 0)),
                  pl.BlockSpec((taps, d), lambda i: (0, 0)), _vec_spec(d), _vec_spec(d), _vec_spec(d)],
        out_specs=[row, row],
        out_shape=[jax.ShapeDtypeStruct((s, d), F32), jax.ShapeDtypeStruct((s, d), ACT_DTYPE)],
        scratch_shapes=[pltpu.VMEM((ts + halo, d), F32), pltpu.VMEM((7, ts + halo - 8, cw), F32)],
        compiler_params=_params("parallel"),
    )(u, u, dw, dwb, ln_g, ln_b)


def _ln_swish_bwd(name, dv, cv, ln_g, ln_b, deps=()):
    s, d = cv.shape
    ts = _pick(s, ROW_TILE, 8)

    def body(dv_ref, cv_ref, g_ref, b_ref, *rest):
        dcv_ref, sums_ref = rest[len(deps):]

        @pl.when(pl.program_id(0) == 0)
        def _():
            sums_ref[...] = jnp.zeros_like(sums_ref)

        cv = cv_ref[...]
        mu = jnp.mean(cv, axis=-1, keepdims=True)
        xc = cv - mu
        rstd = lax.rsqrt(jnp.mean(xc * xc, axis=-1, keepdims=True) + LN_EPS)
        nhat = xc * rstd
        ln = nhat * g_ref[...] + b_ref[...]
        sg = jax.nn.sigmoid(ln)
        dl = dv_ref[...] * (sg * (1.0 + ln * (1.0 - sg)))
        sums_ref[0:1, :] += _colsum(dl * nhat)
        sums_ref[1:2, :] += _colsum(dl)
        dn = dl * g_ref[...]
        dcv = rstd * (dn - jnp.mean(dn, axis=-1, keepdims=True)
                      - nhat * jnp.mean(dn * nhat, axis=-1, keepdims=True))
        sums_ref[2:3, :] += _colsum(dcv)
        dcv_ref[...] = dcv

    row = pl.BlockSpec((ts, d), lambda i: (i, 0))
    return pl.pallas_call(
        body, name=name, grid=(s // ts,),
        in_specs=[row, row, _vec_spec(d), _vec_spec(d)] + [ANY_SPEC] * len(deps),
        out_specs=[row, pl.BlockSpec((8, d), lambda i: (0, 0))],
        out_shape=[jax.ShapeDtypeStruct((s, d), F32), jax.ShapeDtypeStruct((8, d), F32)],
        compiler_params=_params("arbitrary"),
    )(dv, cv, ln_g, ln_b, *deps)


def _conv_glu_bwd(name, dcv, u, dw):
    s, d = dcv.shape
    taps = dw.shape[0]
    taps8 = -(-taps // 8) * 8
    ts, halo = _pick(s, CONV_TILE, 8), CONV_HALO
    lead = halo - (taps - 1)
    per = ts // halo
    n_tiles = s // ts
    cw = min(CONV_CHUNK, d)

    def body(dcv_ref, next_ref, u_ref, prev_ref, dw_ref, du_ref, ddw_ref, sums_ref, nbuf, pbuf, nsh, psh):
        i = pl.program_id(0)

        @pl.when(i == 0)
        def _():
            ddw_ref[...] = jnp.zeros_like(ddw_ref)
            sums_ref[...] = jnp.zeros_like(sums_ref)

        nbuf[0:ts, :] = dcv_ref[...]
        nbuf[ts:, :] = jnp.where(i < n_tiles - 1, next_ref[...], 0.0)
        pbuf[0:halo, :] = jnp.where(i > 0, prev_ref[:, :d] * jax.nn.sigmoid(prev_ref[:, d:]), 0.0)
        pbuf[halo:, :] = u_ref[:, :d] * jax.nn.sigmoid(u_ref[:, d:])
        for c0, _ in _chunks(d):
            _shift_rows(nbuf, nsh, c0, cw, ts + halo - 8)
            _shift_rows(pbuf, psh, c0, cw, ts + halo - 8)
            dcv = dcv_ref[:, c0:c0 + cw]
            dglu = jnp.zeros((ts, cw), F32)
            for k in range(taps):
                dglu += dw_ref[k:k + 1, c0:c0 + cw] * _window(nbuf, nsh, c0, cw, taps - 1 - k, ts)
                ddw_ref[k:k + 1, c0:c0 + cw] += _colsum(dcv * _window(pbuf, psh, c0, cw, lead + k, ts))
            a = u_ref[:, c0:c0 + cw]
            sg = jax.nn.sigmoid(u_ref[:, d + c0:d + c0 + cw])
            da = dglu * sg
            dg = dglu * a * (sg * (1.0 - sg))
            du_ref[:, c0:c0 + cw] = da.astype(du_ref.dtype)
            du_ref[:, d + c0:d + c0 + cw] = dg.astype(du_ref.dtype)
            sums_ref[0:1, c0:c0 + cw] += _colsum(da)
            sums_ref[0:1, d + c0:d + c0 + cw] += _colsum(dg)

    row = pl.BlockSpec((ts, d), lambda i: (i, 0))
    wide = pl.BlockSpec((ts, 2 * d), lambda i: (i, 0))
    return pl.pallas_call(
        body, name=name, grid=(n_tiles,),
        in_specs=[row, pl.BlockSpec((halo, d), lambda i: (jnp.minimum((i + 1) * per, s // halo - 1), 0)),
                  wide, pl.BlockSpec((halo, 2 * d), lambda i: (jnp.maximum(i * per - 1, 0), 0)),
                  pl.BlockSpec((taps, d), lambda i: (0, 0))],
        out_specs=[wide, pl.BlockSpec((taps8, d), lambda i: (0, 0)), pl.BlockSpec((8, 2 * d), lambda i: (0, 0))],
        out_shape=[jax.ShapeDtypeStruct((s, 2 * d), ACT_DTYPE), jax.ShapeDtypeStruct((taps8, d), F32),
                   jax.ShapeDtypeStruct((8, 2 * d), F32)],
        scratch_shapes=[pltpu.VMEM((ts + halo, d), F32), pltpu.VMEM((ts + halo, d), F32),
                        pltpu.VMEM((7, ts + halo - 8, cw), F32), pltpu.VMEM((7, ts + halo - 8, cw), F32)],
        compiler_params=_params("arbitrary"),
    )(dcv, dcv, u, u, dw)


def _short_conv_fwd(name, z, w):
    s, d3 = z.shape
    d = d3 // 3
    taps = w.shape[0]
    ts, halo = _pick(s, CONV_TILE, 8), SHORT_HALO
    lead = halo - (taps - 1)
    per = ts // halo

    def body(z_ref, prev_ref, w_ref, q_ref, pbuf):
        i = pl.program_id(0)
        pbuf[0:halo, :] = jnp.where(i > 0, prev_ref[:, d:2 * d] * prev_ref[:, 2 * d:], 0.0)
        pbuf[halo:, :] = z_ref[:, d:2 * d] * z_ref[:, 2 * d:]
        for c0, cw in _chunks(d):
            acc = jnp.zeros((ts, cw), F32)
            for k in range(taps):
                acc += w_ref[k:k + 1, c0:c0 + cw] * pbuf[lead + k:lead + k + ts, c0:c0 + cw]
            q_ref[:, c0:c0 + cw] = (z_ref[:, c0:c0 + cw] * acc).astype(q_ref.dtype)

    return pl.pallas_call(
        body, name=name, grid=(s // ts,),
        in_specs=[pl.BlockSpec((ts, d3), lambda i: (i, 0)),
                  pl.BlockSpec((halo, d3), lambda i: (jnp.maximum(i * per - 1, 0), 0)),
                  pl.BlockSpec((taps, d), lambda i: (0, 0))],
        out_specs=pl.BlockSpec((ts, d), lambda i: (i, 0)),
        out_shape=jax.ShapeDtypeStruct((s, d), ACT_DTYPE),
        scratch_shapes=[pltpu.VMEM((ts + halo, d), F32)],
        compiler_params=_params("parallel"),
    )(z, z, w)


def _short_conv_bwd(name, dq, z, w, deps=()):
    s, d3 = z.shape
    d = d3 // 3
    taps = w.shape[0]
    ts, halo = _pick(s, CONV_TILE, 8), SHORT_HALO
    lead = halo - (taps - 1)
    per = ts // halo
    n_tiles = s // ts

    def body(dq_ref, dqn_ref, z_ref, zp_ref, zn_ref, w_ref, *rest):
        dz_ref, sums_ref, pbuf, ubuf = rest[len(deps):]
        i = pl.program_id(0)

        @pl.when(i == 0)
        def _():
            sums_ref[...] = jnp.zeros_like(sums_ref)

        pbuf[0:halo, :] = jnp.where(i > 0, zp_ref[:, d:2 * d] * zp_ref[:, 2 * d:], 0.0)
        pbuf[halo:, :] = z_ref[:, d:2 * d] * z_ref[:, 2 * d:]
        ubuf[0:ts, :] = dq_ref[...] * z_ref[:, 0:d]
        ubuf[ts:, :] = jnp.where(i < n_tiles - 1, dqn_ref[...] * zn_ref[:, 0:d], 0.0)
        for c0, cw in _chunks(d):
            du = ubuf[0:ts, c0:c0 + cw]
            conv = jnp.zeros((ts, cw), F32)
            dp = jnp.zeros((ts, cw), F32)
            for k in range(taps):
                wk = w_ref[k:k + 1, c0:c0 + cw]
                shifted = pbuf[lead + k:lead + k + ts, c0:c0 + cw]
                conv += wk * shifted
                dp += wk * ubuf[taps - 1 - k:taps - 1 - k + ts, c0:c0 + cw]
                sums_ref[k:k + 1, c0:c0 + cw] += _colsum(du * shifted)
            dz_ref[:, c0:c0 + cw] = (dq_ref[:, c0:c0 + cw] * conv).astype(dz_ref.dtype)
            dz_ref[:, d + c0:d + c0 + cw] = (dp * z_ref[:, 2 * d + c0:2 * d + c0 + cw]).astype(dz_ref.dtype)
            dz_ref[:, 2 * d + c0:2 * d + c0 + cw] = (dp * z_ref[:, d + c0:d + c0 + cw]).astype(dz_ref.dtype)

    last = s // halo - 1
    return pl.pallas_call(
        body, name=name, grid=(n_tiles,),
        in_specs=[pl.BlockSpec((ts, d), lambda i: (i, 0)),
                  pl.BlockSpec((halo, d), lambda i: (jnp.minimum((i + 1) * per, last), 0)),
                  pl.BlockSpec((ts, d3), lambda i: (i, 0)),
                  pl.BlockSpec((halo, d3), lambda i: (jnp.maximum(i * per - 1, 0), 0)),
                  pl.BlockSpec((halo, d3), lambda i: (jnp.minimum((i + 1) * per, last), 0)),
                  pl.BlockSpec((taps, d), lambda i: (0, 0))] + [ANY_SPEC] * len(deps),
        out_specs=[pl.BlockSpec((ts, d3), lambda i: (i, 0)), pl.BlockSpec((8, d), lambda i: (0, 0))],
        out_shape=[jax.ShapeDtypeStruct((s, d3), ACT_DTYPE), jax.ShapeDtypeStruct((8, d), F32)],
        scratch_shapes=[pltpu.VMEM((ts + halo, d), F32), pltpu.VMEM((ts + halo, d), F32)],
        compiler_params=_params("arbitrary"),
    )(dq, dq, z, z, z, w, *deps)


def _silu(v):
    return v * jax.nn.sigmoid(v)


def _modulation(name, c_all, mod_w, mod_b_cols):
    nl, d, n = mod_w.shape
    b = c_all.shape[0]
    tn = _pick(n, 512)

    def body(c_ref, w_ref, b_ref, o_ref):
        ca = _silu(c_ref[...]).astype(MXU_DTYPE)
        o_ref[...] = jnp.dot(ca, w_ref[...].astype(MXU_DTYPE), preferred_element_type=F32) + b_ref[...]

    return pl.pallas_call(
        body, name=name, grid=(nl, n // tn),
        in_specs=[pl.BlockSpec((b, d), lambda l, j: (0, 0)),
                  pl.BlockSpec((None, d, tn), lambda l, j: (l, 0, j)),
                  pl.BlockSpec((None, 1, tn), lambda l, j: (l, 0, j))],
        out_specs=pl.BlockSpec((None, b, tn), lambda l, j: (l, 0, j)),
        out_shape=jax.ShapeDtypeStruct((nl, b, n), F32),
        compiler_params=_params("parallel", "parallel"),
    )(c_all, mod_w, mod_b_cols.reshape(nl, 1, n))


def _adamw(g, w, m, v):
    m = ADAM_B1 * m + (1.0 - ADAM_B1) * g
    v = ADAM_B2 * v + (1.0 - ADAM_B2) * (g * g)
    m_hat = m / (1.0 - ADAM_B1 ** ADAM_STEP)
    v_hat = v / (1.0 - ADAM_B2 ** ADAM_STEP)
    delta = -ADAM_LR * (m_hat / (jnp.sqrt(v_hat) + ADAM_EPS) + ADAM_WD * w)
    return delta, m, v


def _write_update(g, w_ref, m_ref, v_ref, outs):
    delta, m, v = _adamw(g, w_ref[...], m_ref[...], v_ref[...])
    outs[0][...] = g
    outs[1][...] = delta
    outs[2][...] = m
    outs[3][...] = v


def _modulation_update(name, c_all_t, dmod, w, m, v, deps=()):
    nl, d, n = w.shape
    b = c_all_t.shape[1]
    tr = _pick(d, 256, 8)

    def body(c_ref, dm_ref, w_ref, m_ref, v_ref, *rest):
        outs = rest[len(deps):]
        ca = _silu(c_ref[...])
        dm = dm_ref[...]
        g = ca[:, 0:1] * dm[0:1, :]
        for i in range(1, b):
            g += ca[:, i:i + 1] * dm[i:i + 1, :]
        _write_update(g, w_ref, m_ref, v_ref, outs)

    blk = pl.BlockSpec((None, tr, n), lambda l, r: (l, r, 0))
    return pl.pallas_call(
        body, name=name, grid=(nl, d // tr),
        in_specs=[pl.BlockSpec((tr, b), lambda l, r: (r, 0)), pl.BlockSpec((None, b, n), lambda l, r: (l, 0, 0)),
                  blk, blk, blk] + [ANY_SPEC] * len(deps),
        out_specs=[blk] * 4, out_shape=[jax.ShapeDtypeStruct(w.shape, F32)] * 4,
        compiler_params=_params("parallel", "parallel"),
    )(c_all_t, dmod, w, m, v, *deps)


def _weight_update(name, parts, w, m, v):
    nl, rows, cols = w.shape
    tr = _pick(rows, max(8, (1 << 18) // cols), 8)

    def body(*refs):
        p_refs = refs[:nl]
        w_ref, m_ref, v_ref = refs[nl:nl + 3]
        outs = refs[nl + 3:]
        for q in range(nl):
            @pl.when(pl.program_id(0) == q)
            def _(q=q):
                g = p_refs[q][0].astype(F32)
                for k in range(1, N_CHIP):
                    g += p_refs[q][k].astype(F32)
                _write_update(g, w_ref, m_ref, v_ref, outs)

    def part_spec(q):
        return pl.BlockSpec((N_CHIP, tr, cols), lambda l, r: (0, jnp.where(l == q, r, 0), 0))

    blk = pl.BlockSpec((None, tr, cols), lambda l, r: (l, r, 0))
    return pl.pallas_call(
        body, name=name, grid=(nl, rows // tr),
        in_specs=[part_spec(q) for q in range(nl)] + [blk, blk, blk],
        out_specs=[blk] * 4, out_shape=[jax.ShapeDtypeStruct(w.shape, F32)] * 4,
        compiler_params=_params("arbitrary", "arbitrary"),
    )(*parts, w, m, v)


def _vector_update(name, gathered, w, m, v):
    _, rows, cols = gathered.shape
    rw = w.shape[0]

    def body(g_ref, w_ref, m_ref, v_ref, tot_ref, d_ref, m2_ref, v2_ref):
        tot = g_ref[0]
        for k in range(1, N_DEV):
            tot += g_ref[k]
        tot_ref[...] = tot
        delta, m2, v2 = _adamw(tot[0:rw], w_ref[...], m_ref[...], v_ref[...])
        d_ref[...] = delta
        m2_ref[...] = m2
        v2_ref[...] = v2

    vm = pl.BlockSpec(memory_space=pltpu.VMEM)
    return pl.pallas_call(
        body, name=name, in_specs=[vm] * 4, out_specs=[vm] * 4,
        out_shape=[jax.ShapeDtypeStruct((rows, cols), F32)] + [jax.ShapeDtypeStruct((rw, cols), F32)] * 3,
        compiler_params=pltpu.CompilerParams(vmem_limit_bytes=V7X_VMEM_LIMIT_BYTES),
    )(gathered, w, m, v)


def _plain_update(name, g, w, m, v):
    def body(g_ref, w_ref, m_ref, v_ref, d_ref, m2_ref, v2_ref):
        delta, m2, v2 = _adamw(g_ref[...], w_ref[...], m_ref[...], v_ref[...])
        d_ref[...] = delta
        m2_ref[...] = m2
        v2_ref[...] = v2

    vm = pl.BlockSpec(memory_space=pltpu.VMEM)
    return pl.pallas_call(
        body, name=name, in_specs=[vm] * 4, out_specs=[vm] * 3,
        out_shape=[jax.ShapeDtypeStruct(w.shape, F32)] * 3,
    )(g, w, m, v)


def _rows(a, width, mult=8):
    r = a.reshape(-1, width)
    pad = -r.shape[0] % mult
    return jnp.pad(r, ((0, pad), (0, 0))) if pad else r


def _pack(blocks, width):
    parts, spans, at = [], [], 0
    for a in blocks:
        n = a.size // width
        p = _rows(a, width)
        parts.append(p)
        spans.append((at, n))
        at += p.shape[0]
    return jnp.concatenate(parts, axis=0), spans


def kernel(x, c, mod_w, mod_b, pre_mix_g, post_mix_g, pre_ffn_g, post_ffn_g, a_w1, a_b1, a_dw, a_dwb, a_ln_g, a_ln_b, a_w2, a_b2, b_w_in, b_conv, b_w_out, f_w1, f_w2, loss_target, m_mod_w, m_mod_b, m_pre_mix_g, m_post_mix_g, m_pre_ffn_g, m_post_ffn_g, m_a_w1, m_a_b1, m_a_dw, m_a_dwb, m_a_ln_g, m_a_ln_b, m_a_w2, m_a_b2, m_b_w_in, m_b_conv, m_b_w_out, m_f_w1, m_f_w2, v_mod_w, v_mod_b, v_pre_mix_g, v_post_mix_g, v_pre_ffn_g, v_post_ffn_g, v_a_w1, v_a_b1, v_a_dw, v_a_dwb, v_a_ln_g, v_a_ln_b, v_a_w2, v_a_b2, v_b_w_in, v_b_conv, v_b_w_out, v_f_w1, v_f_w2):
    depth, d = pre_mix_g.shape
    n_a, n_b = a_w1.shape[0], b_w_in.shape[0]
    s = x.shape[1]
    dsh = d // N_DEV
    taps_a, taps_b = a_dw.shape[1], b_conv.shape[1]
    px, py, pc = _pos()
    me = 4 * px + 2 * py + pc
    x0 = x.reshape(s, d)
    target = loss_target.reshape(s, d)

    me1 = jnp.reshape(me, (1,)).astype(jnp.int32)

    def members(i):
        j = i // 2
        mix = [(a_w1, j), (a_w2, j)] if i % 2 == 0 else [(b_w_in, j), (b_w_out, j)]
        return mix, [(f_w1, i), (f_w2, i)]

    def start_group(tag, group, after=()):
        return _gather_start(tag, [_place("place_%s_%d" % (tag, q), w, layer, me1)
                                   for q, (w, layer) in enumerate(group)], after)

    packed0, spans0 = _pack([c, a_dw, b_conv], dsh)
    got0 = _small_gather("gather_cond", packed0)

    def full_width(span):
        at, n = span
        return jnp.transpose(got0[:, at:at + n, :], (1, 0, 2)).reshape(n, d)

    c_all = got0[:, spans0[0][0]:spans0[0][0] + spans0[0][1], :].reshape(N_DEV, d)
    a_dw_full = full_width(spans0[1]).reshape(n_a, taps_a, d)
    b_conv_full = full_width(spans0[2]).reshape(n_b, taps_b, d)

    n_mod = mod_w.shape[2]
    mod_b_cols = lax.dynamic_slice_in_dim(mod_b, me * n_mod, n_mod, axis=1)
    mod_local = _modulation("modulation", c_all, mod_w, mod_b_cols)
    got1 = _small_gather("gather_mod", mod_local.reshape(depth * N_DEV, n_mod))
    mod_me = lax.dynamic_index_in_dim(got1.reshape(N_DEV, depth, N_DEV, n_mod), me, axis=2, keepdims=False)
    mod_me = jnp.transpose(mod_me, (1, 0, 2)).reshape(depth, 6, 1, d)

    def vec(a, i):
        return a[i].reshape(1, -1)

    def pass_groups(tag, handles, after):
        for q in range(len(handles)):
            handles[q] = _gather_pass("%s%d" % (tag, q), handles[q], after)
            after = handles[q]["token"]
        return after

    def wait_groups(tag, handles, after):
        return sum([_gather_wait("%s%d" % (tag, q), hq, after) for q, hq in enumerate(handles)], [])

    mix_copy = [start_group("l0m", members(0)[0], (got1,))]
    ffn_copy = [start_group("l0u", members(0)[1][:1], (mix_copy[0]["token"],))]
    ffn_copy.append(start_group("l0d", members(0)[1][1:], (ffn_copy[0]["token"],)))
    saved = []
    xs = x0
    h = _residual_fwd("fwd_in", xs, pre=(vec(pre_mix_g, 0), mod_me[0, 1], mod_me[0, 0]),
                      deps=[hq["token"] for hq in ffn_copy])[0]
    w_mix = wait_groups("l0m", mix_copy, pass_groups("l0m", mix_copy, h))
    for i in range(depth):
        j = i // 2
        sh_m, sc_m, gt_m, sh_f, sc_f, gt_f = [mod_me[i, q] for q in range(6)]
        keep = {"x_mix": xs, "h_mix": h}
        w1g, w2g = w_mix
        early = i > 0
        if i % 2 == 0:
            u = _mm_cols("a_w1", h, w1g, F32, _store_bias, bias=vec(a_b1, j))[0]
            deps = [pass_groups("l%df" % i, ffn_copy, u)] if early else []
            cv, v = _glu_conv_ln_swish("a_conv", u, a_dw_full[j], vec(a_dwb, j), vec(a_ln_g, j), vec(a_ln_b, j))
            y = _mm_rows("a_w2", v, w2g, F32, bias=vec(a_b2, j), deps=deps)
            keep.update(u=u, cv=cv, v=v)
        else:
            z = _mm_cols("b_w_in", h, w1g, F32)[0]
            deps = [pass_groups("l%df" % i, ffn_copy, z)] if early else []
            q = _short_conv_fwd("b_conv", z, b_conv_full[j])
            y = _mm_rows("b_w_out", q, w2g, F32, deps=deps)
            keep.update(z=z, q=q)
        after = y if early else pass_groups("l%df" % i, ffn_copy, y)
        deps = []
        if i + 1 < depth:
            mix_copy = [start_group("l%dm" % (i + 1), members(i + 1)[0])]
            ffn_next = [start_group("l%df" % (i + 1), members(i + 1)[1])]
            deps = [mix_copy[0]["token"], ffn_next[0]["token"]]
        fw1g, fw2g = wait_groups("l%df" % i, ffn_copy, after)
        keep.update(w1g=w1g, w2g=w2g, fw1g=fw1g, fw2g=fw2g, y_mix=y)
        xs, h = _residual_fwd("fwd_mid", xs, post=(y, vec(post_mix_g, i), gt_m),
                              pre=(vec(pre_ffn_g, i), sc_f, sh_f))
        keep.update(x_ffn=xs, h_ffn=h)
        act, hid = _mm_cols("f_w1", h, fw1g, ACT_DTYPE, _store_relu2, n_out=2, deps=deps)
        deps = [pass_groups("l%dm" % (i + 1), mix_copy, act)] if i + 1 < depth else []
        y = _mm_rows("f_w2", act, fw2g, F32, deps=deps)
        keep.update(act=act, hid=hid, y_ffn=y)
        saved.append(keep)
        if i + 1 < depth:
            w_mix = wait_groups("l%dm" % (i + 1), mix_copy, y)
            ffn_copy = ffn_next
            nxt = mod_me[i + 1]
            xs, h = _residual_fwd("fwd_next", xs, post=(y, vec(post_ffn_g, i), gt_f),
                                  pre=(vec(pre_mix_g, i + 1), nxt[1], nxt[0]))
        else:
            _, g, loss_blk = _residual_fwd("fwd_loss", xs, post=(y, vec(post_ffn_g, i), gt_f), target=target)

    where = jnp.stack([pc, 2 * px + py]).astype(jnp.int32)
    land = {"a_w1": [None] * n_a, "a_w2": [None] * n_a, "b_w_in": [None] * n_b, "b_w_out": [None] * n_b,
            "f_w1": [None] * depth, "f_w2": [None] * depth}
    reductions = []

    def reduce_group(tag, slots, swap, after):
        grads, theirs = _sibling_wait(tag, swap, after)
        pairs = [_chip_partial("chip_partial_%s_%d" % (tag, q), gq, tq, where)
                 for q, (gq, tq) in enumerate(zip(grads, theirs))]
        handle = _reduce_start(tag, [p for p, _ in pairs], [l for _, l in pairs])
        reductions.append((tag, slots, handle))
        return handle["token"]

    zero_vec = jnp.zeros((d,), F32)
    dmod = [[zero_vec] * 6 for _ in range(depth)]
    small = {name: [None] * depth for name in ("pre_mix_g", "post_mix_g", "pre_ffn_g", "post_ffn_g")}
    small_a = {name: [None] * n_a for name in ("a_b1", "a_dwb", "a_ln_g", "a_ln_b", "a_b2", "a_dw")}
    small_b = {"b_conv": [None] * n_b}

    dy, sums = _residual_bwd("bwd_top", g, post=(saved[-1]["y_ffn"], vec(post_ffn_g, depth - 1), mod_me[depth - 1, 5]))
    deps = []
    for i in reversed(range(depth)):
        j = i // 2
        kp = saved[i]
        sh_m, sc_m, gt_m, sh_f, sc_f, gt_f = [mod_me[i, q] for q in range(6)]
        dmod[i][5] = sums[3]
        small["post_ffn_g"][i] = sums[4]
        dhid = _mm_rows_t("f_w2_t", dy, kp["fw2g"], ACT_DTYPE, _store_relu2_grad, extra=kp["hid"], deps=deps)
        g_fw2 = _grad_rows("f_w2_grad", kp["act"], dy, f_w2.shape[1])
        dh = _mm_cols_t("f_w1_t", dhid, kp["fw1g"], F32)
        g_fw1 = _grad_cols("f_w1_grad", kp["h_ffn"], dhid, f_w1.shape[2])
        ffn_swap = _sibling_start("r%df" % i, [g_fw1, g_fw2])
        deps = [ffn_swap["token"]]
        g, dy, sums = _residual_bwd("bwd_mid", g, pre=(dh, kp["x_ffn"], vec(pre_ffn_g, i), sc_f),
                                    post=(kp["y_mix"], vec(post_mix_g, i), gt_m))
        dmod[i][3], dmod[i][4], dmod[i][2] = sums[0], sums[1], sums[3]
        small["pre_ffn_g"][i], small["post_mix_g"][i] = sums[2], sums[4]
        if i % 2 == 0:
            small_a["a_b2"][j] = sums[5]
            dv = _mm_rows_t("a_w2_t", dy, kp["w2g"], F32, deps=deps)
            g_w2 = _grad_rows("a_w2_grad", kp["v"], dy, a_w2.shape[1])
            deps = [reduce_group("r%df" % i, (("f_w1", i), ("f_w2", i)), ffn_swap, g_w2)]
            dcv, lsum = _ln_swish_bwd("a_ln_bwd", dv, kp["cv"], vec(a_ln_g, j), vec(a_ln_b, j), deps=deps)
            small_a["a_ln_g"][j], small_a["a_ln_b"][j], small_a["a_dwb"][j] = lsum[0], lsum[1], lsum[2]
            du, ddw, usum = _conv_glu_bwd("a_conv_bwd", dcv, kp["u"], a_dw_full[j])
            small_a["a_dw"][j], small_a["a_b1"][j] = ddw[:taps_a], usum[0]
            dh = _mm_cols_t("a_w1_t", du, kp["w1g"], F32)
            g_w1 = _grad_cols("a_w1_grad", kp["h_mix"], du, a_w1.shape[2])
            names = ("a_w1", "a_w2")
        else:
            dq = _mm_rows_t("b_w_out_t", dy, kp["w2g"], F32, deps=deps)
            g_w2 = _grad_rows("b_w_out_grad", kp["q"], dy, b_w_out.shape[1])
            deps = [reduce_group("r%df" % i, (("f_w1", i), ("f_w2", i)), ffn_swap, g_w2)]
            dz, wsum = _short_conv_bwd("b_conv_bwd", dq, kp["z"], b_conv_full[j], deps=deps)
            small_b["b_conv"][j] = wsum[:taps_b]
            dh = _mm_cols_t("b_w_in_t", dz, kp["w1g"], F32)
            g_w1 = _grad_cols("b_w_in_grad", kp["h_mix"], dz, b_w_in.shape[2])
            names = ("b_w_in", "b_w_out")
        mix_swap = _sibling_start("r%dm" % i, [g_w1, g_w2])
        deps = [mix_swap["token"]]
        if i > 0:
            prev = saved[i - 1]
            g, dy, sums = _residual_bwd("bwd_next", g, pre=(dh, kp["x_mix"], vec(pre_mix_g, i), sc_m),
                                        post=(prev["y_ffn"], vec(post_ffn_g, i - 1), mod_me[i - 1, 5]))
        else:
            g, sums = _residual_bwd("bwd_in", g, pre=(dh, kp["x_mix"], vec(pre_mix_g, i), sc_m))
        dmod[i][0], dmod[i][1] = sums[0], sums[1]
        small["pre_mix_g"][i] = sums[2]
        if i > 0:
            deps.append(reduce_group("r%dm" % i, ((names[0], j), (names[1], j)), mix_swap, g))
        else:
            last_swap = (((names[0], j), (names[1], j)), mix_swap)
    grad_x = g.reshape(x.shape)

    reps = [jnp.stack([jnp.stack(r) for r in dmod]),
            jnp.stack(small["pre_mix_g"]), jnp.stack(small["post_mix_g"]),
            jnp.stack(small["pre_ffn_g"]), jnp.stack(small["post_ffn_g"]),
            jnp.stack(small_a["a_b1"]), jnp.stack(small_a["a_dwb"]), jnp.stack(small_a["a_ln_g"]),
            jnp.stack(small_a["a_ln_b"]), jnp.stack(small_a["a_b2"])]
    rep_w = [mod_b, pre_mix_g, post_mix_g, pre_ffn_g, post_ffn_g, a_b1, a_dwb, a_ln_g, a_ln_b, a_b2]
    rep_m = [m_mod_b, m_pre_mix_g, m_post_mix_g, m_pre_ffn_g, m_post_ffn_g, m_a_b1, m_a_dwb, m_a_ln_g, m_a_ln_b, m_a_b2]
    rep_v = [v_mod_b, v_pre_mix_g, v_post_mix_g, v_pre_ffn_g, v_post_ffn_g, v_a_b1, v_a_dwb, v_a_ln_g, v_a_ln_b, v_a_b2]
    w_small, spans = _pack(rep_w, d)
    m_small, _ = _pack(rep_m, d)
    v_small, _ = _pack(rep_v, d)
    loss_row = jnp.pad(loss_blk[0:1, 0:1], ((0, 0), (0, d - 1)))
    part_small, spans_g = _pack(reps + [jnp.stack(small_a["a_dw"]), jnp.stack(small_b["b_conv"]), loss_row], d)
    got2 = _small_gather("gather_small", part_small, deps=[last_swap[1]["token"]])
    last_token = reduce_group("r0m", last_swap[0], last_swap[1], got2)
    total, d_small, m2_small, v2_small = _vector_update("vector_update", got2, w_small, m_small, v_small)

    def unpack(buf, span, like):
        return buf[span[0]:span[0] + span[1]].reshape(like.shape)

    rep_out = [[unpack(buf, sp, w) for sp, w in zip(spans, rep_w)] for buf in (total, d_small, m2_small, v2_small)]
    loss = total[spans_g[-1][0], 0]

    conv_g = [lax.dynamic_slice_in_dim(total[sp[0]:sp[0] + sp[1]], me * dsh, dsh, axis=1)
              for sp in spans_g[len(reps):len(reps) + 2]]
    gc, _ = _pack(conv_g, dsh)
    wc, spans_c = _pack([a_dw, b_conv], dsh)
    mc, _ = _pack([m_a_dw, m_b_conv], dsh)
    vc, _ = _pack([v_a_dw, v_b_conv], dsh)
    conv_out = [gc] + list(_plain_update("conv_update", gc, wc, mc, vc))
    conv_out = [[unpack(buf, sp, w) for sp, w in zip(spans_c, (a_dw, b_conv))] for buf in conv_out]

    at, n = spans_g[0]
    dmod_all = got2[:, at:at + n, :].reshape(N_DEV, depth, 6 * d)
    dmod_cols = jnp.transpose(lax.dynamic_slice_in_dim(dmod_all, me * n_mod, n_mod, axis=2), (1, 0, 2))
    mod_out = _modulation_update("modulation_update", jnp.transpose(c_all), dmod_cols, mod_w, m_mod_w, v_mod_w,
                                 deps=[last_token])

    big = {"f_w1": (f_w1, m_f_w1, v_f_w1), "f_w2": (f_w2, m_f_w2, v_f_w2),
           "b_w_in": (b_w_in, m_b_w_in, v_b_w_in), "b_w_out": (b_w_out, m_b_w_out, v_b_w_out),
           "a_w1": (a_w1, m_a_w1, v_a_w1), "a_w2": (a_w2, m_a_w2, v_a_w2)}
    big_out = {}
    after = mod_out[3]

    def update_complete():
        nonlocal after
        for k, wmv in big.items():
            if k not in big_out and all(b is not None for b in land[k]):
                big_out[k] = _weight_update("update_" + k, land[k], *wmv)
                after = big_out[k][3]

    for tag, slots, handle in reductions:
        if tag == reductions[-1][0]:
            update_complete()
        landed = _reduce_wait(tag, handle, after)
        after = landed[0]
        for (key, idx), buf in zip(slots, landed):
            land[key][idx] = buf
    update_complete()

    def family(q):
        rep = dict(zip(("mod_b", "pre_mix_g", "post_mix_g", "pre_ffn_g", "post_ffn_g",
                        "a_b1", "a_dwb", "a_ln_g", "a_ln_b", "a_b2"), rep_out[q]))
        return (mod_out[q], rep["mod_b"], rep["pre_mix_g"], rep["post_mix_g"], rep["pre_ffn_g"], rep["post_ffn_g"],
                big_out["a_w1"][q], rep["a_b1"], conv_out[q][0], rep["a_dwb"], rep["a_ln_g"], rep["a_ln_b"],
                big_out["a_w2"][q], rep["a_b2"], big_out["b_w_in"][q], conv_out[q][1], big_out["b_w_out"][q],
                big_out["f_w1"][q], big_out["f_w2"][q])

    return (loss, grad_x, *family(0), *family(1), *family(2), *family(3))
```

```python
import jax
import jax.numpy as jnp
from jax import lax
from jax.experimental import pallas as pl
from jax.experimental.pallas import tpu as pltpu

MXU_DTYPE = jnp.bfloat16
ACT_DTYPE = jnp.bfloat16
COMM_DTYPE = jnp.bfloat16

N_DEV = 8
N_CHIP = 4
RMS_EPS = 1e-6
LN_EPS = 1e-5
ADAM_LR = 0.001
ADAM_B1 = 0.9
ADAM_B2 = 0.999
ADAM_EPS = 1e-08
ADAM_WD = 0.01
ADAM_STEP = 10

V7X_VMEM_LIMIT_BYTES = 56 * 1024 * 1024
LANE = 128
ROW_TILE = 256
CONV_TILE = 128
CONV_HALO = 32
SHORT_HALO = 8
CONV_CHUNK = 256
MM_TM, MM_TN, MM_TK = 1024, 1024, 2048
MM_RESIDENT = 4 * 1024 * 1024

F32 = jnp.float32
MESH = pl.DeviceIdType.MESH


def _pick(dim, pref, mult=LANE):
    if dim <= pref:
        return dim
    t = pref - pref % mult
    while dim % t:
        t -= mult
    return t


def _params(*sem):
    return pltpu.CompilerParams(dimension_semantics=sem, vmem_limit_bytes=V7X_VMEM_LIMIT_BYTES)


def _pos():
    return lax.axis_index("x"), lax.axis_index("y"), lax.axis_index("c")


def _flip(v, bit):
    return 1 - v if bit else v


def _small_gather(name, v, deps=()):
    rows, cols = v.shape
    n_deps = len(deps)

    def body(v_ref, *rest):
        out_ref, send_sems, recv_sems, local_sem = rest[n_deps:]
        x, y, c = _pos()
        me, sibling = (x, y, c), (x, y, 1 - c)
        chips = [(1 - x, y), (x, 1 - y), (1 - x, 1 - y)]

        def block(px, py, pc):
            return out_ref.at[4 * px + 2 * py + pc]

        def copy(k, owner, to, src=None):
            return pltpu.make_async_remote_copy(
                src_ref=block(*owner) if src is None else src, dst_ref=block(*owner), send_sem=send_sems.at[k],
                recv_sem=recv_sems.at[k], device_id=to, device_id_type=MESH)

        mine = pltpu.make_async_copy(v_ref, block(*me), local_sem)
        mine.start()
        first = [copy(0, me, sibling, src=v_ref)]
        first += [copy(1 + j, me, (*chip, c), src=v_ref) for j, chip in enumerate(chips)]
        for cp in first:
            cp.start()
        passed = [copy(4 + j, (*chip, c), sibling) for j, chip in enumerate(chips)]
        for j, chip in enumerate(chips):
            copy(1 + j, (*chip, c), me).wait_recv()
            passed[j].start()
        copy(0, sibling, me).wait_recv()
        for j, chip in enumerate(chips):
            copy(4 + j, (*chip, 1 - c), me).wait_recv()
        for cp in first + passed:
            cp.wait_send()
        mine.wait()

    return pl.pallas_call(
        body, name=name,
        out_shape=jax.ShapeDtypeStruct((N_DEV, rows, cols), v.dtype),
        in_specs=[pl.BlockSpec(memory_space=pltpu.VMEM)] + [pl.BlockSpec(memory_space=pl.ANY)] * n_deps,
        out_specs=pl.BlockSpec(memory_space=pltpu.VMEM),
        scratch_shapes=[pltpu.SemaphoreType.DMA((N_DEV - 1,)), pltpu.SemaphoreType.DMA((N_DEV - 1,)),
                        pltpu.SemaphoreType.DMA],
        compiler_params=pltpu.CompilerParams(vmem_limit_bytes=V7X_VMEM_LIMIT_BYTES),
    )(v, *deps)


HBM_SPEC = pl.BlockSpec(memory_space=pltpu.HBM)
SEM_SPEC = pl.BlockSpec(memory_space=pltpu.SEMAPHORE)
ANY_SPEC = pl.BlockSpec(memory_space=pl.ANY)
TOKEN = jax.ShapeDtypeStruct((8, LANE), F32)
TOKEN_SPEC = pl.BlockSpec(memory_space=pltpu.VMEM)
SPLIT_COPY = pltpu.CompilerParams(has_side_effects=pltpu.SideEffectType.DATAFLOW_SIDE_EFFECTING)


def _hbm(a):
    return pltpu.with_memory_space_constraint(a, pltpu.HBM)


def _dma_sems(n):
    return pltpu.SemaphoreType.DMA((n,))


def _place(name, w, layer, me, by_columns):
    _, rows, cols = w.shape
    tr = _pick(rows, max(8, (1 << 20) // cols), 8)

    def body(me_ref, w_ref, o_ref):
        o_ref[...] = w_ref[...].astype(o_ref.dtype)

    if by_columns:
        out_spec = pl.BlockSpec((tr, cols), lambda r, me_ref: (r, me_ref[0]))
        out_shape = jax.ShapeDtypeStruct((rows, N_DEV * cols), COMM_DTYPE)
    else:
        out_spec = pl.BlockSpec((None, tr, cols), lambda r, me_ref: (me_ref[0], r, 0))
        out_shape = jax.ShapeDtypeStruct((N_DEV, rows, cols), COMM_DTYPE)
    return pl.pallas_call(
        body, name=name,
        grid_spec=pltpu.PrefetchScalarGridSpec(
            num_scalar_prefetch=1, grid=(rows // tr,),
            in_specs=[pl.BlockSpec((None, tr, cols), lambda r, me_ref: (layer, r, 0))],
            out_specs=out_spec),
        out_shape=out_shape, compiler_params=_params("parallel"),
    )(me, w)


def _block_copy(buf, owner, to, send_sem, recv_sem):
    dev = 4 * owner[0] + 2 * owner[1] + owner[2]
    if len(buf.shape) == 3:
        blk = buf.at[dev]
    else:
        n = buf.shape[1] // N_DEV
        blk = buf.at[:, pl.ds(pl.multiple_of(dev * n, n), n)]
    return pltpu.make_async_remote_copy(src_ref=blk, dst_ref=blk, send_sem=send_sem, recv_sem=recv_sem,
                                        device_id=to, device_id_type=MESH)


def _other_chips(x, y):
    return [(1 - x, y), (x, 1 - y), (1 - x, 1 - y)]


def _gather_start(tag, bufs, after=()):
    n = len(bufs)
    n_in = n + len(after)

    def body(*refs):
        b = refs[:n]
        send, recv_ici, recv_sib = refs[n_in:n_in + 3]
        token = refs[-1]
        x, y, c = _pos()
        me = (x, y, c)
        for i in range(n):
            _block_copy(b[i], me, (x, y, 1 - c), send.at[4 * i], recv_sib.at[i]).start()
            for j, chip in enumerate(_other_chips(x, y)):
                _block_copy(b[i], me, (*chip, c), send.at[4 * i + 1 + j], recv_ici.at[3 * i + j]).start()
        token[...] = jnp.zeros_like(token)

    outs = pl.pallas_call(
        body, name="gather_start_" + tag,
        out_shape=(_dma_sems(4 * n), _dma_sems(3 * n), _dma_sems(n),
                   *[pltpu.HBM(b.shape, b.dtype) for b in bufs], TOKEN),
        in_specs=[HBM_SPEC] * n + [ANY_SPEC] * len(after),
        out_specs=(SEM_SPEC, SEM_SPEC, SEM_SPEC, *[HBM_SPEC] * n, TOKEN_SPEC),
        input_output_aliases={i: 3 + i for i in range(n)}, compiler_params=SPLIT_COPY,
    )(*[_hbm(b) for b in bufs], *after)
    return dict(send=outs[0], recv_ici=outs[1], recv_sib=outs[2], bufs=list(outs[3:3 + n]), token=outs[-1])


def _gather_pass(tag, h, after):
    bufs = h["bufs"]
    n = len(bufs)

    def body(*refs):
        b = refs[:n]
        recv_ici = refs[n]
        fsend, frecv = refs[n + 2:n + 4]
        token = refs[-1]
        x, y, c = _pos()
        for i in range(n):
            for j, chip in enumerate(_other_chips(x, y)):
                _block_copy(b[i], (*chip, c), (x, y, c), fsend.at[3 * i + j], recv_ici.at[3 * i + j]).wait_recv()
                _block_copy(b[i], (*chip, c), (x, y, 1 - c), fsend.at[3 * i + j], frecv.at[3 * i + j]).start()
        token[...] = jnp.zeros_like(token)

    outs = pl.pallas_call(
        body, name="gather_pass_" + tag,
        out_shape=(_dma_sems(3 * n), _dma_sems(3 * n), *[pltpu.HBM(b.shape, b.dtype) for b in bufs], TOKEN),
        in_specs=[HBM_SPEC] * n + [SEM_SPEC, ANY_SPEC],
        out_specs=(SEM_SPEC, SEM_SPEC, *[HBM_SPEC] * n, TOKEN_SPEC),
        input_output_aliases={i: 2 + i for i in range(n)}, compiler_params=SPLIT_COPY,
    )(*bufs, h["recv_ici"], after)
    return dict(h, fsend=outs[0], frecv=outs[1], bufs=list(outs[2:2 + n]), token=outs[-1])


def _gather_wait(tag, h, after):
    bufs = h["bufs"]
    n = len(bufs)

    def body(*refs):
        b = refs[:n]
        send, recv_sib, fsend, frecv = refs[n:n + 4]
        x, y, c = _pos()
        me, sibling = (x, y, c), (x, y, 1 - c)
        for i in range(n):
            _block_copy(b[i], sibling, me, send.at[4 * i], recv_sib.at[i]).wait_recv()
            for j, chip in enumerate(_other_chips(x, y)):
                _block_copy(b[i], (*chip, 1 - c), me, fsend.at[3 * i + j], frecv.at[3 * i + j]).wait_recv()
            for k in range(4):
                _block_copy(b[i], me, sibling, send.at[4 * i + k], recv_sib.at[i]).wait_send()
            for j, chip in enumerate(_other_chips(x, y)):
                _block_copy(b[i], (*chip, c), sibling, fsend.at[3 * i + j], frecv.at[3 * i + j]).wait_send()

    outs = pl.pallas_call(
        body, name="gather_wait_" + tag,
        out_shape=tuple(pltpu.HBM(b.shape, b.dtype) for b in bufs),
        in_specs=[HBM_SPEC] * n + [SEM_SPEC] * 4 + [ANY_SPEC], out_specs=tuple([HBM_SPEC] * n),
        input_output_aliases={i: i for i in range(n)}, compiler_params=SPLIT_COPY,
    )(*bufs, h["send"], h["recv_sib"], h["fsend"], h["frecv"], after)
    return list(outs)


def _sibling_copy(grad, land, send_sem, recv_sem):
    x, y, c = _pos()
    return pltpu.make_async_remote_copy(
        src_ref=grad.at[:, 1 - c], dst_ref=land, send_sem=send_sem, recv_sem=recv_sem,
        device_id=(x, y, 1 - c), device_id_type=MESH)


def _sibling_start(tag, grads):
    n = len(grads)
    landings = [lax.empty((N_CHIP,) + g.shape[2:], g.dtype) for g in grads]

    def body(*refs):
        g, land = refs[:n], refs[n:2 * n]
        send, recv = refs[2 * n:2 * n + 2]
        token = refs[-1]
        for i in range(n):
            _sibling_copy(g[i], land[i], send.at[i], recv.at[i]).start()
        token[...] = jnp.zeros_like(token)

    both = list(grads) + landings
    outs = pl.pallas_call(
        body, name="sibling_start_" + tag,
        out_shape=(_dma_sems(n), _dma_sems(n), *[pltpu.HBM(b.shape, b.dtype) for b in both], TOKEN),
        in_specs=[HBM_SPEC] * (2 * n), out_specs=(SEM_SPEC, SEM_SPEC, *[HBM_SPEC] * (2 * n), TOKEN_SPEC),
        input_output_aliases={i: 2 + i for i in range(2 * n)}, compiler_params=SPLIT_COPY,
    )(*[_hbm(b) for b in both])
    return dict(send=outs[0], recv=outs[1], grads=list(outs[2:2 + n]), landings=list(outs[2 + n:2 + 2 * n]),
                token=outs[-1])


def _sibling_wait(tag, h, after):
    n = len(h["grads"])

    def body(*refs):
        g, land = refs[:n], refs[n:2 * n]
        send, recv = refs[2 * n:2 * n + 2]
        for i in range(n):
            _sibling_copy(g[i], land[i], send.at[i], recv.at[i]).wait()

    both = h["grads"] + h["landings"]
    outs = pl.pallas_call(
        body, name="sibling_wait_" + tag,
        out_shape=tuple(pltpu.HBM(b.shape, b.dtype) for b in both),
        in_specs=[HBM_SPEC] * (2 * n) + [SEM_SPEC, SEM_SPEC, ANY_SPEC], out_specs=tuple([HBM_SPEC] * (2 * n)),
        input_output_aliases={i: i for i in range(2 * n)}, compiler_params=SPLIT_COPY,
    )(*both, h["send"], h["recv"], after)
    return list(outs[:n]), list(outs[n:])


def _reduce_copy(part, land, x, y, c, k, src_chip, send_sem, recv_sem):
    px, py = _flip(x, k >> 1 & 1), _flip(y, k & 1)
    return pltpu.make_async_remote_copy(src_ref=part.at[2 * px + py], dst_ref=land.at[src_chip], send_sem=send_sem,
                                        recv_sem=recv_sem, device_id=(px, py, c), device_id_type=MESH)


def _reduce_start(tag, partials, landings):
    n = len(partials)

    def body(*refs):
        p, land = refs[:n], refs[n:2 * n]
        send, recv = refs[2 * n:2 * n + 2]
        token = refs[-1]
        x, y, c = _pos()
        for i in range(n):
            for k in range(1, N_CHIP):
                _reduce_copy(p[i], land[i], x, y, c, k, 2 * x + y, send.at[3 * i + k - 1], recv.at[3 * i + k - 1]).start()
        token[...] = jnp.zeros_like(token)

    both = list(partials) + list(landings)
    outs = pl.pallas_call(
        body, name="reduce_start_" + tag,
        out_shape=(_dma_sems(3 * n), _dma_sems(3 * n), *[pltpu.HBM(b.shape, b.dtype) for b in both], TOKEN),
        in_specs=[HBM_SPEC] * (2 * n), out_specs=(SEM_SPEC, SEM_SPEC, *[HBM_SPEC] * (2 * n), TOKEN_SPEC),
        input_output_aliases={i: 2 + i for i in range(2 * n)}, compiler_params=SPLIT_COPY,
    )(*[_hbm(b) for b in both])
    return dict(send=outs[0], recv=outs[1], partials=list(outs[2:2 + n]), landings=list(outs[2 + n:2 + 2 * n]),
                token=outs[-1])


def _reduce_wait(tag, h, after):
    n = len(h["partials"])

    def body(*refs):
        p, land = refs[:n], refs[n:2 * n]
        send, recv = refs[2 * n:2 * n + 2]
        x, y, c = _pos()
        for i in range(n):
            for k in range(1, N_CHIP):
                src_chip = 2 * _flip(x, k >> 1 & 1) + _flip(y, k & 1)
                cp = _reduce_copy(p[i], land[i], x, y, c, k, src_chip, send.at[3 * i + k - 1], recv.at[3 * i + k - 1])
                cp.wait_recv()
                cp.wait_send()

    both = h["partials"] + h["landings"]
    outs = pl.pallas_call(
        body, name="reduce_wait_" + tag,
        out_shape=tuple(pltpu.HBM(b.shape, b.dtype) for b in both),
        in_specs=[HBM_SPEC] * (2 * n) + [SEM_SPEC, SEM_SPEC, ANY_SPEC], out_specs=tuple([HBM_SPEC] * (2 * n)),
        input_output_aliases={i: i for i in range(2 * n)}, compiler_params=SPLIT_COPY,
    )(*both, h["send"], h["recv"], after)
    return list(outs[n:])


def _chip_partial(name, grad, theirs, where):
    _, _, rows, cols = grad.shape
    tr = _pick(rows, max(8, (1 << 20) // cols), 8)
    landing = lax.empty(theirs.shape, theirs.dtype)

    def body(where_ref, g_ref, t_ref, land_in, o_ref, land_ref):
        total = (g_ref[...].astype(F32) + t_ref[...].astype(F32)).astype(o_ref.dtype)
        o_ref[...] = total

        @pl.when(pl.program_id(1) == where_ref[1])
        def _():
            land_ref[...] = total

    return pl.pallas_call(
        body, name=name,
        grid_spec=pltpu.PrefetchScalarGridSpec(
            num_scalar_prefetch=1, grid=(rows // tr, N_CHIP),
            in_specs=[pl.BlockSpec((None, None, tr, cols), lambda r, k, w: (k, w[0], r, 0)),
                      pl.BlockSpec((None, tr, cols), lambda r, k, w: (k, r, 0)), ANY_SPEC],
            out_specs=[pl.BlockSpec((None, tr, cols), lambda r, k, w: (k, r, 0)),
                       pl.BlockSpec((None, tr, cols), lambda r, k, w: (w[1], r, 0))]),
        out_shape=[jax.ShapeDtypeStruct(theirs.shape, theirs.dtype)] * 2,
        input_output_aliases={3: 1},
        compiler_params=_params("parallel", "arbitrary"),
    )(where, grad, theirs, landing)


def _matmul(name, a, b, a_spec, b_spec, grid, acc_shape, out_shape, out_specs, epilogue,
            ta=False, tb=False, extras=(), extra_specs=(), deps=()):
    nk = grid[2]
    n_extra = len(extras)
    n_in = 2 + n_extra + len(deps)
    dims = (((0,) if ta else (1,), (1,) if tb else (0,)), ((), ()))

    def body(*refs):
        a_ref, b_ref = refs[:2]
        extra_refs = refs[2:2 + n_extra]
        out_refs = refs[n_in:n_in + len(out_shape)]

        def product():
            return lax.dot_general(a_ref[...].astype(MXU_DTYPE), b_ref[...].astype(MXU_DTYPE), dims,
                                   preferred_element_type=F32)

        if nk == 1:
            epilogue(product(), extra_refs, out_refs)
            return
        acc_ref = refs[-1]
        k = pl.program_id(2)

        @pl.when(k == 0)
        def _():
            acc_ref[...] = product()

        if nk > 2:
            @pl.when((k > 0) & (k < nk - 1))
            def _():
                acc_ref[...] += product()

        @pl.when(k == nk - 1)
        def _():
            epilogue(acc_ref[...] + product(), extra_refs, out_refs)

    return pl.pallas_call(
        body, name=name, grid=grid,
        in_specs=[a_spec, b_spec, *extra_specs, *[ANY_SPEC] * len(deps)], out_specs=out_specs, out_shape=out_shape,
        scratch_shapes=[pltpu.VMEM(acc_shape, F32)] if nk > 1 else [],
        compiler_params=_params("parallel", "parallel", "arbitrary"),
    )(a, b, *extras, *deps)


def _store(acc, extra_refs, out_refs):
    out_refs[0][...] = acc.astype(out_refs[0].dtype)


def _store_bias(acc, extra_refs, out_refs):
    out_refs[0][...] = (acc + extra_refs[0][...]).astype(out_refs[0].dtype)


def _store_relu2(acc, extra_refs, out_refs):
    r = jnp.maximum(acc, 0.0)
    out_refs[0][...] = (r * r).astype(out_refs[0].dtype)
    out_refs[1][...] = acc.astype(out_refs[1].dtype)


def _store_relu2_grad(acc, extra_refs, out_refs):
    hid = extra_refs[0][...].astype(F32)
    out_refs[0][...] = (acc * (2.0 * jnp.maximum(hid, 0.0))).astype(out_refs[0].dtype)


def _mm_cols(name, a, wc, out_dtype, epilogue=_store, bias=None, n_out=1, deps=()):
    s, kdim = a.shape
    n = wc.shape[1]
    tm, tn, tk = _pick(s, MM_TM), _pick(n, MM_TN), _pick(kdim, MM_TK)
    extras, extra_specs = (), ()
    if bias is not None:
        extras, extra_specs = (bias,), (pl.BlockSpec((1, tn), lambda i, j, k: (0, j)),)
    out = jax.ShapeDtypeStruct((s, n), out_dtype)
    spec = pl.BlockSpec((tm, tn), lambda i, j, k: (i, j))
    return _matmul(
        name, a, wc,
        pl.BlockSpec((tm, tk), lambda i, j, k: (i, k)),
        pl.BlockSpec((tk, tn), lambda i, j, k: (k, j)),
        (s // tm, n // tn, kdim // tk), (tm, tn),
        [out] * n_out, [spec] * n_out, epilogue, extras=extras, extra_specs=extra_specs, deps=deps)


def _mm_rows(name, a, wr, out_dtype, bias=None, deps=()):
    s, kdim = a.shape
    w = wr.reshape(kdim, wr.shape[2])
    n = w.shape[1]
    tm, tn, tk = _pick(s, MM_TM), _pick(n, MM_TN), _pick(kdim, MM_TK)
    if kdim * n <= MM_RESIDENT:
        tm, tn = _pick(s, MM_TM // 2), n
    extras, extra_specs, epilogue = (), (), _store
    if bias is not None:
        extras, extra_specs, epilogue = (bias,), (pl.BlockSpec((1, tn), lambda i, j, k: (0, j)),), _store_bias
    return _matmul(
        name, a, w,
        pl.BlockSpec((tm, tk), lambda i, j, k: (i, k)),
        pl.BlockSpec((tk, tn), lambda i, j, k: (k, j)),
        (s // tm, n // tn, kdim // tk), (tm, tn),
        [jax.ShapeDtypeStruct((s, n), out_dtype)], [pl.BlockSpec((tm, tn), lambda i, j, k: (i, j))],
        epilogue, extras=extras, extra_specs=extra_specs, deps=deps)[0]


def _mm_cols_t(name, dy, wc, out_dtype, deps=()):
    s, n = dy.shape
    kdim = wc.shape[0]
    tm, tn, tk = _pick(s, MM_TM), _pick(kdim, MM_TN), _pick(n, MM_TK)
    return _matmul(
        name, dy, wc,
        pl.BlockSpec((tm, tk), lambda i, j, k: (i, k)),
        pl.BlockSpec((tn, tk), lambda i, j, k: (j, k)),
        (s // tm, kdim // tn, n // tk), (tm, tn),
        [jax.ShapeDtypeStruct((s, kdim), out_dtype)], [pl.BlockSpec((tm, tn), lambda i, j, k: (i, j))],
        _store, tb=True, deps=deps)[0]


def _mm_rows_t(name, dy, wr, out_dtype, epilogue=_store, extra=None, deps=()):
    s, n = dy.shape
    w = wr.reshape(-1, n)
    kdim = w.shape[0]
    tm, tn, tk = _pick(s, MM_TM), _pick(kdim, MM_TN), _pick(n, MM_TK)
    if kdim * n <= MM_RESIDENT:
        tm, tn = _pick(s, MM_TM // 2), kdim
    extras, extra_specs = (), ()
    if extra is not None:
        extras, extra_specs = (extra,), (pl.BlockSpec((tm, tn), lambda i, j, k: (i, j)),)
    return _matmul(
        name, dy, w,
        pl.BlockSpec((tm, tk), lambda i, j, k: (i, k)),
        pl.BlockSpec((tn, tk), lambda i, j, k: (j, k)),
        (s // tm, kdim // tn, n // tk), (tm, tn),
        [jax.ShapeDtypeStruct((s, kdim), out_dtype)], [pl.BlockSpec((tm, tn), lambda i, j, k: (i, j))],
        epilogue, tb=True, extras=extras, extra_specs=extra_specs, deps=deps)[0]


def _grad_cols(name, h, dy, n):
    s, kdim = h.shape
    tm, tn, tk = _pick(kdim, MM_TM), _pick(n, MM_TN), _pick(s, MM_TK)
    per = n // tn
    return _matmul(
        name, h, dy,
        pl.BlockSpec((tk, tm), lambda i, j, k: (k, i)),
        pl.BlockSpec((tk, tn), lambda i, j, k: (k, j)),
        (kdim // tm, N_DEV * per, s // tk), (tm, tn),
        [jax.ShapeDtypeStruct((N_DEV, kdim, n), COMM_DTYPE)],
        [pl.BlockSpec((None, tm, tn), lambda i, j, k: (j // per, i, j % per))],
        _store, ta=True)[0].reshape(N_CHIP, 2, kdim, n)


def _grad_rows(name, v, dy, kk):
    s, kdim = v.shape
    n = dy.shape[1]
    tm, tn, tk = _pick(kdim, MM_TM), _pick(n, MM_TN), _pick(s, MM_TK)
    return _matmul(
        name, v, dy,
        pl.BlockSpec((tk, tm), lambda i, j, k: (k, i)),
        pl.BlockSpec((tk, tn), lambda i, j, k: (k, j)),
        (kdim // tm, n // tn, s // tk), (tm, tn),
        [jax.ShapeDtypeStruct((kdim, n), COMM_DTYPE)],
        [pl.BlockSpec((tm, tn), lambda i, j, k: (i, j))],
        _store, ta=True)[0].reshape(N_CHIP, 2, kk, n)


def _rms(v):
    return lax.rsqrt(jnp.mean(v * v, axis=-1, keepdims=True) + RMS_EPS)


def _colsum(v):
    return jnp.sum(v, axis=0, keepdims=True)


def _vec_spec(width):
    return pl.BlockSpec((1, width), lambda i: (0, 0))


def _residual_fwd(name, x, post=None, pre=None, deps=()):
    s, d = x.shape
    ts = _pick(s, ROW_TILE, 8)
    row = pl.BlockSpec((ts, d), lambda i: (i, 0))
    ins, specs = [x], [row]
    if post is not None:
        ins += list(post)
        specs += [row, _vec_spec(d), _vec_spec(d)]
    if pre is not None:
        ins += list(pre)
        specs += [_vec_spec(d)] * 3
    ins += list(deps)
    specs += [ANY_SPEC] * len(deps)
    outs, out_specs = [], []
    if post is not None:
        outs.append(jax.ShapeDtypeStruct((s, d), F32))
        out_specs.append(row)
    if pre is not None:
        outs.append(jax.ShapeDtypeStruct((s, d), ACT_DTYPE))
        out_specs.append(row)

    def body(*refs):
        refs = list(refs)
        xv = refs.pop(0)[...]
        if post is not None:
            y_ref, gp_ref, gt_ref = refs[:3]
            del refs[:3]
        if pre is not None:
            g_ref, sc_ref, sh_ref = refs[:3]
            del refs[:3]
        del refs[:len(deps)]
        if post is not None:
            yv = y_ref[...]
            xv = xv + gt_ref[...] * ((yv * _rms(yv)) * gp_ref[...])
            refs.pop(0)[...] = xv
        if pre is not None:
            hv = ((xv * _rms(xv)) * g_ref[...]) * (1.0 + sc_ref[...]) + sh_ref[...]
            refs.pop(0)[...] = hv.astype(ACT_DTYPE)

    return pl.pallas_call(
        body, name=name, grid=(s // ts,), in_specs=specs, out_specs=out_specs, out_shape=outs,
        compiler_params=_params("parallel"),
    )(*ins)


def _loss_junction(name, x, y, g_post, gt, target):
    s, d = x.shape
    ts = _pick(s, ROW_TILE, 8)
    row = pl.BlockSpec((ts, d), lambda i: (i, 0))

    def body(x_ref, y_ref, gpost_ref, gt_ref, t_ref, g_ref, dy_ref, loss_ref, sums_ref):
        @pl.when(pl.program_id(0) == 0)
        def _():
            loss_ref[...] = jnp.zeros_like(loss_ref)
            sums_ref[...] = jnp.zeros_like(sums_ref)

        yv = y_ref[...]
        r = _rms(yv)
        nrm = yv * r
        rn = nrm * gpost_ref[...]
        err = (x_ref[...] + gt_ref[...] * rn) - t_ref[...]
        loss_ref[...] += (0.5 / d) * jnp.sum(err * err)
        g = err * (1.0 / d)
        g_ref[...] = g
        sums_ref[3:4, :] += _colsum(g * rn)
        d_o = g * gt_ref[...]
        sums_ref[4:5, :] += _colsum(d_o * nrm)
        dn = d_o * gpost_ref[...]
        dy = r * (dn - nrm * jnp.mean(dn * nrm, axis=-1, keepdims=True))
        sums_ref[5:6, :] += _colsum(dy)
        dy_ref[...] = dy.astype(dy_ref.dtype)

    return pl.pallas_call(
        body, name=name, grid=(s // ts,),
        in_specs=[row, row, _vec_spec(d), _vec_spec(d), row],
        out_specs=[row, row, pl.BlockSpec((8, LANE), lambda i: (0, 0)), pl.BlockSpec((8, d), lambda i: (0, 0))],
        out_shape=[jax.ShapeDtypeStruct((s, d), F32), jax.ShapeDtypeStruct((s, d), ACT_DTYPE),
                   jax.ShapeDtypeStruct((8, LANE), F32), jax.ShapeDtypeStruct((8, d), F32)],
        compiler_params=_params("arbitrary"),
    )(x, y, g_post, gt, target)


def _residual_bwd(name, g_out, pre=None, post=None):
    s, d = g_out.shape
    ts = _pick(s, ROW_TILE, 8)
    row = pl.BlockSpec((ts, d), lambda i: (i, 0))
    ins, specs = [g_out], [row]
    outs, out_specs = [], []
    if pre is not None:
        ins += list(pre)
        specs += [row, row, _vec_spec(d), _vec_spec(d)]
        outs.append(jax.ShapeDtypeStruct((s, d), F32))
        out_specs.append(row)
    if post is not None:
        ins += list(post)
        specs += [row, _vec_spec(d), _vec_spec(d)]
        outs.append(jax.ShapeDtypeStruct((s, d), ACT_DTYPE))
        out_specs.append(row)
    outs.append(jax.ShapeDtypeStruct((8, d), F32))
    out_specs.append(pl.BlockSpec((8, d), lambda i: (0, 0)))

    def body(*refs):
        refs = list(refs)
        g = refs.pop(0)[...]
        if pre is not None:
            dh_ref, x_ref, gpre_ref, sc_ref = refs[:4]
            del refs[:4]
        if post is not None:
            y_ref, gpost_ref, gt_ref = refs[:3]
            del refs[:3]
        sums_ref = refs[-1]

        @pl.when(pl.program_id(0) == 0)
        def _():
            sums_ref[...] = jnp.zeros_like(sums_ref)

        if pre is not None:
            dh, xv = dh_ref[...], x_ref[...]
            r = _rms(xv)
            nrm = xv * r
            d_rn = dh * (1.0 + sc_ref[...])
            sums_ref[0:1, :] += _colsum(dh)
            sums_ref[1:2, :] += _colsum(dh * (nrm * gpre_ref[...]))
            sums_ref[2:3, :] += _colsum(d_rn * nrm)
            dn = d_rn * gpre_ref[...]
            g = g + r * (dn - nrm * jnp.mean(dn * nrm, axis=-1, keepdims=True))
            refs.pop(0)[...] = g
        if post is not None:
            yv = y_ref[...]
            r = _rms(yv)
            nrm = yv * r
            sums_ref[3:4, :] += _colsum(g * (nrm * gpost_ref[...]))
            d_o = g * gt_ref[...]
            sums_ref[4:5, :] += _colsum(d_o * nrm)
            dn = d_o * gpost_ref[...]
            dy = r * (dn - nrm * jnp.mean(dn * nrm, axis=-1, keepdims=True))
            sums_ref[5:6, :] += _colsum(dy)
            refs.pop(0)[...] = dy.astype(ACT_DTYPE)

    return pl.pallas_call(
        body, name=name, grid=(s // ts,), in_specs=specs, out_specs=out_specs, out_shape=outs,
        compiler_params=_params("arbitrary"),
    )(*ins)


def _chunks(d):
    cw = min(CONV_CHUNK, d)
    return [(c * cw, cw) for c in range(d // cw)]


def _shift_rows(buf, sh, c0, cw, rows):
    for r in range(1, 8):
        sh[r - 1] = buf[r:r + rows, c0:c0 + cw]


def _window(buf, sh, c0, cw, offset, ts):
    q, r = divmod(offset, 8)
    if r == 0:
        return buf[8 * q:8 * q + ts, c0:c0 + cw]
    return sh[r - 1, 8 * q:8 * q + ts, :]


def _glu_conv_ln_swish(name, u, dw, dwb, ln_g, ln_b):
    s, d2 = u.shape
    d = d2 // 2
    taps = dw.shape[0]
    ts, halo = _pick(s, CONV_TILE, 8), CONV_HALO
    lead = halo - (taps - 1)
    per = ts // halo
    cw = min(CONV_CHUNK, d)

    def body(cur_ref, prev_ref, dw_ref, dwb_ref, g_ref, b_ref, cv_ref, v_ref, buf, sh):
        i = pl.program_id(0)
        buf[0:halo, :] = jnp.where(i > 0, prev_ref[:, :d] * jax.nn.sigmoid(prev_ref[:, d:]), 0.0)
        buf[halo:, :] = cur_ref[:, :d] * jax.nn.sigmoid(cur_ref[:, d:])
        for c0, _ in _chunks(d):
            _shift_rows(buf, sh, c0, cw, ts + halo - 8)
            acc = jnp.zeros((ts, cw), F32)
            for k in range(taps):
                acc += dw_ref[k:k + 1, c0:c0 + cw] * _window(buf, sh, c0, cw, lead + k, ts)
            cv_ref[:, c0:c0 + cw] = acc + dwb_ref[:, c0:c0 + cw]
        cv = cv_ref[...]
        mu = jnp.mean(cv, axis=-1, keepdims=True)
        xc = cv - mu
        var = jnp.mean(xc * xc, axis=-1, keepdims=True)
        ln = (xc * lax.rsqrt(var + LN_EPS)) * g_ref[...] + b_ref[...]
        v_ref[...] = (ln * jax.nn.sigmoid(ln)).astype(v_ref.dtype)

    row = pl.BlockSpec((ts, d), lambda i: (i, 0))
    return pl.pallas_call(
        body, name=name, grid=(s // ts,),
        in_specs=[pl.BlockSpec((ts, d2), lambda i: (i, 0)),
                  pl.BlockSpec((halo, d2), lambda i: (jnp.maximum(i * per - 1, 0), 0)),
                  pl.BlockSpec((taps, d), lambda i: (0, 0)), _vec_spec(d), _vec_spec(d), _vec_spec(d)],
        out_specs=[row, row],
        out_shape=[jax.ShapeDtypeStruct((s, d), F32), jax.ShapeDtypeStruct((s, d), ACT_DTYPE)],
        scratch_shapes=[pltpu.VMEM((ts + halo, d), F32), pltpu.VMEM((7, ts + halo - 8, cw), F32)],
        compiler_params=_params("parallel"),
    )(u, u, dw, dwb, ln_g, ln_b)


def _ln_swish_bwd(name, dv, cv, ln_g, ln_b, deps=()):
    s, d = cv.shape
    ts = _pick(s, ROW_TILE, 8)

    def body(dv_ref, cv_ref, g_ref, b_ref, *rest):
        dcv_ref, sums_ref = rest[len(deps):]

        @pl.when(pl.program_id(0) == 0)
        def _():
            sums_ref[...] = jnp.zeros_like(sums_ref)

        cv = cv_ref[...]
        mu = jnp.mean(cv, axis=-1, keepdims=True)
        xc = cv - mu
        rstd = lax.rsqrt(jnp.mean(xc * xc, axis=-1, keepdims=True) + LN_EPS)
        nhat = xc * rstd
        ln = nhat * g_ref[...] + b_ref[...]
        sg = jax.nn.sigmoid(ln)
        dl = dv_ref[...] * (sg * (1.0 + ln * (1.0 - sg)))
        sums_ref[0:1, :] += _colsum(dl * nhat)
        sums_ref[1:2, :] += _colsum(dl)
        dn = dl * g_ref[...]
        dcv = rstd * (dn - jnp.mean(dn, axis=-1, keepdims=True)
                      - nhat * jnp.mean(dn * nhat, axis=-1, keepdims=True))
        sums_ref[2:3, :] += _colsum(dcv)
        dcv_ref[...] = dcv

    row = pl.BlockSpec((ts, d), lambda i: (i, 0))
    return pl.pallas_call(
        body, name=name, grid=(s // ts,),
        in_specs=[row, row, _vec_spec(d), _vec_spec(d)] + [ANY_SPEC] * len(deps),
        out_specs=[row, pl.BlockSpec((8, d), lambda i: (0, 0))],
        out_shape=[jax.ShapeDtypeStruct((s, d), F32), jax.ShapeDtypeStruct((8, d), F32)],
        compiler_params=_params("arbitrary"),
    )(dv, cv, ln_g, ln_b, *deps)


def _conv_glu_bwd(name, dcv, u, dw):
    s, d = dcv.shape
    taps = dw.shape[0]
    taps8 = -(-taps // 8) * 8
    ts, halo = _pick(s, CONV_TILE, 8), CONV_HALO
    lead = halo - (taps - 1)
    per = ts // halo
    n_tiles = s // ts
    cw = min(CONV_CHUNK, d)

    def body(dcv_ref, next_ref, u_ref, prev_ref, dw_ref, du_ref, ddw_ref, sums_ref, nbuf, pbuf, nsh, psh):
        i = pl.program_id(0)

        @pl.when(i == 0)
        def _():
            ddw_ref[...] = jnp.zeros_like(ddw_ref)
            sums_ref[...] = jnp.zeros_like(sums_ref)

        nbuf[0:ts, :] = dcv_ref[...]
        nbuf[ts:, :] = jnp.where(i < n_tiles - 1, next_ref[...], 0.0)
        pbuf[0:halo, :] = jnp.where(i > 0, prev_ref[:, :d] * jax.nn.sigmoid(prev_ref[:, d:]), 0.0)
        pbuf[halo:, :] = u_ref[:, :d] * jax.nn.sigmoid(u_ref[:, d:])
        for c0, _ in _chunks(d):
            _shift_rows(nbuf, nsh, c0, cw, ts + halo - 8)
            _shift_rows(pbuf, psh, c0, cw, ts + halo - 8)
            dcv = dcv_ref[:, c0:c0 + cw]
            dglu = jnp.zeros((ts, cw), F32)
            for k in range(taps):
                dglu += dw_ref[k:k + 1, c0:c0 + cw] * _window(nbuf, nsh, c0, cw, taps - 1 - k, ts)
                ddw_ref[k:k + 1, c0:c0 + cw] += _colsum(dcv * _window(pbuf, psh, c0, cw, lead + k, ts))
            a = u_ref[:, c0:c0 + cw]
            sg = jax.nn.sigmoid(u_ref[:, d + c0:d + c0 + cw])
            da = dglu * sg
            dg = dglu * a * (sg * (1.0 - sg))
            du_ref[:, c0:c0 + cw] = da.astype(du_ref.dtype)
            du_ref[:, d + c0:d + c0 + cw] = dg.astype(du_ref.dtype)
            sums_ref[0:1, c0:c0 + cw] += _colsum(da)
            sums_ref[0:1, d + c0:d + c0 + cw] += _colsum(dg)

    row = pl.BlockSpec((ts, d), lambda i: (i, 0))
    wide = pl.BlockSpec((ts, 2 * d), lambda i: (i, 0))
    return pl.pallas_call(
        body, name=name, grid=(n_tiles,),
        in_specs=[row, pl.BlockSpec((halo, d), lambda i: (jnp.minimum((i + 1) * per, s // halo - 1), 0)),
                  wide, pl.BlockSpec((halo, 2 * d), lambda i: (jnp.maximum(i * per - 1, 0), 0)),
                  pl.BlockSpec((taps, d), lambda i: (0, 0))],
        out_specs=[wide, pl.BlockSpec((taps8, d), lambda i: (0, 0)), pl.BlockSpec((8, 2 * d), lambda i: (0, 0))],
        out_shape=[jax.ShapeDtypeStruct((s, 2 * d), ACT_DTYPE), jax.ShapeDtypeStruct((taps8, d), F32),
                   jax.ShapeDtypeStruct((8, 2 * d), F32)],
        scratch_shapes=[pltpu.VMEM((ts + halo, d), F32), pltpu.VMEM((ts + halo, d), F32),
                        pltpu.VMEM((7, ts + halo - 8, cw), F32), pltpu.VMEM((7, ts + halo - 8, cw), F32)],
        compiler_params=_params("arbitrary"),
    )(dcv, dcv, u, u, dw)


def _short_conv_fwd(name, z, w):
    s, d3 = z.shape
    d = d3 // 3
    taps = w.shape[0]
    ts, halo = _pick(s, CONV_TILE, 8), SHORT_HALO
    lead = halo - (taps - 1)
    per = ts // halo

    def body(z_ref, prev_ref, w_ref, q_ref, pbuf):
        i = pl.program_id(0)
        pbuf[0:halo, :] = jnp.where(i > 0, prev_ref[:, d:2 * d] * prev_ref[:, 2 * d:], 0.0)
        pbuf[halo:, :] = z_ref[:, d:2 * d] * z_ref[:, 2 * d:]
        for c0, cw in _chunks(d):
            acc = jnp.zeros((ts, cw), F32)
            for k in range(taps):
                acc += w_ref[k:k + 1, c0:c0 + cw] * pbuf[lead + k:lead + k + ts, c0:c0 + cw]
            q_ref[:, c0:c0 + cw] = (z_ref[:, c0:c0 + cw] * acc).astype(q_ref.dtype)

    return pl.pallas_call(
        body, name=name, grid=(s // ts,),
        in_specs=[pl.BlockSpec((ts, d3), lambda i: (i, 0)),
                  pl.BlockSpec((halo, d3), lambda i: (jnp.maximum(i * per - 1, 0), 0)),
                  pl.BlockSpec((taps, d), lambda i: (0, 0))],
        out_specs=pl.BlockSpec((ts, d), lambda i: (i, 0)),
        out_shape=jax.ShapeDtypeStruct((s, d), ACT_DTYPE),
        scratch_shapes=[pltpu.VMEM((ts + halo, d), F32)],
        compiler_params=_params("parallel"),
    )(z, z, w)


def _short_conv_bwd(name, dq, z, w, deps=()):
    s, d3 = z.shape
    d = d3 // 3
    taps = w.shape[0]
    ts, halo = _pick(s, CONV_TILE, 8), SHORT_HALO
    lead = halo - (taps - 1)
    per = ts // halo
    n_tiles = s // ts

    def body(dq_ref, dqn_ref, z_ref, zp_ref, zn_ref, w_ref, *rest):
        dz_ref, sums_ref, pbuf, ubuf = rest[len(deps):]
        i = pl.program_id(0)

        @pl.when(i == 0)
        def _():
            sums_ref[...] = jnp.zeros_like(sums_ref)

        pbuf[0:halo, :] = jnp.where(i > 0, zp_ref[:, d:2 * d] * zp_ref[:, 2 * d:], 0.0)
        pbuf[halo:, :] = z_ref[:, d:2 * d] * z_ref[:, 2 * d:]
        ubuf[0:ts, :] = dq_ref[...] * z_ref[:, 0:d]
        ubuf[ts:, :] = jnp.where(i < n_tiles - 1, dqn_ref[...] * zn_ref[:, 0:d], 0.0)
        for c0, cw in _chunks(d):
            du = ubuf[0:ts, c0:c0 + cw]
            conv = jnp.zeros((ts, cw), F32)
            dp = jnp.zeros((ts, cw), F32)
            for k in range(taps):
                wk = w_ref[k:k + 1, c0:c0 + cw]
                shifted = pbuf[lead + k:lead + k + ts, c0:c0 + cw]
                conv += wk * shifted
                dp += wk * ubuf[taps - 1 - k:taps - 1 - k + ts, c0:c0 + cw]
                sums_ref[k:k + 1, c0:c0 + cw] += _colsum(du * shifted)
            dz_ref[:, c0:c0 + cw] = (dq_ref[:, c0:c0 + cw] * conv).astype(dz_ref.dtype)
            dz_ref[:, d + c0:d + c0 + cw] = (dp * z_ref[:, 2 * d + c0:2 * d + c0 + cw]).astype(dz_ref.dtype)
            dz_ref[:, 2 * d + c0:2 * d + c0 + cw] = (dp * z_ref[:, d + c0:d + c0 + cw]).astype(dz_ref.dtype)

    last = s // halo - 1
    return pl.pallas_call(
        body, name=name, grid=(n_tiles,),
        in_specs=[pl.BlockSpec((ts, d), lambda i: (i, 0)),
                  pl.BlockSpec((halo, d), lambda i: (jnp.minimum((i + 1) * per, last), 0)),
                  pl.BlockSpec((ts, d3), lambda i: (i, 0)),
                  pl.BlockSpec((halo, d3), lambda i: (jnp.maximum(i * per - 1, 0), 0)),
                  pl.BlockSpec((halo, d3), lambda i: (jnp.minimum((i + 1) * per, last), 0)),
                  pl.BlockSpec((taps, d), lambda i: (0, 0))] + [ANY_SPEC] * len(deps),
        out_specs=[pl.BlockSpec((ts, d3), lambda i: (i, 0)), pl.BlockSpec((8, d), lambda i: (0, 0))],
        out_shape=[jax.ShapeDtypeStruct((s, d3), ACT_DTYPE), jax.ShapeDtypeStruct((8, d), F32)],
        scratch_shapes=[pltpu.VMEM((ts + halo, d), F32), pltpu.VMEM((ts + halo, d), F32)],
        compiler_params=_params("arbitrary"),
    )(dq, dq, z, z, z, w, *deps)


def _silu(v):
    return v * jax.nn.sigmoid(v)


def _modulation(name, c_all, mod_w, mod_b_cols):
    nl, d, n = mod_w.shape
    b = c_all.shape[0]
    tn = _pick(n, 512)

    def body(c_ref, w_ref, b_ref, o_ref):
        ca = _silu(c_ref[...]).astype(MXU_DTYPE)
        o_ref[...] = jnp.dot(ca, w_ref[...].astype(MXU_DTYPE), preferred_element_type=F32) + b_ref[...]

    return pl.pallas_call(
        body, name=name, grid=(nl, n // tn),
        in_specs=[pl.BlockSpec((b, d), lambda l, j: (0, 0)),
                  pl.BlockSpec((None, d, tn), lambda l, j: (l, 0, j)),
                  pl.BlockSpec((None, 1, tn), lambda l, j: (l, 0, j))],
        out_specs=pl.BlockSpec((None, b, tn), lambda l, j: (l, 0, j)),
        out_shape=jax.ShapeDtypeStruct((nl, b, n), F32),
        compiler_params=_params("parallel", "parallel"),
    )(c_all, mod_w, mod_b_cols.reshape(nl, 1, n))


def _adamw(g, w, m, v):
    m = ADAM_B1 * m + (1.0 - ADAM_B1) * g
    v = ADAM_B2 * v + (1.0 - ADAM_B2) * (g * g)
    m_hat = m / (1.0 - ADAM_B1 ** ADAM_STEP)
    v_hat = v / (1.0 - ADAM_B2 ** ADAM_STEP)
    delta = -ADAM_LR * (m_hat / (jnp.sqrt(v_hat) + ADAM_EPS) + ADAM_WD * w)
    return delta, m, v


def _write_update(g, w_ref, m_ref, v_ref, outs):
    delta, m, v = _adamw(g, w_ref[...], m_ref[...], v_ref[...])
    outs[0][...] = g
    outs[1][...] = delta
    outs[2][...] = m
    outs[3][...] = v


def _modulation_update(name, c_all_t, dmod, w, m, v, deps=()):
    nl, d, n = w.shape
    b = c_all_t.shape[1]
    tr = _pick(d, 256, 8)

    def body(c_ref, dm_ref, w_ref, m_ref, v_ref, *rest):
        outs = rest[len(deps):]
        ca = _silu(c_ref[...])
        dm = dm_ref[...]
        g = ca[:, 0:1] * dm[0:1, :]
        for i in range(1, b):
            g += ca[:, i:i + 1] * dm[i:i + 1, :]
        _write_update(g, w_ref, m_ref, v_ref, outs)

    blk = pl.BlockSpec((None, tr, n), lambda l, r: (l, r, 0))
    return pl.pallas_call(
        body, name=name, grid=(nl, d // tr),
        in_specs=[pl.BlockSpec((tr, b), lambda l, r: (r, 0)), pl.BlockSpec((None, b, n), lambda l, r: (l, 0, 0)),
                  blk, blk, blk] + [ANY_SPEC] * len(deps),
        out_specs=[blk] * 4, out_shape=[jax.ShapeDtypeStruct(w.shape, F32)] * 4,
        compiler_params=_params("parallel", "parallel"),
    )(c_all_t, dmod, w, m, v, *deps)


def _weight_update(name, parts, w, m, v):
    nl, rows, cols = w.shape
    tr = _pick(rows, max(8, (1 << 18) // cols), 8)

    def body(*refs):
        p_refs = refs[:nl]
        w_ref, m_ref, v_ref = refs[nl:nl + 3]
        outs = refs[nl + 3:]
        for q in range(nl):
            @pl.when(pl.program_id(0) == q)
            def _(q=q):
                g = p_refs[q][0].astype(F32)
                for k in range(1, N_CHIP):
                    g += p_refs[q][k].astype(F32)
                _write_update(g, w_ref, m_ref, v_ref, outs)

    def part_spec(q):
        return pl.BlockSpec((N_CHIP, tr, cols), lambda l, r: (0, jnp.where(l == q, r, 0), 0))

    blk = pl.BlockSpec((None, tr, cols), lambda l, r: (l, r, 0))
    return pl.pallas_call(
        body, name=name, grid=(nl, rows // tr),
        in_specs=[part_spec(q) for q in range(nl)] + [blk, blk, blk],
        out_specs=[blk] * 4, out_shape=[jax.ShapeDtypeStruct(w.shape, F32)] * 4,
        compiler_params=_params("arbitrary", "arbitrary"),
    )(*parts, w, m, v)


def _vector_update(name, gathered, w, m, v):
    _, rows, cols = gathered.shape
    rw = w.shape[0]

    def body(g_ref, w_ref, m_ref, v_ref, tot_ref, d_ref, m2_ref, v2_ref):
        tot = g_ref[0]
        for k in range(1, N_DEV):
            tot += g_ref[k]
        tot_ref[...] = tot
        delta, m2, v2 = _adamw(tot[0:rw], w_ref[...], m_ref[...], v_ref[...])
        d_ref[...] = delta
        m2_ref[...] = m2
        v2_ref[...] = v2

    vm = pl.BlockSpec(memory_space=pltpu.VMEM)
    return pl.pallas_call(
        body, name=name, in_specs=[vm] * 4, out_specs=[vm] * 4,
        out_shape=[jax.ShapeDtypeStruct((rows, cols), F32)] + [jax.ShapeDtypeStruct((rw, cols), F32)] * 3,
        compiler_params=pltpu.CompilerParams(vmem_limit_bytes=V7X_VMEM_LIMIT_BYTES),
    )(gathered, w, m, v)


def _plain_update(name, g, w, m, v):
    def body(g_ref, w_ref, m_ref, v_ref, d_ref, m2_ref, v2_ref):
        delta, m2, v2 = _adamw(g_ref[...], w_ref[...], m_ref[...], v_ref[...])
        d_ref[...] = delta
        m2_ref[...] = m2
        v2_ref[...] = v2

    vm = pl.BlockSpec(memory_space=pltpu.VMEM)
    return pl.pallas_call(
        body, name=name, in_specs=[vm] * 4, out_specs=[vm] * 3,
        out_shape=[jax.ShapeDtypeStruct(w.shape, F32)] * 3,
    )(g, w, m, v)


def _rows(a, width, mult=8):
    r = a.reshape(-1, width)
    pad = -r.shape[0] % mult
    return jnp.pad(r, ((0, pad), (0, 0))) if pad else r


def _pack(blocks, width):
    parts, spans, at = [], [], 0
    for a in blocks:
        n = a.size // width
        p = _rows(a, width)
        parts.append(p)
        spans.append((at, n))
        at += p.shape[0]
    return jnp.concatenate(parts, axis=0), spans


def kernel(x, c, mod_w, mod_b, pre_mix_g, post_mix_g, pre_ffn_g, post_ffn_g, a_w1, a_b1, a_dw, a_dwb, a_ln_g, a_ln_b, a_w2, a_b2, b_w_in, b_conv, b_w_out, f_w1, f_w2, loss_target, m_mod_w, m_mod_b, m_pre_mix_g, m_post_mix_g, m_pre_ffn_g, m_post_ffn_g, m_a_w1, m_a_b1, m_a_dw, m_a_dwb, m_a_ln_g, m_a_ln_b, m_a_w2, m_a_b2, m_b_w_in, m_b_conv, m_b_w_out, m_f_w1, m_f_w2, v_mod_w, v_mod_b, v_pre_mix_g, v_post_mix_g, v_pre_ffn_g, v_post_ffn_g, v_a_w1, v_a_b1, v_a_dw, v_a_dwb, v_a_ln_g, v_a_ln_b, v_a_w2, v_a_b2, v_b_w_in, v_b_conv, v_b_w_out, v_f_w1, v_f_w2):
    depth, d = pre_mix_g.shape
    n_a, n_b = a_w1.shape[0], b_w_in.shape[0]
    s = x.shape[1]
    dsh = d // N_DEV
    taps_a, taps_b = a_dw.shape[1], b_conv.shape[1]
    px, py, pc = _pos()
    me = 4 * px + 2 * py + pc
    x0 = x.reshape(s, d)
    target = loss_target.reshape(s, d)

    me1 = jnp.reshape(me, (1,)).astype(jnp.int32)

    def members(i):
        j = i // 2
        mix = [(a_w1, j, True), (a_w2, j, False)] if i % 2 == 0 else [(b_w_in, j, True), (b_w_out, j, False)]
        return mix, [(f_w1, i, True), (f_w2, i, False)]

    placed = {(i, q): _place("place_l%d_%d" % (i, q), w, layer, me1, by_columns)
              for i in range(depth) for q, (w, layer, by_columns) in enumerate(sum(members(i), []))}

    def start_group(tag, group, after=()):
        return _gather_start(tag, [placed[key] for key in group], after)

    packed0, spans0 = _pack([c, a_dw, b_conv], dsh)
    got0 = _small_gather("gather_cond", packed0)

    def full_width(span):
        at, n = span
        return jnp.transpose(got0[:, at:at + n, :], (1, 0, 2)).reshape(n, d)

    c_all = got0[:, spans0[0][0]:spans0[0][0] + spans0[0][1], :].reshape(N_DEV, d)
    a_dw_full = full_width(spans0[1]).reshape(n_a, taps_a, d)
    b_conv_full = full_width(spans0[2]).reshape(n_b, taps_b, d)

    n_mod = mod_w.shape[2]
    mod_b_cols = lax.dynamic_slice_in_dim(mod_b, me * n_mod, n_mod, axis=1)
    mod_local = _modulation("modulation", c_all, mod_w, mod_b_cols)
    got1 = _small_gather("gather_mod", mod_local.reshape(depth * N_DEV, n_mod))
    mod_me = lax.dynamic_index_in_dim(got1.reshape(N_DEV, depth, N_DEV, n_mod), me, axis=2, keepdims=False)
    mod_me = jnp.transpose(mod_me, (1, 0, 2)).reshape(depth, 6, 1, d)

    def vec(a, i):
        return a[i].reshape(1, -1)

    def pass_groups(tag, handles, after):
        for q in range(len(handles)):
            handles[q] = _gather_pass("%s%d" % (tag, q), handles[q], after)
            after = handles[q]["token"]
        return after

    def wait_groups(tag, handles, after):
        return sum([_gather_wait("%s%d" % (tag, q), hq, after) for q, hq in enumerate(handles)], [])

    mix_copy = [start_group("l0m", [(0, 0)], (got1,))]
    mix_late = [start_group("l0n", [(0, 1)], (mix_copy[0]["token"],))]
    ffn_copy = [start_group("l0u", [(0, 2)], (mix_late[0]["token"],))]
    ffn_copy.append(start_group("l0d", [(0, 3)], (ffn_copy[0]["token"],)))
    saved = []
    xs = x0
    h = _residual_fwd("fwd_in", xs, pre=(vec(pre_mix_g, 0), mod_me[0, 1], mod_me[0, 0]),
                      deps=[hq["token"] for hq in ffn_copy] + [buf for (i, _), buf in placed.items() if i > 0])[0]
    w_mix = wait_groups("l0m", mix_copy, pass_groups("l0m", mix_copy, h))
    for i in range(depth):
        j = i // 2
        sh_m, sc_m, gt_m, sh_f, sc_f, gt_f = [mod_me[i, q] for q in range(6)]
        keep = {"x_mix": xs, "h_mix": h}
        early = i > 0
        w1g, w2g = w_mix if early else (w_mix[0], None)
        if i % 2 == 0:
            u = _mm_cols("a_w1", h, w1g, F32, _store_bias, bias=vec(a_b1, j))[0]
            deps = [pass_groups("l%df" % i, ffn_copy, u)] if early else []
            if not early:
                w2g, = wait_groups("l0n", mix_late, pass_groups("l0n", mix_late, u))
            cv, v = _glu_conv_ln_swish("a_conv", u, a_dw_full[j], vec(a_dwb, j), vec(a_ln_g, j), vec(a_ln_b, j))
            y = _mm_rows("a_w2", v, w2g, F32, bias=vec(a_b2, j), deps=deps)
            keep.update(u=u, cv=cv, v=v)
        else:
            z = _mm_cols("b_w_in", h, w1g, F32)[0]
            deps = [pass_groups("l%df" % i, ffn_copy, z)] if early else []
            q = _short_conv_fwd("b_conv", z, b_conv_full[j])
            y = _mm_rows("b_w_out", q, w2g, F32, deps=deps)
            keep.update(z=z, q=q)
        after = y if early else pass_groups("l%df" % i, ffn_copy, y)
        deps = []
        if i + 1 < depth:
            mix_copy = [start_group("l%dm" % (i + 1), [(i + 1, 0), (i + 1, 1)])]
            ffn_next = [start_group("l%df" % (i + 1), [(i + 1, 2), (i + 1, 3)])]
            deps = [mix_copy[0]["token"], ffn_next[0]["token"]]
        fw1g, fw2g = wait_groups("l%df" % i, ffn_copy, after)
        keep.update(w1g=w1g, w2g=w2g, fw1g=fw1g, fw2g=fw2g, y_mix=y)
        xs, h = _residual_fwd("fwd_mid", xs, post=(y, vec(post_mix_g, i), gt_m),
                              pre=(vec(pre_ffn_g, i), sc_f, sh_f))
        keep.update(x_ffn=xs, h_ffn=h)
        act, hid = _mm_cols("f_w1", h, fw1g, ACT_DTYPE, _store_relu2, n_out=2, deps=deps)
        deps = [pass_groups("l%dm" % (i + 1), mix_copy, act)] if i + 1 < depth else []
        y = _mm_rows("f_w2", act, fw2g, F32, deps=deps)
        keep.update(act=act, hid=hid, y_ffn=y)
        saved.append(keep)
        if i + 1 < depth:
            w_mix = wait_groups("l%dm" % (i + 1), mix_copy, y)
            ffn_copy = ffn_next
            nxt = mod_me[i + 1]
            xs, h = _residual_fwd("fwd_next", xs, post=(y, vec(post_ffn_g, i), gt_f),
                                  pre=(vec(pre_mix_g, i + 1), nxt[1], nxt[0]))
        else:
            g, dy, loss_blk, sums = _loss_junction("loss_junction", xs, y, vec(post_ffn_g, i), gt_f, target)

    where = jnp.stack([pc, 2 * px + py]).astype(jnp.int32)
    land = {"a_w1": [None] * n_a, "a_w2": [None] * n_a, "b_w_in": [None] * n_b, "b_w_out": [None] * n_b,
            "f_w1": [None] * depth, "f_w2": [None] * depth}
    reductions = []

    def reduce_group(tag, slots, swap, after):
        grads, theirs = _sibling_wait(tag, swap, after)
        pairs = [_chip_partial("chip_partial_%s_%d" % (tag, q), gq, tq, where)
                 for q, (gq, tq) in enumerate(zip(grads, theirs))]
        handle = _reduce_start(tag, [p for p, _ in pairs], [l for _, l in pairs])
        reductions.append((tag, slots, handle))
        return handle["token"]

    zero_vec = jnp.zeros((d,), F32)
    dmod = [[zero_vec] * 6 for _ in range(depth)]
    small = {name: [None] * depth for name in ("pre_mix_g", "post_mix_g", "pre_ffn_g", "post_ffn_g")}
    small_a = {name: [None] * n_a for name in ("a_b1", "a_dwb", "a_ln_g", "a_ln_b", "a_b2", "a_dw")}
    small_b = {"b_conv": [None] * n_b}

    deps = []
    for i in reversed(range(depth)):
        j = i // 2
        kp = saved[i]
        sh_m, sc_m, gt_m, sh_f, sc_f, gt_f = [mod_me[i, q] for q in range(6)]
        dmod[i][5] = sums[3]
        small["post_ffn_g"][i] = sums[4]
        dhid = _mm_rows_t("f_w2_t", dy, kp["fw2g"], ACT_DTYPE, _store_relu2_grad, extra=kp["hid"], deps=deps)
        deps = []
        if i + 1 < depth:
            deps = [reduce_group("r%dm" % (i + 1), last_swap[0], last_swap[1], dhid)]
        g_fw2 = _grad_rows("f_w2_grad", kp["act"], dy, f_w2.shape[1])
        dh = _mm_cols_t("f_w1_t", dhid, kp["fw1g"], F32, deps=deps)
        g_fw1 = _grad_cols("f_w1_grad", kp["h_ffn"], dhid, f_w1.shape[2])
        ffn_swap = _sibling_start("r%df" % i, [g_fw1, g_fw2])
        deps = [ffn_swap["token"]]
        g, dy, sums = _residual_bwd("bwd_mid", g, pre=(dh, kp["x_ffn"], vec(pre_ffn_g, i), sc_f),
                                    post=(kp["y_mix"], vec(post_mix_g, i), gt_m))
        dmod[i][3], dmod[i][4], dmod[i][2] = sums[0], sums[1], sums[3]
        small["pre_ffn_g"][i], small["post_mix_g"][i] = sums[2], sums[4]
        if i % 2 == 0:
            small_a["a_b2"][j] = sums[5]
            dv = _mm_rows_t("a_w2_t", dy, kp["w2g"], F32, deps=deps)
            g_w2 = _grad_rows("a_w2_grad", kp["v"], dy, a_w2.shape[1])
            deps = [reduce_group("r%df" % i, (("f_w1", i), ("f_w2", i)), ffn_swap, g_w2)]
            dcv, lsum = _ln_swish_bwd("a_ln_bwd", dv, kp["cv"], vec(a_ln_g, j), vec(a_ln_b, j), deps=deps)
            small_a["a_ln_g"][j], small_a["a_ln_b"][j], small_a["a_dwb"][j] = lsum[0], lsum[1], lsum[2]
            du, ddw, usum = _conv_glu_bwd("a_conv_bwd", dcv, kp["u"], a_dw_full[j])
            small_a["a_dw"][j], small_a["a_b1"][j] = ddw[:taps_a], usum[0]
            dh = _mm_cols_t("a_w1_t", du, kp["w1g"], F32)
            g_w1 = _grad_cols("a_w1_grad", kp["h_mix"], du, a_w1.shape[2])
            names = ("a_w1", "a_w2")
        else:
            dq = _mm_rows_t("b_w_out_t", dy, kp["w2g"], F32, deps=deps)
            g_w2 = _grad_rows("b_w_out_grad", kp["q"], dy, b_w_out.shape[1])
            deps = [reduce_group("r%df" % i, (("f_w1", i), ("f_w2", i)), ffn_swap, g_w2)]
            dz, wsum = _short_conv_bwd("b_conv_bwd", dq, kp["z"], b_conv_full[j], deps=deps)
            small_b["b_conv"][j] = wsum[:taps_b]
            dh = _mm_cols_t("b_w_in_t", dz, kp["w1g"], F32)
            g_w1 = _grad_cols("b_w_in_grad", kp["h_mix"], dz, b_w_in.shape[2])
            names = ("b_w_in", "b_w_out")
        mix_swap = _sibling_start("r%dm" % i, [g_w1, g_w2])
        deps = [mix_swap["token"]]
        if i > 0:
            prev = saved[i - 1]
            g, dy, sums = _residual_bwd("bwd_next", g, pre=(dh, kp["x_mix"], vec(pre_mix_g, i), sc_m),
                                        post=(prev["y_ffn"], vec(post_ffn_g, i - 1), mod_me[i - 1, 5]))
        else:
            g, sums = _residual_bwd("bwd_in", g, pre=(dh, kp["x_mix"], vec(pre_mix_g, i), sc_m))
        dmod[i][0], dmod[i][1] = sums[0], sums[1]
        small["pre_mix_g"][i] = sums[2]
        last_swap = (((names[0], j), (names[1], j)), mix_swap)
    grad_x = g.reshape(x.shape)

    reps = [jnp.stack([jnp.stack(r) for r in dmod]),
            jnp.stack(small["pre_mix_g"]), jnp.stack(small["post_mix_g"]),
            jnp.stack(small["pre_ffn_g"]), jnp.stack(small["post_ffn_g"]),
            jnp.stack(small_a["a_b1"]), jnp.stack(small_a["a_dwb"]), jnp.stack(small_a["a_ln_g"]),
            jnp.stack(small_a["a_ln_b"]), jnp.stack(small_a["a_b2"])]
    rep_w = [mod_b, pre_mix_g, post_mix_g, pre_ffn_g, post_ffn_g, a_b1, a_dwb, a_ln_g, a_ln_b, a_b2]
    rep_m = [m_mod_b, m_pre_mix_g, m_post_mix_g, m_pre_ffn_g, m_post_ffn_g, m_a_b1, m_a_dwb, m_a_ln_g, m_a_ln_b, m_a_b2]
    rep_v = [v_mod_b, v_pre_mix_g, v_post_mix_g, v_pre_ffn_g, v_post_ffn_g, v_a_b1, v_a_dwb, v_a_ln_g, v_a_ln_b, v_a_b2]
    w_small, spans = _pack(rep_w, d)
    m_small, _ = _pack(rep_m, d)
    v_small, _ = _pack(rep_v, d)
    loss_row = jnp.pad(loss_blk[0:1, 0:1], ((0, 0), (0, d - 1)))
    part_small, spans_g = _pack(reps + [jnp.stack(small_a["a_dw"]), jnp.stack(small_b["b_conv"]), loss_row], d)
    got2 = _small_gather("gather_small", part_small, deps=[last_swap[1]["token"]])
    last_token = reduce_group("r0m", last_swap[0], last_swap[1], got2)
    total, d_small, m2_small, v2_small = _vector_update("vector_update", got2, w_small, m_small, v_small)

    def unpack(buf, span, like):
        return buf[span[0]:span[0] + span[1]].reshape(like.shape)

    rep_out = [[unpack(buf, sp, w) for sp, w in zip(spans, rep_w)] for buf in (total, d_small, m2_small, v2_small)]
    loss = total[spans_g[-1][0], 0]

    conv_g = [lax.dynamic_slice_in_dim(total[sp[0]:sp[0] + sp[1]], me * dsh, dsh, axis=1)
              for sp in spans_g[len(reps):len(reps) + 2]]
    gc, _ = _pack(conv_g, dsh)
    wc, spans_c = _pack([a_dw, b_conv], dsh)
    mc, _ = _pack([m_a_dw, m_b_conv], dsh)
    vc, _ = _pack([v_a_dw, v_b_conv], dsh)
    conv_out = [gc] + list(_plain_update("conv_update", gc, wc, mc, vc))
    conv_out = [[unpack(buf, sp, w) for sp, w in zip(spans_c, (a_dw, b_conv))] for buf in conv_out]

    at, n = spans_g[0]
    dmod_all = got2[:, at:at + n, :].reshape(N_DEV, depth, 6 * d)
    dmod_cols = jnp.transpose(lax.dynamic_slice_in_dim(dmod_all, me * n_mod, n_mod, axis=2), (1, 0, 2))
    mod_out = _modulation_update("modulation_update", jnp.transpose(c_all), dmod_cols, mod_w, m_mod_w, v_mod_w,
                                 deps=[last_token])

    big = {"f_w1": (f_w1, m_f_w1, v_f_w1), "f_w2": (f_w2, m_f_w2, v_f_w2),
           "b_w_in": (b_w_in, m_b_w_in, v_b_w_in), "b_w_out": (b_w_out, m_b_w_out, v_b_w_out),
           "a_w1": (a_w1, m_a_w1, v_a_w1), "a_w2": (a_w2, m_a_w2, v_a_w2)}
    big_out = {}
    after = mod_out[3]

    def update_complete():
        nonlocal after
        for k, wmv in big.items():
            if k not in big_out and all(b is not None for b in land[k]):
                big_out[k] = _weight_update("update_" + k, land[k], *wmv)
                after = big_out[k][3]

    for tag, slots, handle in reductions:
        if tag == reductions[-1][0]:
            update_complete()
        landed = _reduce_wait(tag, handle, after)
        after = landed[0]
        for (key, idx), buf in zip(slots, landed):
            land[key][idx] = buf
    update_complete()

    def family(q):
        rep = dict(zip(("mod_b", "pre_mix_g", "post_mix_g", "pre_ffn_g", "post_ffn_g",
                        "a_b1", "a_dwb", "a_ln_g", "a_ln_b", "a_b2"), rep_out[q]))
        return (mod_out[q], rep["mod_b"], rep["pre_mix_g"], rep["post_mix_g"], rep["pre_ffn_g"], rep["post_ffn_g"],
                big_out["a_w1"][q], rep["a_b1"], conv_out[q][0], rep["a_dwb"], rep["a_ln_g"], rep["a_ln_b"],
                big_out["a_w2"][q], rep["a_b2"], big_out["b_w_in"][q], conv_out[q][1], big_out["b_w_out"][q],
                big_out["f_w1"][q], big_out["f_w2"][q])

    return (loss, grad_x, *family(0), *family(1), *family(2), *family(3))
```

```python
import jax
import jax.numpy as jnp
from jax import lax
from jax.experimental import pallas as pl
from jax.experimental.pallas import tpu as pltpu

MXU_DTYPE = jnp.bfloat16
ACT_DTYPE = jnp.bfloat16
COMM_DTYPE = jnp.bfloat16

N_DEV = 8
N_CHIP = 4
RMS_EPS = 1e-6
LN_EPS = 1e-5
ADAM_LR = 0.001
ADAM_B1 = 0.9
ADAM_B2 = 0.999
ADAM_EPS = 1e-08
ADAM_WD = 0.01
ADAM_STEP = 10

V7X_VMEM_LIMIT_BYTES = 56 * 1024 * 1024
LANE = 128
ROW_TILE = 256
CONV_TILE = 128
CONV_HALO = 32
SHORT_HALO = 8
CONV_CHUNK = 256
MM_TM, MM_TN, MM_TK = 1024, 1024, 2048
MM_RESIDENT = 4 * 1024 * 1024

F32 = jnp.float32
MESH = pl.DeviceIdType.MESH


def _pick(dim, pref, mult=LANE):
    if dim <= pref:
        return dim
    t = pref - pref % mult
    while dim % t:
        t -= mult
    return t


def _params(*sem):
    return pltpu.CompilerParams(dimension_semantics=sem, vmem_limit_bytes=V7X_VMEM_LIMIT_BYTES)


def _pos():
    return lax.axis_index("x"), lax.axis_index("y"), lax.axis_index("c")


def _flip(v, bit):
    return 1 - v if bit else v


def _small_gather(name, v, deps=()):
    rows, cols = v.shape
    n_deps = len(deps)

    def body(v_ref, *rest):
        out_ref, send_sems, recv_sems, local_sem = rest[n_deps:]
        x, y, c = _pos()
        me, sibling = (x, y, c), (x, y, 1 - c)
        chips = [(1 - x, y), (x, 1 - y), (1 - x, 1 - y)]

        def block(px, py, pc):
            return out_ref.at[4 * px + 2 * py + pc]

        def copy(k, owner, to, src=None):
            return pltpu.make_async_remote_copy(
                src_ref=block(*owner) if src is None else src, dst_ref=block(*owner), send_sem=send_sems.at[k],
                recv_sem=recv_sems.at[k], device_id=to, device_id_type=MESH)

        mine = pltpu.make_async_copy(v_ref, block(*me), local_sem)
        mine.start()
        first = [copy(0, me, sibling, src=v_ref)]
        first += [copy(1 + j, me, (*chip, c), src=v_ref) for j, chip in enumerate(chips)]
        for cp in first:
            cp.start()
        passed = [copy(4 + j, (*chip, c), sibling) for j, chip in enumerate(chips)]
        for j, chip in enumerate(chips):
            copy(1 + j, (*chip, c), me).wait_recv()
            passed[j].start()
        copy(0, sibling, me).wait_recv()
        for j, chip in enumerate(chips):
            copy(4 + j, (*chip, 1 - c), me).wait_recv()
        for cp in first + passed:
            cp.wait_send()
        mine.wait()

    return pl.pallas_call(
        body, name=name,
        out_shape=jax.ShapeDtypeStruct((N_DEV, rows, cols), v.dtype),
        in_specs=[pl.BlockSpec(memory_space=pltpu.VMEM)] + [pl.BlockSpec(memory_space=pl.ANY)] * n_deps,
        out_specs=pl.BlockSpec(memory_space=pltpu.VMEM),
        scratch_shapes=[pltpu.SemaphoreType.DMA((N_DEV - 1,)), pltpu.SemaphoreType.DMA((N_DEV - 1,)),
                        pltpu.SemaphoreType.DMA],
        compiler_params=pltpu.CompilerParams(vmem_limit_bytes=V7X_VMEM_LIMIT_BYTES),
    )(v, *deps)


HBM_SPEC = pl.BlockSpec(memory_space=pltpu.HBM)
SEM_SPEC = pl.BlockSpec(memory_space=pltpu.SEMAPHORE)
ANY_SPEC = pl.BlockSpec(memory_space=pl.ANY)
TOKEN = jax.ShapeDtypeStruct((8, LANE), F32)
TOKEN_SPEC = pl.BlockSpec(memory_space=pltpu.VMEM)
SPLIT_COPY = pltpu.CompilerParams(has_side_effects=pltpu.SideEffectType.DATAFLOW_SIDE_EFFECTING)


def _hbm(a):
    return pltpu.with_memory_space_constraint(a, pltpu.HBM)


def _dma_sems(n):
    return pltpu.SemaphoreType.DMA((n,))


def _place(name, w, layer, me, by_columns, dtype=None):
    dtype = COMM_DTYPE if dtype is None else dtype
    _, rows, cols = w.shape
    tr = _pick(rows, max(8, (1 << 20) // cols), 8)

    def body(me_ref, w_ref, o_ref):
        o_ref[...] = w_ref[...].astype(o_ref.dtype)

    if by_columns:
        out_spec = pl.BlockSpec((tr, cols), lambda r, me_ref: (r, me_ref[0]))
        out_shape = jax.ShapeDtypeStruct((rows, N_DEV * cols), dtype)
    else:
        out_spec = pl.BlockSpec((None, tr, cols), lambda r, me_ref: (me_ref[0], r, 0))
        out_shape = jax.ShapeDtypeStruct((N_DEV, rows, cols), dtype)
    return pl.pallas_call(
        body, name=name,
        grid_spec=pltpu.PrefetchScalarGridSpec(
            num_scalar_prefetch=1, grid=(rows // tr,),
            in_specs=[pl.BlockSpec((None, tr, cols), lambda r, me_ref: (layer, r, 0))],
            out_specs=out_spec),
        out_shape=out_shape, compiler_params=_params("parallel"),
    )(me, w)


def _block_copy(buf, owner, to, send_sem, recv_sem):
    dev = 4 * owner[0] + 2 * owner[1] + owner[2]
    if len(buf.shape) == 3:
        blk = buf.at[dev]
    else:
        n = buf.shape[1] // N_DEV
        blk = buf.at[:, pl.ds(pl.multiple_of(dev * n, n), n)]
    return pltpu.make_async_remote_copy(src_ref=blk, dst_ref=blk, send_sem=send_sem, recv_sem=recv_sem,
                                        device_id=to, device_id_type=MESH)


def _other_chips(x, y):
    return [(1 - x, y), (x, 1 - y), (1 - x, 1 - y)]


def _gather_start(tag, bufs, after=()):
    n = len(bufs)
    n_in = n + len(after)

    def body(*refs):
        b = refs[:n]
        send, recv_ici, recv_sib = refs[n_in:n_in + 3]
        token = refs[-1]
        x, y, c = _pos()
        me = (x, y, c)
        for i in range(n):
            _block_copy(b[i], me, (x, y, 1 - c), send.at[4 * i], recv_sib.at[i]).start()
            for j, chip in enumerate(_other_chips(x, y)):
                _block_copy(b[i], me, (*chip, c), send.at[4 * i + 1 + j], recv_ici.at[3 * i + j]).start()
        token[...] = jnp.zeros_like(token)

    outs = pl.pallas_call(
        body, name="gather_start_" + tag,
        out_shape=(_dma_sems(4 * n), _dma_sems(3 * n), _dma_sems(n),
                   *[pltpu.HBM(b.shape, b.dtype) for b in bufs], TOKEN),
        in_specs=[HBM_SPEC] * n + [ANY_SPEC] * len(after),
        out_specs=(SEM_SPEC, SEM_SPEC, SEM_SPEC, *[HBM_SPEC] * n, TOKEN_SPEC),
        input_output_aliases={i: 3 + i for i in range(n)}, compiler_params=SPLIT_COPY,
    )(*[_hbm(b) for b in bufs], *after)
    return dict(send=outs[0], recv_ici=outs[1], recv_sib=outs[2], bufs=list(outs[3:3 + n]), token=outs[-1])


def _gather_pass(tag, h, after):
    bufs = h["bufs"]
    n = len(bufs)

    def body(*refs):
        b = refs[:n]
        recv_ici = refs[n]
        fsend, frecv = refs[n + 2:n + 4]
        token = refs[-1]
        x, y, c = _pos()
        for i in range(n):
            for j, chip in enumerate(_other_chips(x, y)):
                _block_copy(b[i], (*chip, c), (x, y, c), fsend.at[3 * i + j], recv_ici.at[3 * i + j]).wait_recv()
                _block_copy(b[i], (*chip, c), (x, y, 1 - c), fsend.at[3 * i + j], frecv.at[3 * i + j]).start()
        token[...] = jnp.zeros_like(token)

    outs = pl.pallas_call(
        body, name="gather_pass_" + tag,
        out_shape=(_dma_sems(3 * n), _dma_sems(3 * n), *[pltpu.HBM(b.shape, b.dtype) for b in bufs], TOKEN),
        in_specs=[HBM_SPEC] * n + [SEM_SPEC, ANY_SPEC],
        out_specs=(SEM_SPEC, SEM_SPEC, *[HBM_SPEC] * n, TOKEN_SPEC),
        input_output_aliases={i: 2 + i for i in range(n)}, compiler_params=SPLIT_COPY,
    )(*bufs, h["recv_ici"], after)
    return dict(h, fsend=outs[0], frecv=outs[1], bufs=list(outs[2:2 + n]), token=outs[-1])


def _gather_wait(tag, h, after):
    bufs = h["bufs"]
    n = len(bufs)

    def body(*refs):
        b = refs[:n]
        send, recv_sib, fsend, frecv = refs[n:n + 4]
        x, y, c = _pos()
        me, sibling = (x, y, c), (x, y, 1 - c)
        for i in range(n):
            _block_copy(b[i], sibling, me, send.at[4 * i], recv_sib.at[i]).wait_recv()
            for j, chip in enumerate(_other_chips(x, y)):
                _block_copy(b[i], (*chip, 1 - c), me, fsend.at[3 * i + j], frecv.at[3 * i + j]).wait_recv()
            for k in range(4):
                _block_copy(b[i], me, sibling, send.at[4 * i + k], recv_sib.at[i]).wait_send()
            for j, chip in enumerate(_other_chips(x, y)):
                _block_copy(b[i], (*chip, c), sibling, fsend.at[3 * i + j], frecv.at[3 * i + j]).wait_send()

    outs = pl.pallas_call(
        body, name="gather_wait_" + tag,
        out_shape=tuple(pltpu.HBM(b.shape, b.dtype) for b in bufs),
        in_specs=[HBM_SPEC] * n + [SEM_SPEC] * 4 + [ANY_SPEC], out_specs=tuple([HBM_SPEC] * n),
        input_output_aliases={i: i for i in range(n)}, compiler_params=SPLIT_COPY,
    )(*bufs, h["send"], h["recv_sib"], h["fsend"], h["frecv"], after)
    return list(outs)


def _sibling_copy(grad, land, send_sem, recv_sem):
    x, y, c = _pos()
    return pltpu.make_async_remote_copy(
        src_ref=grad.at[:, 1 - c], dst_ref=land, send_sem=send_sem, recv_sem=recv_sem,
        device_id=(x, y, 1 - c), device_id_type=MESH)


def _sibling_start(tag, grads):
    n = len(grads)
    landings = [lax.empty((N_CHIP,) + g.shape[2:], g.dtype) for g in grads]

    def body(*refs):
        g, land = refs[:n], refs[n:2 * n]
        send, recv = refs[2 * n:2 * n + 2]
        token = refs[-1]
        for i in range(n):
            _sibling_copy(g[i], land[i], send.at[i], recv.at[i]).start()
        token[...] = jnp.zeros_like(token)

    both = list(grads) + landings
    outs = pl.pallas_call(
        body, name="sibling_start_" + tag,
        out_shape=(_dma_sems(n), _dma_sems(n), *[pltpu.HBM(b.shape, b.dtype) for b in both], TOKEN),
        in_specs=[HBM_SPEC] * (2 * n), out_specs=(SEM_SPEC, SEM_SPEC, *[HBM_SPEC] * (2 * n), TOKEN_SPEC),
        input_output_aliases={i: 2 + i for i in range(2 * n)}, compiler_params=SPLIT_COPY,
    )(*[_hbm(b) for b in both])
    return dict(send=outs[0], recv=outs[1], grads=list(outs[2:2 + n]), landings=list(outs[2 + n:2 + 2 * n]),
                token=outs[-1])


def _sibling_wait(tag, h, after):
    n = len(h["grads"])

    def body(*refs):
        g, land = refs[:n], refs[n:2 * n]
        send, recv = refs[2 * n:2 * n + 2]
        for i in range(n):
            _sibling_copy(g[i], land[i], send.at[i], recv.at[i]).wait()

    both = h["grads"] + h["landings"]
    outs = pl.pallas_call(
        body, name="sibling_wait_" + tag,
        out_shape=tuple(pltpu.HBM(b.shape, b.dtype) for b in both),
        in_specs=[HBM_SPEC] * (2 * n) + [SEM_SPEC, SEM_SPEC, ANY_SPEC], out_specs=tuple([HBM_SPEC] * (2 * n)),
        input_output_aliases={i: i for i in range(2 * n)}, compiler_params=SPLIT_COPY,
    )(*both, h["send"], h["recv"], after)
    return list(outs[:n]), list(outs[n:])


def _reduce_copy(part, land, x, y, c, k, src_chip, send_sem, recv_sem):
    px, py = _flip(x, k >> 1 & 1), _flip(y, k & 1)
    return pltpu.make_async_remote_copy(src_ref=part.at[2 * px + py], dst_ref=land.at[src_chip], send_sem=send_sem,
                                        recv_sem=recv_sem, device_id=(px, py, c), device_id_type=MESH)


def _reduce_start(tag, partials, landings):
    n = len(partials)

    def body(*refs):
        p, land = refs[:n], refs[n:2 * n]
        send, recv = refs[2 * n:2 * n + 2]
        token = refs[-1]
        x, y, c = _pos()
        for i in range(n):
            for k in range(1, N_CHIP):
                _reduce_copy(p[i], land[i], x, y, c, k, 2 * x + y, send.at[3 * i + k - 1], recv.at[3 * i + k - 1]).start()
        token[...] = jnp.zeros_like(token)

    both = list(partials) + list(landings)
    outs = pl.pallas_call(
        body, name="reduce_start_" + tag,
        out_shape=(_dma_sems(3 * n), _dma_sems(3 * n), *[pltpu.HBM(b.shape, b.dtype) for b in both], TOKEN),
        in_specs=[HBM_SPEC] * (2 * n), out_specs=(SEM_SPEC, SEM_SPEC, *[HBM_SPEC] * (2 * n), TOKEN_SPEC),
        input_output_aliases={i: 2 + i for i in range(2 * n)}, compiler_params=SPLIT_COPY,
    )(*[_hbm(b) for b in both])
    return dict(send=outs[0], recv=outs[1], partials=list(outs[2:2 + n]), landings=list(outs[2 + n:2 + 2 * n]),
                token=outs[-1])


def _reduce_wait(tag, h, after):
    n = len(h["partials"])

    def body(*refs):
        p, land = refs[:n], refs[n:2 * n]
        send, recv = refs[2 * n:2 * n + 2]
        x, y, c = _pos()
        for i in range(n):
            for k in range(1, N_CHIP):
                src_chip = 2 * _flip(x, k >> 1 & 1) + _flip(y, k & 1)
                cp = _reduce_copy(p[i], land[i], x, y, c, k, src_chip, send.at[3 * i + k - 1], recv.at[3 * i + k - 1])
                cp.wait_recv()
                cp.wait_send()

    both = h["partials"] + h["landings"]
    outs = pl.pallas_call(
        body, name="reduce_wait_" + tag,
        out_shape=tuple(pltpu.HBM(b.shape, b.dtype) for b in both),
        in_specs=[HBM_SPEC] * (2 * n) + [SEM_SPEC, SEM_SPEC, ANY_SPEC], out_specs=tuple([HBM_SPEC] * (2 * n)),
        input_output_aliases={i: i for i in range(2 * n)}, compiler_params=SPLIT_COPY,
    )(*both, h["send"], h["recv"], after)
    return list(outs[n:])


def _chip_partial(name, grad, theirs, where):
    _, _, rows, cols = grad.shape
    tr = _pick(rows, max(8, (1 << 20) // cols), 8)
    landing = lax.empty(theirs.shape, theirs.dtype)

    def body(where_ref, g_ref, t_ref, land_in, o_ref, land_ref):
        total = (g_ref[...].astype(F32) + t_ref[...].astype(F32)).astype(o_ref.dtype)
        o_ref[...] = total

        @pl.when(pl.program_id(1) == where_ref[1])
        def _():
            land_ref[...] = total

    return pl.pallas_call(
        body, name=name,
        grid_spec=pltpu.PrefetchScalarGridSpec(
            num_scalar_prefetch=1, grid=(rows // tr, N_CHIP),
            in_specs=[pl.BlockSpec((None, None, tr, cols), lambda r, k, w: (k, w[0], r, 0)),
                      pl.BlockSpec((None, tr, cols), lambda r, k, w: (k, r, 0)), ANY_SPEC],
            out_specs=[pl.BlockSpec((None, tr, cols), lambda r, k, w: (k, r, 0)),
                       pl.BlockSpec((None, tr, cols), lambda r, k, w: (w[1], r, 0))]),
        out_shape=[jax.ShapeDtypeStruct(theirs.shape, theirs.dtype)] * 2,
        input_output_aliases={3: 1},
        compiler_params=_params("parallel", "arbitrary"),
    )(where, grad, theirs, landing)


def _matmul(name, a, b, a_spec, b_spec, grid, acc_shape, out_shape, out_specs, epilogue,
            ta=False, tb=False, extras=(), extra_specs=(), deps=()):
    nk = grid[2]
    n_extra = len(extras)
    n_in = 2 + n_extra + len(deps)
    dims = (((0,) if ta else (1,), (1,) if tb else (0,)), ((), ()))

    def body(*refs):
        a_ref, b_ref = refs[:2]
        extra_refs = refs[2:2 + n_extra]
        out_refs = refs[n_in:n_in + len(out_shape)]

        def product():
            return lax.dot_general(a_ref[...].astype(MXU_DTYPE), b_ref[...].astype(MXU_DTYPE), dims,
                                   preferred_element_type=F32)

        if nk == 1:
            epilogue(product(), extra_refs, out_refs)
            return
        acc_ref = refs[-1]
        k = pl.program_id(2)

        @pl.when(k == 0)
        def _():
            acc_ref[...] = product()

        if nk > 2:
            @pl.when((k > 0) & (k < nk - 1))
            def _():
                acc_ref[...] += product()

        @pl.when(k == nk - 1)
        def _():
            epilogue(acc_ref[...] + product(), extra_refs, out_refs)

    return pl.pallas_call(
        body, name=name, grid=grid,
        in_specs=[a_spec, b_spec, *extra_specs, *[ANY_SPEC] * len(deps)], out_specs=out_specs, out_shape=out_shape,
        scratch_shapes=[pltpu.VMEM(acc_shape, F32)] if nk > 1 else [],
        compiler_params=_params("parallel", "parallel", "arbitrary"),
    )(a, b, *extras, *deps)


def _store(acc, extra_refs, out_refs):
    out_refs[0][...] = acc.astype(out_refs[0].dtype)


def _store_bias(acc, extra_refs, out_refs):
    out_refs[0][...] = (acc + extra_refs[0][...]).astype(out_refs[0].dtype)


def _store_relu2(acc, extra_refs, out_refs):
    r = jnp.maximum(acc, 0.0)
    out_refs[0][...] = (r * r).astype(out_refs[0].dtype)
    out_refs[1][...] = acc.astype(out_refs[1].dtype)


def _store_relu2_grad(acc, extra_refs, out_refs):
    hid = extra_refs[0][...].astype(F32)
    out_refs[0][...] = (acc * (2.0 * jnp.maximum(hid, 0.0))).astype(out_refs[0].dtype)


def _mm_cols(name, a, wc, out_dtype, epilogue=_store, bias=None, n_out=1, deps=()):
    s, kdim = a.shape
    n = wc.shape[1]
    tm, tn, tk = _pick(s, MM_TM), _pick(n, MM_TN), _pick(kdim, MM_TK)
    extras, extra_specs = (), ()
    if bias is not None:
        extras, extra_specs = (bias,), (pl.BlockSpec((1, tn), lambda i, j, k: (0, j)),)
    out = jax.ShapeDtypeStruct((s, n), out_dtype)
    spec = pl.BlockSpec((tm, tn), lambda i, j, k: (i, j))
    return _matmul(
        name, a, wc,
        pl.BlockSpec((tm, tk), lambda i, j, k: (i, k)),
        pl.BlockSpec((tk, tn), lambda i, j, k: (k, j)),
        (s // tm, n // tn, kdim // tk), (tm, tn),
        [out] * n_out, [spec] * n_out, epilogue, extras=extras, extra_specs=extra_specs, deps=deps)


def _mm_rows(name, a, wr, out_dtype, bias=None, deps=()):
    s, kdim = a.shape
    w = wr.reshape(kdim, wr.shape[2])
    n = w.shape[1]
    tm, tn, tk = _pick(s, MM_TM), _pick(n, MM_TN), _pick(kdim, MM_TK)
    if kdim * n <= MM_RESIDENT:
        tm, tn = _pick(s, MM_TM // 2), n
    extras, extra_specs, epilogue = (), (), _store
    if bias is not None:
        extras, extra_specs, epilogue = (bias,), (pl.BlockSpec((1, tn), lambda i, j, k: (0, j)),), _store_bias
    return _matmul(
        name, a, w,
        pl.BlockSpec((tm, tk), lambda i, j, k: (i, k)),
        pl.BlockSpec((tk, tn), lambda i, j, k: (k, j)),
        (s // tm, n // tn, kdim // tk), (tm, tn),
        [jax.ShapeDtypeStruct((s, n), out_dtype)], [pl.BlockSpec((tm, tn), lambda i, j, k: (i, j))],
        epilogue, extras=extras, extra_specs=extra_specs, deps=deps)[0]


def _mm_cols_t(name, dy, wc, out_dtype, deps=()):
    s, n = dy.shape
    kdim = wc.shape[0]
    tm, tn, tk = _pick(s, MM_TM), _pick(kdim, MM_TN), _pick(n, MM_TK)
    return _matmul(
        name, dy, wc,
        pl.BlockSpec((tm, tk), lambda i, j, k: (i, k)),
        pl.BlockSpec((tn, tk), lambda i, j, k: (j, k)),
        (s // tm, kdim // tn, n // tk), (tm, tn),
        [jax.ShapeDtypeStruct((s, kdim), out_dtype)], [pl.BlockSpec((tm, tn), lambda i, j, k: (i, j))],
        _store, tb=True, deps=deps)[0]


def _mm_rows_t(name, dy, wr, out_dtype, epilogue=_store, extra=None, deps=()):
    s, n = dy.shape
    w = wr.reshape(-1, n)
    kdim = w.shape[0]
    tm, tn, tk = _pick(s, MM_TM), _pick(kdim, MM_TN), _pick(n, MM_TK)
    if kdim * n <= MM_RESIDENT:
        tm, tn = _pick(s, MM_TM // 2), kdim
    extras, extra_specs = (), ()
    if extra is not None:
        extras, extra_specs = (extra,), (pl.BlockSpec((tm, tn), lambda i, j, k: (i, j)),)
    return _matmul(
        name, dy, w,
        pl.BlockSpec((tm, tk), lambda i, j, k: (i, k)),
        pl.BlockSpec((tn, tk), lambda i, j, k: (j, k)),
        (s // tm, kdim // tn, n // tk), (tm, tn),
        [jax.ShapeDtypeStruct((s, kdim), out_dtype)], [pl.BlockSpec((tm, tn), lambda i, j, k: (i, j))],
        epilogue, tb=True, extras=extras, extra_specs=extra_specs, deps=deps)[0]


def _grad_cols(name, h, dy, n):
    s, kdim = h.shape
    tm, tn, tk = _pick(kdim, MM_TM), _pick(n, MM_TN), _pick(s, MM_TK)
    per = n // tn
    return _matmul(
        name, h, dy,
        pl.BlockSpec((tk, tm), lambda i, j, k: (k, i)),
        pl.BlockSpec((tk, tn), lambda i, j, k: (k, j)),
        (kdim // tm, N_DEV * per, s // tk), (tm, tn),
        [jax.ShapeDtypeStruct((N_DEV, kdim, n), COMM_DTYPE)],
        [pl.BlockSpec((None, tm, tn), lambda i, j, k: (j // per, i, j % per))],
        _store, ta=True)[0].reshape(N_CHIP, 2, kdim, n)


def _grad_rows(name, v, dy, kk):
    s, kdim = v.shape
    n = dy.shape[1]
    tm, tn, tk = _pick(kdim, MM_TM), _pick(n, MM_TN), _pick(s, MM_TK)
    return _matmul(
        name, v, dy,
        pl.BlockSpec((tk, tm), lambda i, j, k: (k, i)),
        pl.BlockSpec((tk, tn), lambda i, j, k: (k, j)),
        (kdim // tm, n // tn, s // tk), (tm, tn),
        [jax.ShapeDtypeStruct((kdim, n), COMM_DTYPE)],
        [pl.BlockSpec((tm, tn), lambda i, j, k: (i, j))],
        _store, ta=True)[0].reshape(N_CHIP, 2, kk, n)


def _rms(v):
    return lax.rsqrt(jnp.mean(v * v, axis=-1, keepdims=True) + RMS_EPS)


def _colsum(v):
    return jnp.sum(v, axis=0, keepdims=True)


def _vec_spec(width):
    return pl.BlockSpec((1, width), lambda i: (0, 0))


def _residual_fwd(name, x, post=None, pre=None, deps=()):
    s, d = x.shape
    ts = _pick(s, ROW_TILE, 8)
    row = pl.BlockSpec((ts, d), lambda i: (i, 0))
    ins, specs = [x], [row]
    if post is not None:
        ins += list(post)
        specs += [row, _vec_spec(d), _vec_spec(d)]
    if pre is not None:
        ins += list(pre)
        specs += [_vec_spec(d)] * 3
    ins += list(deps)
    specs += [ANY_SPEC] * len(deps)
    outs, out_specs = [], []
    if post is not None:
        outs.append(jax.ShapeDtypeStruct((s, d), F32))
        out_specs.append(row)
    if pre is not None:
        outs.append(jax.ShapeDtypeStruct((s, d), ACT_DTYPE))
        out_specs.append(row)

    def body(*refs):
        refs = list(refs)
        xv = refs.pop(0)[...]
        if post is not None:
            y_ref, gp_ref, gt_ref = refs[:3]
            del refs[:3]
        if pre is not None:
            g_ref, sc_ref, sh_ref = refs[:3]
            del refs[:3]
        del refs[:len(deps)]
        if post is not None:
            yv = y_ref[...]
            xv = xv + gt_ref[...] * ((yv * _rms(yv)) * gp_ref[...])
            refs.pop(0)[...] = xv
        if pre is not None:
            hv = ((xv * _rms(xv)) * g_ref[...]) * (1.0 + sc_ref[...]) + sh_ref[...]
            refs.pop(0)[...] = hv.astype(ACT_DTYPE)

    return pl.pallas_call(
        body, name=name, grid=(s // ts,), in_specs=specs, out_specs=out_specs, out_shape=outs,
        compiler_params=_params("parallel"),
    )(*ins)


def _loss_junction(name, x, y, g_post, gt, target):
    s, d = x.shape
    ts = _pick(s, ROW_TILE, 8)
    row = pl.BlockSpec((ts, d), lambda i: (i, 0))

    def body(x_ref, y_ref, gpost_ref, gt_ref, t_ref, g_ref, dy_ref, loss_ref, sums_ref):
        @pl.when(pl.program_id(0) == 0)
        def _():
            loss_ref[...] = jnp.zeros_like(loss_ref)
            sums_ref[...] = jnp.zeros_like(sums_ref)

        yv = y_ref[...]
        r = _rms(yv)
        nrm = yv * r
        rn = nrm * gpost_ref[...]
        err = (x_ref[...] + gt_ref[...] * rn) - t_ref[...]
        loss_ref[...] += (0.5 / d) * jnp.sum(err * err)
        g = err * (1.0 / d)
        g_ref[...] = g
        sums_ref[3:4, :] += _colsum(g * rn)
        d_o = g * gt_ref[...]
        sums_ref[4:5, :] += _colsum(d_o * nrm)
        dn = d_o * gpost_ref[...]
        dy = r * (dn - nrm * jnp.mean(dn * nrm, axis=-1, keepdims=True))
        sums_ref[5:6, :] += _colsum(dy)
        dy_ref[...] = dy.astype(dy_ref.dtype)

    return pl.pallas_call(
        body, name=name, grid=(s // ts,),
        in_specs=[row, row, _vec_spec(d), _vec_spec(d), row],
        out_specs=[row, row, pl.BlockSpec((8, LANE), lambda i: (0, 0)), pl.BlockSpec((8, d), lambda i: (0, 0))],
        out_shape=[jax.ShapeDtypeStruct((s, d), F32), jax.ShapeDtypeStruct((s, d), ACT_DTYPE),
                   jax.ShapeDtypeStruct((8, LANE), F32), jax.ShapeDtypeStruct((8, d), F32)],
        compiler_params=_params("arbitrary"),
    )(x, y, g_post, gt, target)


def _residual_bwd(name, g_out, pre=None, post=None):
    s, d = g_out.shape
    ts = _pick(s, ROW_TILE, 8)
    row = pl.BlockSpec((ts, d), lambda i: (i, 0))
    ins, specs = [g_out], [row]
    outs, out_specs = [], []
    if pre is not None:
        ins += list(pre)
        specs += [row, row, _vec_spec(d), _vec_spec(d)]
        outs.append(jax.ShapeDtypeStruct((s, d), F32))
        out_specs.append(row)
    if post is not None:
        ins += list(post)
        specs += [row, _vec_spec(d), _vec_spec(d)]
        outs.append(jax.ShapeDtypeStruct((s, d), ACT_DTYPE))
        out_specs.append(row)
    outs.append(jax.ShapeDtypeStruct((8, d), F32))
    out_specs.append(pl.BlockSpec((8, d), lambda i: (0, 0)))

    def body(*refs):
        refs = list(refs)
        g = refs.pop(0)[...]
        if pre is not None:
            dh_ref, x_ref, gpre_ref, sc_ref = refs[:4]
            del refs[:4]
        if post is not None:
            y_ref, gpost_ref, gt_ref = refs[:3]
            del refs[:3]
        sums_ref = refs[-1]

        @pl.when(pl.program_id(0) == 0)
        def _():
            sums_ref[...] = jnp.zeros_like(sums_ref)

        if pre is not None:
            dh, xv = dh_ref[...], x_ref[...]
            r = _rms(xv)
            nrm = xv * r
            d_rn = dh * (1.0 + sc_ref[...])
            sums_ref[0:1, :] += _colsum(dh)
            sums_ref[1:2, :] += _colsum(dh * (nrm * gpre_ref[...]))
            sums_ref[2:3, :] += _colsum(d_rn * nrm)
            dn = d_rn * gpre_ref[...]
            g = g + r * (dn - nrm * jnp.mean(dn * nrm, axis=-1, keepdims=True))
            refs.pop(0)[...] = g
        if post is not None:
            yv = y_ref[...]
            r = _rms(yv)
            nrm = yv * r
            sums_ref[3:4, :] += _colsum(g * (nrm * gpost_ref[...]))
            d_o = g * gt_ref[...]
            sums_ref[4:5, :] += _colsum(d_o * nrm)
            dn = d_o * gpost_ref[...]
            dy = r * (dn - nrm * jnp.mean(dn * nrm, axis=-1, keepdims=True))
            sums_ref[5:6, :] += _colsum(dy)
            refs.pop(0)[...] = dy.astype(ACT_DTYPE)

    return pl.pallas_call(
        body, name=name, grid=(s // ts,), in_specs=specs, out_specs=out_specs, out_shape=outs,
        compiler_params=_params("arbitrary"),
    )(*ins)


def _chunks(d):
    cw = min(CONV_CHUNK, d)
    return [(c * cw, cw) for c in range(d // cw)]


def _glu_of(u_ref, d):
    return u_ref[:, :d].astype(F32) * jax.nn.sigmoid(u_ref[:, d:].astype(F32))


def _gated_input(z_ref, d):
    return z_ref[:, d:2 * d].astype(F32) * z_ref[:, 2 * d:].astype(F32)


def _shift_rows(buf, sh, c0, cw, rows):
    for r in range(1, 8):
        sh[r - 1] = buf[r:r + rows, c0:c0 + cw]


def _window(buf, sh, c0, cw, offset, ts):
    q, r = divmod(offset, 8)
    if r == 0:
        return buf[8 * q:8 * q + ts, c0:c0 + cw]
    return sh[r - 1, 8 * q:8 * q + ts, :]


def _glu_conv_ln_swish(name, u, dw, dwb, ln_g, ln_b):
    s, d2 = u.shape
    d = d2 // 2
    taps = dw.shape[0]
    ts, halo = _pick(s, CONV_TILE, 8), CONV_HALO
    lead = halo - (taps - 1)
    per = ts // halo
    cw = min(CONV_CHUNK, d)

    def body(cur_ref, prev_ref, dw_ref, dwb_ref, g_ref, b_ref, cv_ref, v_ref, buf, sh):
        i = pl.program_id(0)
        buf[0:halo, :] = jnp.where(i > 0, _glu_of(prev_ref, d), 0.0)
        buf[halo:, :] = _glu_of(cur_ref, d)
        for c0, _ in _chunks(d):
            _shift_rows(buf, sh, c0, cw, ts + halo - 8)
            acc = jnp.zeros((ts, cw), F32)
            for k in range(taps):
                acc += dw_ref[k:k + 1, c0:c0 + cw] * _window(buf, sh, c0, cw, lead + k, ts)
            cv_ref[:, c0:c0 + cw] = acc + dwb_ref[:, c0:c0 + cw]
        cv = cv_ref[...]
        mu = jnp.mean(cv, axis=-1, keepdims=True)
        xc = cv - mu
        var = jnp.mean(xc * xc, axis=-1, keepdims=True)
        ln = (xc * lax.rsqrt(var + LN_EPS)) * g_ref[...] + b_ref[...]
        v_ref[...] = (ln * jax.nn.sigmoid(ln)).astype(v_ref.dtype)

    row = pl.BlockSpec((ts, d), lambda i: (i, 0))
    return pl.pallas_call(
        body, name=name, grid=(s // ts,),
        in_specs=[pl.BlockSpec((ts, d2), lambda i: (i, 0)),
                  pl.BlockSpec((halo, d2), lambda i: (jnp.maximum(i * per - 1, 0), 0)),
                  pl.BlockSpec((taps, d), lambda i: (0, 0)), _vec_spec(d), _vec_spec(d), _vec_spec(d)],
        out_specs=[row, row],
        out_shape=[jax.ShapeDtypeStruct((s, d), F32), jax.ShapeDtypeStruct((s, d), ACT_DTYPE)],
        scratch_shapes=[pltpu.VMEM((ts + halo, d), F32), pltpu.VMEM((7, ts + halo - 8, cw), F32)],
        compiler_params=_params("parallel"),
    )(u, u, dw, dwb, ln_g, ln_b)


def _ln_swish_bwd(name, dv, cv, ln_g, ln_b, deps=()):
    s, d = cv.shape
    ts = _pick(s, ROW_TILE, 8)

    def body(dv_ref, cv_ref, g_ref, b_ref, *rest):
        dcv_ref, sums_ref = rest[len(deps):]

        @pl.when(pl.program_id(0) == 0)
        def _():
            sums_ref[...] = jnp.zeros_like(sums_ref)

        cv = cv_ref[...]
        mu = jnp.mean(cv, axis=-1, keepdims=True)
        xc = cv - mu
        rstd = lax.rsqrt(jnp.mean(xc * xc, axis=-1, keepdims=True) + LN_EPS)
        nhat = xc * rstd
        ln = nhat * g_ref[...] + b_ref[...]
        sg = jax.nn.sigmoid(ln)
        dl = dv_ref[...] * (sg * (1.0 + ln * (1.0 - sg)))
        sums_ref[0:1, :] += _colsum(dl * nhat)
        sums_ref[1:2, :] += _colsum(dl)
        dn = dl * g_ref[...]
        dcv = rstd * (dn - jnp.mean(dn, axis=-1, keepdims=True)
                      - nhat * jnp.mean(dn * nhat, axis=-1, keepdims=True))
        sums_ref[2:3, :] += _colsum(dcv)
        dcv_ref[...] = dcv

    row = pl.BlockSpec((ts, d), lambda i: (i, 0))
    return pl.pallas_call(
        body, name=name, grid=(s // ts,),
        in_specs=[row, row, _vec_spec(d), _vec_spec(d)] + [ANY_SPEC] * len(deps),
        out_specs=[row, pl.BlockSpec((8, d), lambda i: (0, 0))],
        out_shape=[jax.ShapeDtypeStruct((s, d), F32), jax.ShapeDtypeStruct((8, d), F32)],
        compiler_params=_params("arbitrary"),
    )(dv, cv, ln_g, ln_b, *deps)


def _conv_glu_bwd(name, dcv, u, dw):
    s, d = dcv.shape
    taps = dw.shape[0]
    taps8 = -(-taps // 8) * 8
    ts, halo = _pick(s, CONV_TILE, 8), CONV_HALO
    lead = halo - (taps - 1)
    per = ts // halo
    n_tiles = s // ts
    cw = min(CONV_CHUNK, d)

    def body(dcv_ref, next_ref, u_ref, prev_ref, dw_ref, du_ref, ddw_ref, sums_ref, nbuf, pbuf, nsh, psh):
        i = pl.program_id(0)

        @pl.when(i == 0)
        def _():
            ddw_ref[...] = jnp.zeros_like(ddw_ref)
            sums_ref[...] = jnp.zeros_like(sums_ref)

        nbuf[0:ts, :] = dcv_ref[...]
        nbuf[ts:, :] = jnp.where(i < n_tiles - 1, next_ref[...], 0.0)
        pbuf[0:halo, :] = jnp.where(i > 0, _glu_of(prev_ref, d), 0.0)
        pbuf[halo:, :] = _glu_of(u_ref, d)
        for c0, _ in _chunks(d):
            _shift_rows(nbuf, nsh, c0, cw, ts + halo - 8)
            _shift_rows(pbuf, psh, c0, cw, ts + halo - 8)
            dcv = dcv_ref[:, c0:c0 + cw]
            dglu = jnp.zeros((ts, cw), F32)
            for k in range(taps):
                dglu += dw_ref[k:k + 1, c0:c0 + cw] * _window(nbuf, nsh, c0, cw, taps - 1 - k, ts)
                ddw_ref[k:k + 1, c0:c0 + cw] += _colsum(dcv * _window(pbuf, psh, c0, cw, lead + k, ts))
            a = u_ref[:, c0:c0 + cw].astype(F32)
            sg = jax.nn.sigmoid(u_ref[:, d + c0:d + c0 + cw].astype(F32))
            da = dglu * sg
            dg = dglu * a * (sg * (1.0 - sg))
            du_ref[:, c0:c0 + cw] = da.astype(du_ref.dtype)
            du_ref[:, d + c0:d + c0 + cw] = dg.astype(du_ref.dtype)
            sums_ref[0:1, c0:c0 + cw] += _colsum(da)
            sums_ref[0:1, d + c0:d + c0 + cw] += _colsum(dg)

    row = pl.BlockSpec((ts, d), lambda i: (i, 0))
    wide = pl.BlockSpec((ts, 2 * d), lambda i: (i, 0))
    return pl.pallas_call(
        body, name=name, grid=(n_tiles,),
        in_specs=[row, pl.BlockSpec((halo, d), lambda i: (jnp.minimum((i + 1) * per, s // halo - 1), 0)),
                  wide, pl.BlockSpec((halo, 2 * d), lambda i: (jnp.maximum(i * per - 1, 0), 0)),
                  pl.BlockSpec((taps, d), lambda i: (0, 0))],
        out_specs=[wide, pl.BlockSpec((taps8, d), lambda i: (0, 0)), pl.BlockSpec((8, 2 * d), lambda i: (0, 0))],
        out_shape=[jax.ShapeDtypeStruct((s, 2 * d), ACT_DTYPE), jax.ShapeDtypeStruct((taps8, d), F32),
                   jax.ShapeDtypeStruct((8, 2 * d), F32)],
        scratch_shapes=[pltpu.VMEM((ts + halo, d), F32), pltpu.VMEM((ts + halo, d), F32),
                        pltpu.VMEM((7, ts + halo - 8, cw), F32), pltpu.VMEM((7, ts + halo - 8, cw), F32)],
        compiler_params=_params("arbitrary"),
    )(dcv, dcv, u, u, dw)


def _short_conv_fwd(name, z, w):
    s, d3 = z.shape
    d = d3 // 3
    taps = w.shape[0]
    ts, halo = _pick(s, CONV_TILE, 8), SHORT_HALO
    lead = halo - (taps - 1)
    per = ts // halo

    def body(z_ref, prev_ref, w_ref, q_ref, pbuf):
        i = pl.program_id(0)
        pbuf[0:halo, :] = jnp.where(i > 0, _gated_input(prev_ref, d), 0.0)
        pbuf[halo:, :] = _gated_input(z_ref, d)
        for c0, cw in _chunks(d):
            acc = jnp.zeros((ts, cw), F32)
            for k in range(taps):
                acc += w_ref[k:k + 1, c0:c0 + cw] * pbuf[lead + k:lead + k + ts, c0:c0 + cw]
            q_ref[:, c0:c0 + cw] = (z_ref[:, c0:c0 + cw].astype(F32) * acc).astype(q_ref.dtype)

    return pl.pallas_call(
        body, name=name, grid=(s // ts,),
        in_specs=[pl.BlockSpec((ts, d3), lambda i: (i, 0)),
                  pl.BlockSpec((halo, d3), lambda i: (jnp.maximum(i * per - 1, 0), 0)),
                  pl.BlockSpec((taps, d), lambda i: (0, 0))],
        out_specs=pl.BlockSpec((ts, d), lambda i: (i, 0)),
        out_shape=jax.ShapeDtypeStruct((s, d), ACT_DTYPE),
        scratch_shapes=[pltpu.VMEM((ts + halo, d), F32)],
        compiler_params=_params("parallel"),
    )(z, z, w)


def _short_conv_bwd(name, dq, z, w, deps=()):
    s, d3 = z.shape
    d = d3 // 3
    taps = w.shape[0]
    ts, halo = _pick(s, CONV_TILE, 8), SHORT_HALO
    lead = halo - (taps - 1)
    per = ts // halo
    n_tiles = s // ts

    def body(dq_ref, dqn_ref, z_ref, zp_ref, zn_ref, w_ref, *rest):
        dz_ref, sums_ref, pbuf, ubuf = rest[len(deps):]
        i = pl.program_id(0)

        @pl.when(i == 0)
        def _():
            sums_ref[...] = jnp.zeros_like(sums_ref)

        pbuf[0:halo, :] = jnp.where(i > 0, _gated_input(zp_ref, d), 0.0)
        pbuf[halo:, :] = _gated_input(z_ref, d)
        ubuf[0:ts, :] = dq_ref[...] * z_ref[:, 0:d].astype(F32)
        ubuf[ts:, :] = jnp.where(i < n_tiles - 1, dqn_ref[...] * zn_ref[:, 0:d].astype(F32), 0.0)
        for c0, cw in _chunks(d):
            du = ubuf[0:ts, c0:c0 + cw]
            conv = jnp.zeros((ts, cw), F32)
            dp = jnp.zeros((ts, cw), F32)
            for k in range(taps):
                wk = w_ref[k:k + 1, c0:c0 + cw]
                shifted = pbuf[lead + k:lead + k + ts, c0:c0 + cw]
                conv += wk * shifted
                dp += wk * ubuf[taps - 1 - k:taps - 1 - k + ts, c0:c0 + cw]
                sums_ref[k:k + 1, c0:c0 + cw] += _colsum(du * shifted)
            dz_ref[:, c0:c0 + cw] = (dq_ref[:, c0:c0 + cw] * conv).astype(dz_ref.dtype)
            dz_ref[:, d + c0:d + c0 + cw] = (dp * z_ref[:, 2 * d + c0:2 * d + c0 + cw].astype(F32)).astype(dz_ref.dtype)
            dz_ref[:, 2 * d + c0:2 * d + c0 + cw] = (dp * z_ref[:, d + c0:d + c0 + cw].astype(F32)).astype(dz_ref.dtype)

    last = s // halo - 1
    return pl.pallas_call(
        body, name=name, grid=(n_tiles,),
        in_specs=[pl.BlockSpec((ts, d), lambda i: (i, 0)),
                  pl.BlockSpec((halo, d), lambda i: (jnp.minimum((i + 1) * per, last), 0)),
                  pl.BlockSpec((ts, d3), lambda i: (i, 0)),
                  pl.BlockSpec((halo, d3), lambda i: (jnp.maximum(i * per - 1, 0), 0)),
                  pl.BlockSpec((halo, d3), lambda i: (jnp.minimum((i + 1) * per, last), 0)),
                  pl.BlockSpec((taps, d), lambda i: (0, 0))] + [ANY_SPEC] * len(deps),
        out_specs=[pl.BlockSpec((ts, d3), lambda i: (i, 0)), pl.BlockSpec((8, d), lambda i: (0, 0))],
        out_shape=[jax.ShapeDtypeStruct((s, d3), ACT_DTYPE), jax.ShapeDtypeStruct((8, d), F32)],
        scratch_shapes=[pltpu.VMEM((ts + halo, d), F32), pltpu.VMEM((ts + halo, d), F32)],
        compiler_params=_params("arbitrary"),
    )(dq, dq, z, z, z, w, *deps)


def _silu(v):
    return v * jax.nn.sigmoid(v)


def _modulation(name, c_all, mod_w, mod_b_cols):
    nl, d, n = mod_w.shape
    b = c_all.shape[0]
    tn = _pick(n, 512)

    def body(c_ref, w_ref, b_ref, o_ref):
        ca = _silu(c_ref[...]).astype(MXU_DTYPE)
        o_ref[...] = jnp.dot(ca, w_ref[...].astype(MXU_DTYPE), preferred_element_type=F32) + b_ref[...]

    return pl.pallas_call(
        body, name=name, grid=(nl, n // tn),
        in_specs=[pl.BlockSpec((b, d), lambda l, j: (0, 0)),
                  pl.BlockSpec((None, d, tn), lambda l, j: (l, 0, j)),
                  pl.BlockSpec((None, 1, tn), lambda l, j: (l, 0, j))],
        out_specs=pl.BlockSpec((None, b, tn), lambda l, j: (l, 0, j)),
        out_shape=jax.ShapeDtypeStruct((nl, b, n), F32),
        compiler_params=_params("parallel", "parallel"),
    )(c_all, mod_w, mod_b_cols.reshape(nl, 1, n))


def _adamw(g, w, m, v):
    m = ADAM_B1 * m + (1.0 - ADAM_B1) * g
    v = ADAM_B2 * v + (1.0 - ADAM_B2) * (g * g)
    m_hat = m / (1.0 - ADAM_B1 ** ADAM_STEP)
    v_hat = v / (1.0 - ADAM_B2 ** ADAM_STEP)
    delta = -ADAM_LR * (m_hat / (jnp.sqrt(v_hat) + ADAM_EPS) + ADAM_WD * w)
    return delta, m, v


def _write_update(g, w_ref, m_ref, v_ref, outs):
    delta, m, v = _adamw(g, w_ref[...], m_ref[...], v_ref[...])
    outs[0][...] = g
    outs[1][...] = delta
    outs[2][...] = m
    outs[3][...] = v


def _modulation_update(name, c_all_t, dmod, w, m, v, deps=()):
    nl, d, n = w.shape
    b = c_all_t.shape[1]
    tr = _pick(d, 256, 8)

    def body(c_ref, dm_ref, w_ref, m_ref, v_ref, *rest):
        outs = rest[len(deps):]
        ca = _silu(c_ref[...])
        dm = dm_ref[...]
        g = ca[:, 0:1] * dm[0:1, :]
        for i in range(1, b):
            g += ca[:, i:i + 1] * dm[i:i + 1, :]
        _write_update(g, w_ref, m_ref, v_ref, outs)

    blk = pl.BlockSpec((None, tr, n), lambda l, r: (l, r, 0))
    return pl.pallas_call(
        body, name=name, grid=(nl, d // tr),
        in_specs=[pl.BlockSpec((tr, b), lambda l, r: (r, 0)), pl.BlockSpec((None, b, n), lambda l, r: (l, 0, 0)),
                  blk, blk, blk] + [ANY_SPEC] * len(deps),
        out_specs=[blk] * 4, out_shape=[jax.ShapeDtypeStruct(w.shape, F32)] * 4,
        compiler_params=_params("parallel", "parallel"),
    )(c_all_t, dmod, w, m, v, *deps)


def _weight_update(name, parts, w, m, v):
    nl, rows, cols = w.shape
    tr = _pick(rows, max(8, (1 << 18) // cols), 8)

    def body(*refs):
        p_refs = refs[:nl]
        w_ref, m_ref, v_ref = refs[nl:nl + 3]
        outs = refs[nl + 3:]
        for q in range(nl):
            @pl.when(pl.program_id(0) == q)
            def _(q=q):
                g = p_refs[q][0].astype(F32)
                for k in range(1, N_CHIP):
                    g += p_refs[q][k].astype(F32)
                _write_update(g, w_ref, m_ref, v_ref, outs)

    def part_spec(q):
        return pl.BlockSpec((N_CHIP, tr, cols), lambda l, r: (0, jnp.where(l == q, r, 0), 0))

    blk = pl.BlockSpec((None, tr, cols), lambda l, r: (l, r, 0))
    return pl.pallas_call(
        body, name=name, grid=(nl, rows // tr),
        in_specs=[part_spec(q) for q in range(nl)] + [blk, blk, blk],
        out_specs=[blk] * 4, out_shape=[jax.ShapeDtypeStruct(w.shape, F32)] * 4,
        compiler_params=_params("arbitrary", "arbitrary"),
    )(*parts, w, m, v)


def _vector_update(name, gathered, w, m, v):
    _, rows, cols = gathered.shape
    rw = w.shape[0]

    def body(g_ref, w_ref, m_ref, v_ref, tot_ref, d_ref, m2_ref, v2_ref):
        tot = g_ref[0]
        for k in range(1, N_DEV):
            tot += g_ref[k]
        tot_ref[...] = tot
        delta, m2, v2 = _adamw(tot[0:rw], w_ref[...], m_ref[...], v_ref[...])
        d_ref[...] = delta
        m2_ref[...] = m2
        v2_ref[...] = v2

    vm = pl.BlockSpec(memory_space=pltpu.VMEM)
    return pl.pallas_call(
        body, name=name, in_specs=[vm] * 4, out_specs=[vm] * 4,
        out_shape=[jax.ShapeDtypeStruct((rows, cols), F32)] + [jax.ShapeDtypeStruct((rw, cols), F32)] * 3,
        compiler_params=pltpu.CompilerParams(vmem_limit_bytes=V7X_VMEM_LIMIT_BYTES),
    )(gathered, w, m, v)


def _plain_update(name, g, w, m, v):
    def body(g_ref, w_ref, m_ref, v_ref, d_ref, m2_ref, v2_ref):
        delta, m2, v2 = _adamw(g_ref[...], w_ref[...], m_ref[...], v_ref[...])
        d_ref[...] = delta
        m2_ref[...] = m2
        v2_ref[...] = v2

    vm = pl.BlockSpec(memory_space=pltpu.VMEM)
    return pl.pallas_call(
        body, name=name, in_specs=[vm] * 4, out_specs=[vm] * 3,
        out_shape=[jax.ShapeDtypeStruct(w.shape, F32)] * 3,
    )(g, w, m, v)


def _rows(a, width, mult=8):
    r = a.reshape(-1, width)
    pad = -r.shape[0] % mult
    return jnp.pad(r, ((0, pad), (0, 0))) if pad else r


def _pack(blocks, width):
    parts, spans, at = [], [], 0
    for a in blocks:
        n = a.size // width
        p = _rows(a, width)
        parts.append(p)
        spans.append((at, n))
        at += p.shape[0]
    return jnp.concatenate(parts, axis=0), spans


def kernel(x, c, mod_w, mod_b, pre_mix_g, post_mix_g, pre_ffn_g, post_ffn_g, a_w1, a_b1, a_dw, a_dwb, a_ln_g, a_ln_b, a_w2, a_b2, b_w_in, b_conv, b_w_out, f_w1, f_w2, loss_target, m_mod_w, m_mod_b, m_pre_mix_g, m_post_mix_g, m_pre_ffn_g, m_post_ffn_g, m_a_w1, m_a_b1, m_a_dw, m_a_dwb, m_a_ln_g, m_a_ln_b, m_a_w2, m_a_b2, m_b_w_in, m_b_conv, m_b_w_out, m_f_w1, m_f_w2, v_mod_w, v_mod_b, v_pre_mix_g, v_post_mix_g, v_pre_ffn_g, v_post_ffn_g, v_a_w1, v_a_b1, v_a_dw, v_a_dwb, v_a_ln_g, v_a_ln_b, v_a_w2, v_a_b2, v_b_w_in, v_b_conv, v_b_w_out, v_f_w1, v_f_w2):
    depth, d = pre_mix_g.shape
    n_a, n_b = a_w1.shape[0], b_w_in.shape[0]
    s = x.shape[1]
    dsh = d // N_DEV
    taps_a, taps_b = a_dw.shape[1], b_conv.shape[1]
    px, py, pc = _pos()
    me = 4 * px + 2 * py + pc
    x0 = x.reshape(s, d)
    target = loss_target.reshape(s, d)

    me1 = jnp.reshape(me, (1,)).astype(jnp.int32)

    def members(i):
        j = i // 2
        mix = [(a_w1, j, True), (a_w2, j, False)] if i % 2 == 0 else [(b_w_in, j, True), (b_w_out, j, False)]
        return mix, [(f_w1, i, True), (f_w2, i, False)]

    placed = {(i, q): _place("place_l%d_%d" % (i, q), w, layer, me1, by_columns)
              for i in range(depth) for q, (w, layer, by_columns) in enumerate(sum(members(i), []))}

    def start_group(tag, group, after=()):
        return _gather_start(tag, [placed[key] for key in group], after)

    packed0, spans0 = _pack([c, a_dw, b_conv], dsh)
    got0 = _small_gather("gather_cond", packed0)

    def full_width(span):
        at, n = span
        return jnp.transpose(got0[:, at:at + n, :], (1, 0, 2)).reshape(n, d)

    c_all = got0[:, spans0[0][0]:spans0[0][0] + spans0[0][1], :].reshape(N_DEV, d)
    a_dw_full = full_width(spans0[1]).reshape(n_a, taps_a, d)
    b_conv_full = full_width(spans0[2]).reshape(n_b, taps_b, d)

    n_mod = mod_w.shape[2]
    mod_b_cols = lax.dynamic_slice_in_dim(mod_b, me * n_mod, n_mod, axis=1)
    mod_local = _modulation("modulation", c_all, mod_w, mod_b_cols)
    got1 = _small_gather("gather_mod", mod_local.reshape(depth * N_DEV, n_mod))
    mod_me = lax.dynamic_index_in_dim(got1.reshape(N_DEV, depth, N_DEV, n_mod), me, axis=2, keepdims=False)
    mod_me = jnp.transpose(mod_me, (1, 0, 2)).reshape(depth, 6, 1, d)

    def vec(a, i):
        return a[i].reshape(1, -1)

    def pass_groups(tag, handles, after):
        for q in range(len(handles)):
            handles[q] = _gather_pass("%s%d" % (tag, q), handles[q], after)
            after = handles[q]["token"]
        return after

    def wait_groups(tag, handles, after):
        return sum([_gather_wait("%s%d" % (tag, q), hq, after) for q, hq in enumerate(handles)], [])

    mix_copy = [start_group("l0m", [(0, 0)], (got1,))]
    mix_late = [start_group("l0n", [(0, 1)], (mix_copy[0]["token"],))]
    ffn_copy = [start_group("l0u", [(0, 2)], (mix_late[0]["token"],))]
    ffn_copy.append(start_group("l0d", [(0, 3)], (ffn_copy[0]["token"],)))
    saved = []
    xs = x0
    h = _residual_fwd("fwd_in", xs, pre=(vec(pre_mix_g, 0), mod_me[0, 1], mod_me[0, 0]),
                      deps=[hq["token"] for hq in ffn_copy] + [buf for (i, _), buf in placed.items() if i > 0])[0]
    w_mix = wait_groups("l0m", mix_copy, pass_groups("l0m", mix_copy, h))
    for i in range(depth):
        j = i // 2
        sh_m, sc_m, gt_m, sh_f, sc_f, gt_f = [mod_me[i, q] for q in range(6)]
        keep = {"x_mix": xs, "h_mix": h}
        early = i > 0
        w1g, w2g = w_mix if early else (w_mix[0], None)
        if i % 2 == 0:
            u = _mm_cols("a_w1", h, w1g, F32, _store_bias, bias=vec(a_b1, j))[0]
            deps = [pass_groups("l%df" % i, ffn_copy, u)] if early else []
            if not early:
                w2g, = wait_groups("l0n", mix_late, pass_groups("l0n", mix_late, u))
            cv, v = _glu_conv_ln_swish("a_conv", u, a_dw_full[j], vec(a_dwb, j), vec(a_ln_g, j), vec(a_ln_b, j))
            y = _mm_rows("a_w2", v, w2g, F32, bias=vec(a_b2, j), deps=deps)
            keep.update(u=u, cv=cv, v=v)
        else:
            z = _mm_cols("b_w_in", h, w1g, F32)[0]
            deps = [pass_groups("l%df" % i, ffn_copy, z)] if early else []
            q = _short_conv_fwd("b_conv", z, b_conv_full[j])
            y = _mm_rows("b_w_out", q, w2g, F32, deps=deps)
            keep.update(z=z, q=q)
        after = y if early else pass_groups("l%df" % i, ffn_copy, y)
        deps = []
        if i + 1 < depth:
            mix_copy = [start_group("l%dm" % (i + 1), [(i + 1, 0), (i + 1, 1)])]
            ffn_next = [start_group("l%df" % (i + 1), [(i + 1, 2), (i + 1, 3)])]
            deps = [mix_copy[0]["token"], ffn_next[0]["token"]]
        fw1g, fw2g = wait_groups("l%df" % i, ffn_copy, after)
        keep.update(w1g=w1g, w2g=w2g, fw1g=fw1g, fw2g=fw2g, y_mix=y)
        xs, h = _residual_fwd("fwd_mid", xs, post=(y, vec(post_mix_g, i), gt_m),
                              pre=(vec(pre_ffn_g, i), sc_f, sh_f))
        keep.update(x_ffn=xs, h_ffn=h)
        act, hid = _mm_cols("f_w1", h, fw1g, ACT_DTYPE, _store_relu2, n_out=2, deps=deps)
        deps = [pass_groups("l%dm" % (i + 1), mix_copy, act)] if i + 1 < depth else []
        y = _mm_rows("f_w2", act, fw2g, F32, deps=deps)
        keep.update(act=act, hid=hid, y_ffn=y)
        saved.append(keep)
        if i + 1 < depth:
            w_mix = wait_groups("l%dm" % (i + 1), mix_copy, y)
            ffn_copy = ffn_next
            nxt = mod_me[i + 1]
            xs, h = _residual_fwd("fwd_next", xs, post=(y, vec(post_ffn_g, i), gt_f),
                                  pre=(vec(pre_mix_g, i + 1), nxt[1], nxt[0]))
        else:
            g, dy, loss_blk, sums = _loss_junction("loss_junction", xs, y, vec(post_ffn_g, i), gt_f, target)

    where = jnp.stack([pc, 2 * px + py]).astype(jnp.int32)
    land = {"a_w1": [None] * n_a, "a_w2": [None] * n_a, "b_w_in": [None] * n_b, "b_w_out": [None] * n_b,
            "f_w1": [None] * depth, "f_w2": [None] * depth}
    reductions = []

    def reduce_group(tag, slots, swap, after):
        grads, theirs = _sibling_wait(tag, swap, after)
        pairs = [_chip_partial("chip_partial_%s_%d" % (tag, q), gq, tq, where)
                 for q, (gq, tq) in enumerate(zip(grads, theirs))]
        handle = _reduce_start(tag, [p for p, _ in pairs], [l for _, l in pairs])
        reductions.append((tag, slots, handle))
        return handle["token"]

    zero_vec = jnp.zeros((d,), F32)
    dmod = [[zero_vec] * 6 for _ in range(depth)]
    small = {name: [None] * depth for name in ("pre_mix_g", "post_mix_g", "pre_ffn_g", "post_ffn_g")}
    small_a = {name: [None] * n_a for name in ("a_b1", "a_dwb", "a_ln_g", "a_ln_b", "a_b2", "a_dw")}
    small_b = {"b_conv": [None] * n_b}

    deps = []
    for i in reversed(range(depth)):
        j = i // 2
        kp = saved[i]
        sh_m, sc_m, gt_m, sh_f, sc_f, gt_f = [mod_me[i, q] for q in range(6)]
        dmod[i][5] = sums[3]
        small["post_ffn_g"][i] = sums[4]
        dhid = _mm_rows_t("f_w2_t", dy, kp["fw2g"], ACT_DTYPE, _store_relu2_grad, extra=kp["hid"], deps=deps)
        deps = []
        if i + 1 < depth:
            deps = [reduce_group("r%dm" % (i + 1), last_swap[0], last_swap[1], dhid)]
        g_fw2 = _grad_rows("f_w2_grad", kp["act"], dy, f_w2.shape[1])
        dh = _mm_cols_t("f_w1_t", dhid, kp["fw1g"], F32, deps=deps)
        g_fw1 = _grad_cols("f_w1_grad", kp["h_ffn"], dhid, f_w1.shape[2])
        ffn_swap = _sibling_start("r%df" % i, [g_fw1, g_fw2])
        deps = [ffn_swap["token"]]
        g, dy, sums = _residual_bwd("bwd_mid", g, pre=(dh, kp["x_ffn"], vec(pre_ffn_g, i), sc_f),
                                    post=(kp["y_mix"], vec(post_mix_g, i), gt_m))
        dmod[i][3], dmod[i][4], dmod[i][2] = sums[0], sums[1], sums[3]
        small["pre_ffn_g"][i], small["post_mix_g"][i] = sums[2], sums[4]
        if i % 2 == 0:
            small_a["a_b2"][j] = sums[5]
            dv = _mm_rows_t("a_w2_t", dy, kp["w2g"], F32, deps=deps)
            g_w2 = _grad_rows("a_w2_grad", kp["v"], dy, a_w2.shape[1])
            deps = [reduce_group("r%df" % i, (("f_w1", i), ("f_w2", i)), ffn_swap, g_w2)]
            dcv, lsum = _ln_swish_bwd("a_ln_bwd", dv, kp["cv"], vec(a_ln_g, j), vec(a_ln_b, j), deps=deps)
            small_a["a_ln_g"][j], small_a["a_ln_b"][j], small_a["a_dwb"][j] = lsum[0], lsum[1], lsum[2]
            du, ddw, usum = _conv_glu_bwd("a_conv_bwd", dcv, kp["u"], a_dw_full[j])
            small_a["a_dw"][j], small_a["a_b1"][j] = ddw[:taps_a], usum[0]
            dh = _mm_cols_t("a_w1_t", du, kp["w1g"], F32)
            g_w1 = _grad_cols("a_w1_grad", kp["h_mix"], du, a_w1.shape[2])
            names = ("a_w1", "a_w2")
        else:
            dq = _mm_rows_t("b_w_out_t", dy, kp["w2g"], F32, deps=deps)
            g_w2 = _grad_rows("b_w_out_grad", kp["q"], dy, b_w_out.shape[1])
            deps = [reduce_group("r%df" % i, (("f_w1", i), ("f_w2", i)), ffn_swap, g_w2)]
            dz, wsum = _short_conv_bwd("b_conv_bwd", dq, kp["z"], b_conv_full[j], deps=deps)
            small_b["b_conv"][j] = wsum[:taps_b]
            dh = _mm_cols_t("b_w_in_t", dz, kp["w1g"], F32)
            g_w1 = _grad_cols("b_w_in_grad", kp["h_mix"], dz, b_w_in.shape[2])
            names = ("b_w_in", "b_w_out")
        mix_swap = _sibling_start("r%dm" % i, [g_w1, g_w2])
        deps = [mix_swap["token"]]
        if i > 0:
            prev = saved[i - 1]
            g, dy, sums = _residual_bwd("bwd_next", g, pre=(dh, kp["x_mix"], vec(pre_mix_g, i), sc_m),
                                        post=(prev["y_ffn"], vec(post_ffn_g, i - 1), mod_me[i - 1, 5]))
        else:
            g, sums = _residual_bwd("bwd_in", g, pre=(dh, kp["x_mix"], vec(pre_mix_g, i), sc_m))
        dmod[i][0], dmod[i][1] = sums[0], sums[1]
        small["pre_mix_g"][i] = sums[2]
        last_swap = (((names[0], j), (names[1], j)), mix_swap)
    grad_x = g.reshape(x.shape)

    reps = [jnp.stack([jnp.stack(r) for r in dmod]),
            jnp.stack(small["pre_mix_g"]), jnp.stack(small["post_mix_g"]),
            jnp.stack(small["pre_ffn_g"]), jnp.stack(small["post_ffn_g"]),
            jnp.stack(small_a["a_b1"]), jnp.stack(small_a["a_dwb"]), jnp.stack(small_a["a_ln_g"]),
            jnp.stack(small_a["a_ln_b"]), jnp.stack(small_a["a_b2"])]
    rep_w = [mod_b, pre_mix_g, post_mix_g, pre_ffn_g, post_ffn_g, a_b1, a_dwb, a_ln_g, a_ln_b, a_b2]
    rep_m = [m_mod_b, m_pre_mix_g, m_post_mix_g, m_pre_ffn_g, m_post_ffn_g, m_a_b1, m_a_dwb, m_a_ln_g, m_a_ln_b, m_a_b2]
    rep_v = [v_mod_b, v_pre_mix_g, v_post_mix_g, v_pre_ffn_g, v_post_ffn_g, v_a_b1, v_a_dwb, v_a_ln_g, v_a_ln_b, v_a_b2]
    w_small, spans = _pack(rep_w, d)
    m_small, _ = _pack(rep_m, d)
    v_small, _ = _pack(rep_v, d)
    loss_row = jnp.pad(loss_blk[0:1, 0:1], ((0, 0), (0, d - 1)))
    part_small, spans_g = _pack(reps + [jnp.stack(small_a["a_dw"]), jnp.stack(small_b["b_conv"]), loss_row], d)
    small_copy = _gather_start("small", [_place("place_small", part_small[None], 0, me1, False, F32)],
                               (last_swap[1]["token"],))
    after = reduce_group("r0m", last_swap[0], last_swap[1], small_copy["token"])
    big = {"f_w1": (f_w1, m_f_w1, v_f_w1), "f_w2": (f_w2, m_f_w2, v_f_w2),
           "b_w_in": (b_w_in, m_b_w_in, v_b_w_in), "b_w_out": (b_w_out, m_b_w_out, v_b_w_out),
           "a_w1": (a_w1, m_a_w1, v_a_w1), "a_w2": (a_w2, m_a_w2, v_a_w2)}
    big_out = {}

    def update_complete():
        nonlocal after
        for k, wmv in big.items():
            if k not in big_out and all(b is not None for b in land[k]):
                big_out[k] = _weight_update("update_" + k, land[k], *wmv)
                after = big_out[k][3]

    def wait_reduction(tag, slots, handle):
        nonlocal after
        landed = _reduce_wait(tag, handle, after)
        after = landed[0]
        for (key, idx), buf in zip(slots, landed):
            land[key][idx] = buf

    for entry in reductions[:-1]:
        wait_reduction(*entry)
    update_complete()
    small_copy = _gather_pass("small", small_copy, after)
    got2, = _gather_wait("small", small_copy, small_copy["token"])
    total, d_small, m2_small, v2_small = _vector_update("vector_update", got2, w_small, m_small, v_small)

    def unpack(buf, span, like):
        return buf[span[0]:span[0] + span[1]].reshape(like.shape)

    rep_out = [[unpack(buf, sp, w) for sp, w in zip(spans, rep_w)] for buf in (total, d_small, m2_small, v2_small)]
    loss = total[spans_g[-1][0], 0]

    conv_g = [lax.dynamic_slice_in_dim(total[sp[0]:sp[0] + sp[1]], me * dsh, dsh, axis=1)
              for sp in spans_g[len(reps):len(reps) + 2]]
    gc, _ = _pack(conv_g, dsh)
    wc, spans_c = _pack([a_dw, b_conv], dsh)
    mc, _ = _pack([m_a_dw, m_b_conv], dsh)
    vc, _ = _pack([v_a_dw, v_b_conv], dsh)
    conv_out = [gc] + list(_plain_update("conv_update", gc, wc, mc, vc))
    conv_out = [[unpack(buf, sp, w) for sp, w in zip(spans_c, (a_dw, b_conv))] for buf in conv_out]

    at, n = spans_g[0]
    dmod_all = got2[:, at:at + n, :].reshape(N_DEV, depth, 6 * d)
    dmod_cols = jnp.transpose(lax.dynamic_slice_in_dim(dmod_all, me * n_mod, n_mod, axis=2), (1, 0, 2))
    mod_out = _modulation_update("modulation_update", jnp.transpose(c_all), dmod_cols, mod_w, m_mod_w, v_mod_w)
    after = mod_out[3]
    wait_reduction(*reductions[-1])
    update_complete()

    def family(q):
        rep = dict(zip(("mod_b", "pre_mix_g", "post_mix_g", "pre_ffn_g", "post_ffn_g",
                        "a_b1", "a_dwb", "a_ln_g", "a_ln_b", "a_b2"), rep_out[q]))
        return (mod_out[q], rep["mod_b"], rep["pre_mix_g"], rep["post_mix_g"], rep["pre_ffn_g"], rep["post_ffn_g"],
                big_out["a_w1"][q], rep["a_b1"], conv_out[q][0], rep["a_dwb"], rep["a_ln_g"], rep["a_ln_b"],
                big_out["a_w2"][q], rep["a_b2"], big_out["b_w_in"][q], conv_out[q][1], big_out["b_w_out"][q],
                big_out["f_w1"][q], big_out["f_w2"][q])

    return (loss, grad_x, *family(0), *family(1), *family(2), *family(3))
```

```python
import jax
import jax.numpy as jnp
from jax import lax
from jax.experimental import pallas as pl
from jax.experimental.pallas import tpu as pltpu

MXU_DTYPE = jnp.bfloat16
ACT_DTYPE = jnp.bfloat16
COMM_DTYPE = jnp.bfloat16

N_DEV = 8
N_CHIP = 4
RMS_EPS = 1e-6
LN_EPS = 1e-5
ADAM_LR = 0.001
ADAM_B1 = 0.9
ADAM_B2 = 0.999
ADAM_EPS = 1e-08
ADAM_WD = 0.01
ADAM_STEP = 10

V7X_VMEM_LIMIT_BYTES = 56 * 1024 * 1024
LANE = 128
ROW_TILE = 256
CONV_TILE = 128
CONV_HALO = 32
SHORT_HALO = 8
CONV_CHUNK = 256
MM_TM, MM_TN, MM_TK = 1024, 1024, 2048
MM_RESIDENT = 4 * 1024 * 1024

F32 = jnp.float32
MESH = pl.DeviceIdType.MESH


def _pick(dim, pref, mult=LANE):
    if dim <= pref:
        return dim
    t = pref - pref % mult
    while dim % t:
        t -= mult
    return t


def _params(*sem):
    return pltpu.CompilerParams(dimension_semantics=sem, vmem_limit_bytes=V7X_VMEM_LIMIT_BYTES)


def _pos():
    return lax.axis_index("x"), lax.axis_index("y"), lax.axis_index("c")


def _flip(v, bit):
    return 1 - v if bit else v


def _small_gather(name, v, deps=()):
    rows, cols = v.shape
    n_deps = len(deps)

    def body(v_ref, *rest):
        out_ref, send_sems, recv_sems, local_sem = rest[n_deps:]
        x, y, c = _pos()
        me, sibling = (x, y, c), (x, y, 1 - c)
        chips = [(1 - x, y), (x, 1 - y), (1 - x, 1 - y)]

        def block(px, py, pc):
            return out_ref.at[4 * px + 2 * py + pc]

        def copy(k, owner, to, src=None):
            return pltpu.make_async_remote_copy(
                src_ref=block(*owner) if src is None else src, dst_ref=block(*owner), send_sem=send_sems.at[k],
                recv_sem=recv_sems.at[k], device_id=to, device_id_type=MESH)

        mine = pltpu.make_async_copy(v_ref, block(*me), local_sem)
        mine.start()
        first = [copy(0, me, sibling, src=v_ref)]
        first += [copy(1 + j, me, (*chip, c), src=v_ref) for j, chip in enumerate(chips)]
        for cp in first:
            cp.start()
        passed = [copy(4 + j, (*chip, c), sibling) for j, chip in enumerate(chips)]
        for j, chip in enumerate(chips):
            copy(1 + j, (*chip, c), me).wait_recv()
            passed[j].start()
        copy(0, sibling, me).wait_recv()
        for j, chip in enumerate(chips):
            copy(4 + j, (*chip, 1 - c), me).wait_recv()
        for cp in first + passed:
            cp.wait_send()
        mine.wait()

    return pl.pallas_call(
        body, name=name,
        out_shape=jax.ShapeDtypeStruct((N_DEV, rows, cols), v.dtype),
        in_specs=[pl.BlockSpec(memory_space=pltpu.VMEM)] + [pl.BlockSpec(memory_space=pl.ANY)] * n_deps,
        out_specs=pl.BlockSpec(memory_space=pltpu.VMEM),
        scratch_shapes=[pltpu.SemaphoreType.DMA((N_DEV - 1,)), pltpu.SemaphoreType.DMA((N_DEV - 1,)),
                        pltpu.SemaphoreType.DMA],
        compiler_params=pltpu.CompilerParams(vmem_limit_bytes=V7X_VMEM_LIMIT_BYTES),
    )(v, *deps)


HBM_SPEC = pl.BlockSpec(memory_space=pltpu.HBM)
SEM_SPEC = pl.BlockSpec(memory_space=pltpu.SEMAPHORE)
ANY_SPEC = pl.BlockSpec(memory_space=pl.ANY)
TOKEN = jax.ShapeDtypeStruct((8, LANE), F32)
TOKEN_SPEC = pl.BlockSpec(memory_space=pltpu.VMEM)
SPLIT_COPY = pltpu.CompilerParams(has_side_effects=pltpu.SideEffectType.DATAFLOW_SIDE_EFFECTING)


def _hbm(a):
    return pltpu.with_memory_space_constraint(a, pltpu.HBM)


def _dma_sems(n):
    return pltpu.SemaphoreType.DMA((n,))


def _place(name, w, layer, me, by_columns, dtype=None):
    dtype = COMM_DTYPE if dtype is None else dtype
    _, rows, cols = w.shape
    tr = _pick(rows, max(8, (1 << 20) // cols), 8)

    def body(me_ref, w_ref, o_ref):
        o_ref[...] = w_ref[...].astype(o_ref.dtype)

    if by_columns:
        out_spec = pl.BlockSpec((tr, cols), lambda r, me_ref: (r, me_ref[0]))
        out_shape = jax.ShapeDtypeStruct((rows, N_DEV * cols), dtype)
    else:
        out_spec = pl.BlockSpec((None, tr, cols), lambda r, me_ref: (me_ref[0], r, 0))
        out_shape = jax.ShapeDtypeStruct((N_DEV, rows, cols), dtype)
    return pl.pallas_call(
        body, name=name,
        grid_spec=pltpu.PrefetchScalarGridSpec(
            num_scalar_prefetch=1, grid=(rows // tr,),
            in_specs=[pl.BlockSpec((None, tr, cols), lambda r, me_ref: (layer, r, 0))],
            out_specs=out_spec),
        out_shape=out_shape, compiler_params=_params("parallel"),
    )(me, w)


def _block_copy(buf, owner, to, send_sem, recv_sem):
    dev = 4 * owner[0] + 2 * owner[1] + owner[2]
    if len(buf.shape) == 3:
        blk = buf.at[dev]
    else:
        n = buf.shape[1] // N_DEV
        blk = buf.at[:, pl.ds(pl.multiple_of(dev * n, n), n)]
    return pltpu.make_async_remote_copy(src_ref=blk, dst_ref=blk, send_sem=send_sem, recv_sem=recv_sem,
                                        device_id=to, device_id_type=MESH)


def _other_chips(x, y):
    return [(1 - x, y), (x, 1 - y), (1 - x, 1 - y)]


def _gather_start(tag, bufs, after=()):
    n = len(bufs)
    n_in = n + len(after)

    def body(*refs):
        b = refs[:n]
        send, recv_ici, recv_sib = refs[n_in:n_in + 3]
        token = refs[-1]
        x, y, c = _pos()
        me = (x, y, c)
        for i in range(n):
            _block_copy(b[i], me, (x, y, 1 - c), send.at[4 * i], recv_sib.at[i]).start()
            for j, chip in enumerate(_other_chips(x, y)):
                _block_copy(b[i], me, (*chip, c), send.at[4 * i + 1 + j], recv_ici.at[3 * i + j]).start()
        token[...] = jnp.zeros_like(token)

    outs = pl.pallas_call(
        body, name="gather_start_" + tag,
        out_shape=(_dma_sems(4 * n), _dma_sems(3 * n), _dma_sems(n),
                   *[pltpu.HBM(b.shape, b.dtype) for b in bufs], TOKEN),
        in_specs=[HBM_SPEC] * n + [ANY_SPEC] * len(after),
        out_specs=(SEM_SPEC, SEM_SPEC, SEM_SPEC, *[HBM_SPEC] * n, TOKEN_SPEC),
        input_output_aliases={i: 3 + i for i in range(n)}, compiler_params=SPLIT_COPY,
    )(*[_hbm(b) for b in bufs], *after)
    return dict(send=outs[0], recv_ici=outs[1], recv_sib=outs[2], bufs=list(outs[3:3 + n]), token=outs[-1])


def _gather_pass(tag, h, after):
    bufs = h["bufs"]
    n = len(bufs)

    def body(*refs):
        b = refs[:n]
        recv_ici = refs[n]
        fsend, frecv = refs[n + 2:n + 4]
        token = refs[-1]
        x, y, c = _pos()
        for i in range(n):
            for j, chip in enumerate(_other_chips(x, y)):
                _block_copy(b[i], (*chip, c), (x, y, c), fsend.at[3 * i + j], recv_ici.at[3 * i + j]).wait_recv()
                _block_copy(b[i], (*chip, c), (x, y, 1 - c), fsend.at[3 * i + j], frecv.at[3 * i + j]).start()
        token[...] = jnp.zeros_like(token)

    outs = pl.pallas_call(
        body, name="gather_pass_" + tag,
        out_shape=(_dma_sems(3 * n), _dma_sems(3 * n), *[pltpu.HBM(b.shape, b.dtype) for b in bufs], TOKEN),
        in_specs=[HBM_SPEC] * n + [SEM_SPEC, ANY_SPEC],
        out_specs=(SEM_SPEC, SEM_SPEC, *[HBM_SPEC] * n, TOKEN_SPEC),
        input_output_aliases={i: 2 + i for i in range(n)}, compiler_params=SPLIT_COPY,
    )(*bufs, h["recv_ici"], after)
    return dict(h, fsend=outs[0], frecv=outs[1], bufs=list(outs[2:2 + n]), token=outs[-1])


def _gather_wait(tag, h, after):
    bufs = h["bufs"]
    n = len(bufs)

    def body(*refs):
        b = refs[:n]
        send, recv_sib, fsend, frecv = refs[n:n + 4]
        x, y, c = _pos()
        me, sibling = (x, y, c), (x, y, 1 - c)
        for i in range(n):
            _block_copy(b[i], sibling, me, send.at[4 * i], recv_sib.at[i]).wait_recv()
            for j, chip in enumerate(_other_chips(x, y)):
                _block_copy(b[i], (*chip, 1 - c), me, fsend.at[3 * i + j], frecv.at[3 * i + j]).wait_recv()
            for k in range(4):
                _block_copy(b[i], me, sibling, send.at[4 * i + k], recv_sib.at[i]).wait_send()
            for j, chip in enumerate(_other_chips(x, y)):
                _block_copy(b[i], (*chip, c), sibling, fsend.at[3 * i + j], frecv.at[3 * i + j]).wait_send()

    outs = pl.pallas_call(
        body, name="gather_wait_" + tag,
        out_shape=tuple(pltpu.HBM(b.shape, b.dtype) for b in bufs),
        in_specs=[HBM_SPEC] * n + [SEM_SPEC] * 4 + [ANY_SPEC], out_specs=tuple([HBM_SPEC] * n),
        input_output_aliases={i: i for i in range(n)}, compiler_params=SPLIT_COPY,
    )(*bufs, h["send"], h["recv_sib"], h["fsend"], h["frecv"], after)
    return list(outs)


def _sibling_copy(grad, land, send_sem, recv_sem):
    x, y, c = _pos()
    return pltpu.make_async_remote_copy(
        src_ref=grad.at[:, 1 - c], dst_ref=land, send_sem=send_sem, recv_sem=recv_sem,
        device_id=(x, y, 1 - c), device_id_type=MESH)


def _sibling_start(tag, grads):
    n = len(grads)
    landings = [lax.empty((N_CHIP,) + g.shape[2:], g.dtype) for g in grads]

    def body(*refs):
        g, land = refs[:n], refs[n:2 * n]
        send, recv = refs[2 * n:2 * n + 2]
        token = refs[-1]
        for i in range(n):
            _sibling_copy(g[i], land[i], send.at[i], recv.at[i]).start()
        token[...] = jnp.zeros_like(token)

    both = list(grads) + landings
    outs = pl.pallas_call(
        body, name="sibling_start_" + tag,
        out_shape=(_dma_sems(n), _dma_sems(n), *[pltpu.HBM(b.shape, b.dtype) for b in both], TOKEN),
        in_specs=[HBM_SPEC] * (2 * n), out_specs=(SEM_SPEC, SEM_SPEC, *[HBM_SPEC] * (2 * n), TOKEN_SPEC),
        input_output_aliases={i: 2 + i for i in range(2 * n)}, compiler_params=SPLIT_COPY,
    )(*[_hbm(b) for b in both])
    return dict(send=outs[0], recv=outs[1], grads=list(outs[2:2 + n]), landings=list(outs[2 + n:2 + 2 * n]),
                token=outs[-1])


def _sibling_wait(tag, h, after):
    n = len(h["grads"])

    def body(*refs):
        g, land = refs[:n], refs[n:2 * n]
        send, recv = refs[2 * n:2 * n + 2]
        for i in range(n):
            _sibling_copy(g[i], land[i], send.at[i], recv.at[i]).wait()

    both = h["grads"] + h["landings"]
    outs = pl.pallas_call(
        body, name="sibling_wait_" + tag,
        out_shape=tuple(pltpu.HBM(b.shape, b.dtype) for b in both),
        in_specs=[HBM_SPEC] * (2 * n) + [SEM_SPEC, SEM_SPEC, ANY_SPEC], out_specs=tuple([HBM_SPEC] * (2 * n)),
        input_output_aliases={i: i for i in range(2 * n)}, compiler_params=SPLIT_COPY,
    )(*both, h["send"], h["recv"], after)
    return list(outs[:n]), list(outs[n:])


def _reduce_copy(part, land, x, y, c, k, src_chip, send_sem, recv_sem):
    px, py = _flip(x, k >> 1 & 1), _flip(y, k & 1)
    return pltpu.make_async_remote_copy(src_ref=part.at[2 * px + py], dst_ref=land.at[src_chip], send_sem=send_sem,
                                        recv_sem=recv_sem, device_id=(px, py, c), device_id_type=MESH)


def _reduce_start(tag, partials, landings):
    n = len(partials)

    def body(*refs):
        p, land = refs[:n], refs[n:2 * n]
        send, recv = refs[2 * n:2 * n + 2]
        token = refs[-1]
        x, y, c = _pos()
        for i in range(n):
            for k in range(1, N_CHIP):
                _reduce_copy(p[i], land[i], x, y, c, k, 2 * x + y, send.at[3 * i + k - 1], recv.at[3 * i + k - 1]).start()
        token[...] = jnp.zeros_like(token)

    both = list(partials) + list(landings)
    outs = pl.pallas_call(
        body, name="reduce_start_" + tag,
        out_shape=(_dma_sems(3 * n), _dma_sems(3 * n), *[pltpu.HBM(b.shape, b.dtype) for b in both], TOKEN),
        in_specs=[HBM_SPEC] * (2 * n), out_specs=(SEM_SPEC, SEM_SPEC, *[HBM_SPEC] * (2 * n), TOKEN_SPEC),
        input_output_aliases={i: 2 + i for i in range(2 * n)}, compiler_params=SPLIT_COPY,
    )(*[_hbm(b) for b in both])
    return dict(send=outs[0], recv=outs[1], partials=list(outs[2:2 + n]), landings=list(outs[2 + n:2 + 2 * n]),
                token=outs[-1])


def _reduce_wait(tag, h, after):
    n = len(h["partials"])

    def body(*refs):
        p, land = refs[:n], refs[n:2 * n]
        send, recv = refs[2 * n:2 * n + 2]
        x, y, c = _pos()
        for i in range(n):
            for k in range(1, N_CHIP):
                src_chip = 2 * _flip(x, k >> 1 & 1) + _flip(y, k & 1)
                cp = _reduce_copy(p[i], land[i], x, y, c, k, src_chip, send.at[3 * i + k - 1], recv.at[3 * i + k - 1])
                cp.wait_recv()
                cp.wait_send()

    both = h["partials"] + h["landings"]
    outs = pl.pallas_call(
        body, name="reduce_wait_" + tag,
        out_shape=tuple(pltpu.HBM(b.shape, b.dtype) for b in both),
        in_specs=[HBM_SPEC] * (2 * n) + [SEM_SPEC, SEM_SPEC, ANY_SPEC], out_specs=tuple([HBM_SPEC] * (2 * n)),
        input_output_aliases={i: i for i in range(2 * n)}, compiler_params=SPLIT_COPY,
    )(*both, h["send"], h["recv"], after)
    return list(outs[n:])


def _chip_partial(name, grad, theirs, where):
    _, _, rows, cols = grad.shape
    tr = _pick(rows, max(8, (1 << 20) // cols), 8)
    landing = lax.empty(theirs.shape, theirs.dtype)

    def body(where_ref, g_ref, t_ref, land_in, o_ref, land_ref):
        total = (g_ref[...].astype(F32) + t_ref[...].astype(F32)).astype(o_ref.dtype)
        o_ref[...] = total

        @pl.when(pl.program_id(1) == where_ref[1])
        def _():
            land_ref[...] = total

    return pl.pallas_call(
        body, name=name,
        grid_spec=pltpu.PrefetchScalarGridSpec(
            num_scalar_prefetch=1, grid=(rows // tr, N_CHIP),
            in_specs=[pl.BlockSpec((None, None, tr, cols), lambda r, k, w: (k, w[0], r, 0)),
                      pl.BlockSpec((None, tr, cols), lambda r, k, w: (k, r, 0)), ANY_SPEC],
            out_specs=[pl.BlockSpec((None, tr, cols), lambda r, k, w: (k, r, 0)),
                       pl.BlockSpec((None, tr, cols), lambda r, k, w: (w[1], r, 0))]),
        out_shape=[jax.ShapeDtypeStruct(theirs.shape, theirs.dtype)] * 2,
        input_output_aliases={3: 1},
        compiler_params=_params("parallel", "arbitrary"),
    )(where, grad, theirs, landing)


def _matmul(name, a, b, a_spec, b_spec, grid, acc_shape, out_shape, out_specs, epilogue,
            ta=False, tb=False, extras=(), extra_specs=(), deps=()):
    nk = grid[2]
    n_extra = len(extras)
    n_in = 2 + n_extra + len(deps)
    dims = (((0,) if ta else (1,), (1,) if tb else (0,)), ((), ()))

    def body(*refs):
        a_ref, b_ref = refs[:2]
        extra_refs = refs[2:2 + n_extra]
        out_refs = refs[n_in:n_in + len(out_shape)]

        def product():
            return lax.dot_general(a_ref[...].astype(MXU_DTYPE), b_ref[...].astype(MXU_DTYPE), dims,
                                   preferred_element_type=F32)

        if nk == 1:
            epilogue(product(), extra_refs, out_refs)
            return
        acc_ref = refs[-1]
        k = pl.program_id(2)

        @pl.when(k == 0)
        def _():
            acc_ref[...] = product()

        if nk > 2:
            @pl.when((k > 0) & (k < nk - 1))
            def _():
                acc_ref[...] += product()

        @pl.when(k == nk - 1)
        def _():
            epilogue(acc_ref[...] + product(), extra_refs, out_refs)

    return pl.pallas_call(
        body, name=name, grid=grid,
        in_specs=[a_spec, b_spec, *extra_specs, *[ANY_SPEC] * len(deps)], out_specs=out_specs, out_shape=out_shape,
        scratch_shapes=[pltpu.VMEM(acc_shape, F32)] if nk > 1 else [],
        compiler_params=_params("parallel", "parallel", "arbitrary"),
    )(a, b, *extras, *deps)


def _store(acc, extra_refs, out_refs):
    out_refs[0][...] = acc.astype(out_refs[0].dtype)


def _store_bias(acc, extra_refs, out_refs):
    out_refs[0][...] = (acc + extra_refs[0][...]).astype(out_refs[0].dtype)


def _store_relu2(acc, extra_refs, out_refs):
    r = jnp.maximum(acc, 0.0)
    out_refs[0][...] = (r * r).astype(out_refs[0].dtype)
    out_refs[1][...] = acc.astype(out_refs[1].dtype)


def _store_relu2_grad(acc, extra_refs, out_refs):
    hid = extra_refs[0][...].astype(F32)
    out_refs[0][...] = (acc * (2.0 * jnp.maximum(hid, 0.0))).astype(out_refs[0].dtype)


def _mm_cols(name, a, wc, out_dtype, epilogue=_store, bias=None, n_out=1, deps=()):
    s, kdim = a.shape
    n = wc.shape[1]
    tm, tn, tk = _pick(s, MM_TM), _pick(n, MM_TN), _pick(kdim, MM_TK)
    extras, extra_specs = (), ()
    if bias is not None:
        extras, extra_specs = (bias,), (pl.BlockSpec((1, tn), lambda i, j, k: (0, j)),)
    out = jax.ShapeDtypeStruct((s, n), out_dtype)
    spec = pl.BlockSpec((tm, tn), lambda i, j, k: (i, j))
    return _matmul(
        name, a, wc,
        pl.BlockSpec((tm, tk), lambda i, j, k: (i, k)),
        pl.BlockSpec((tk, tn), lambda i, j, k: (k, j)),
        (s // tm, n // tn, kdim // tk), (tm, tn),
        [out] * n_out, [spec] * n_out, epilogue, extras=extras, extra_specs=extra_specs, deps=deps)


def _mm_rows(name, a, wr, out_dtype, bias=None, deps=()):
    s, kdim = a.shape
    w = wr.reshape(kdim, wr.shape[2])
    n = w.shape[1]
    tm, tn, tk = _pick(s, MM_TM), _pick(n, MM_TN), _pick(kdim, MM_TK)
    if kdim * n <= MM_RESIDENT:
        tm, tn = _pick(s, MM_TM // 2), n
    extras, extra_specs, epilogue = (), (), _store
    if bias is not None:
        extras, extra_specs, epilogue = (bias,), (pl.BlockSpec((1, tn), lambda i, j, k: (0, j)),), _store_bias
    return _matmul(
        name, a, w,
        pl.BlockSpec((tm, tk), lambda i, j, k: (i, k)),
        pl.BlockSpec((tk, tn), lambda i, j, k: (k, j)),
        (s // tm, n // tn, kdim // tk), (tm, tn),
        [jax.ShapeDtypeStruct((s, n), out_dtype)], [pl.BlockSpec((tm, tn), lambda i, j, k: (i, j))],
        epilogue, extras=extras, extra_specs=extra_specs, deps=deps)[0]


def _mm_cols_t(name, dy, wc, out_dtype, deps=()):
    s, n = dy.shape
    kdim = wc.shape[0]
    tm, tn, tk = _pick(s, MM_TM), _pick(kdim, MM_TN), _pick(n, MM_TK)
    return _matmul(
        name, dy, wc,
        pl.BlockSpec((tm, tk), lambda i, j, k: (i, k)),
        pl.BlockSpec((tn, tk), lambda i, j, k: (j, k)),
        (s // tm, kdim // tn, n // tk), (tm, tn),
        [jax.ShapeDtypeStruct((s, kdim), out_dtype)], [pl.BlockSpec((tm, tn), lambda i, j, k: (i, j))],
        _store, tb=True, deps=deps)[0]


def _mm_rows_t(name, dy, wr, out_dtype, epilogue=_store, extra=None, deps=()):
    s, n = dy.shape
    w = wr.reshape(-1, n)
    kdim = w.shape[0]
    tm, tn, tk = _pick(s, MM_TM), _pick(kdim, MM_TN), _pick(n, MM_TK)
    if kdim * n <= MM_RESIDENT:
        tm, tn = _pick(s, MM_TM // 2), kdim
    extras, extra_specs = (), ()
    if extra is not None:
        extras, extra_specs = (extra,), (pl.BlockSpec((tm, tn), lambda i, j, k: (i, j)),)
    return _matmul(
        name, dy, w,
        pl.BlockSpec((tm, tk), lambda i, j, k: (i, k)),
        pl.BlockSpec((tn, tk), lambda i, j, k: (j, k)),
        (s // tm, kdim // tn, n // tk), (tm, tn),
        [jax.ShapeDtypeStruct((s, kdim), out_dtype)], [pl.BlockSpec((tm, tn), lambda i, j, k: (i, j))],
        epilogue, tb=True, extras=extras, extra_specs=extra_specs, deps=deps)[0]


def _grad_cols(name, h, dy, n):
    s, kdim = h.shape
    tm, tn, tk = _pick(kdim, MM_TM), _pick(n, MM_TN), _pick(s, MM_TK)
    per = n // tn
    return _matmul(
        name, h, dy,
        pl.BlockSpec((tk, tm), lambda i, j, k: (k, i)),
        pl.BlockSpec((tk, tn), lambda i, j, k: (k, j)),
        (kdim // tm, N_DEV * per, s // tk), (tm, tn),
        [jax.ShapeDtypeStruct((N_DEV, kdim, n), COMM_DTYPE)],
        [pl.BlockSpec((None, tm, tn), lambda i, j, k: (j // per, i, j % per))],
        _store, ta=True)[0].reshape(N_CHIP, 2, kdim, n)


def _grad_rows(name, v, dy, kk):
    s, kdim = v.shape
    n = dy.shape[1]
    tm, tn, tk = _pick(kdim, MM_TM), _pick(n, MM_TN), _pick(s, MM_TK)
    return _matmul(
        name, v, dy,
        pl.BlockSpec((tk, tm), lambda i, j, k: (k, i)),
        pl.BlockSpec((tk, tn), lambda i, j, k: (k, j)),
        (kdim // tm, n // tn, s // tk), (tm, tn),
        [jax.ShapeDtypeStruct((kdim, n), COMM_DTYPE)],
        [pl.BlockSpec((tm, tn), lambda i, j, k: (i, j))],
        _store, ta=True)[0].reshape(N_CHIP, 2, kk, n)


def _rms(v):
    return lax.rsqrt(jnp.mean(v * v, axis=-1, keepdims=True) + RMS_EPS)


def _colsum(v):
    return jnp.sum(v, axis=0, keepdims=True)


def _vec_spec(width):
    return pl.BlockSpec((1, width), lambda i: (0, 0))


def _residual_fwd(name, x, post=None, pre=None, deps=()):
    s, d = x.shape
    ts = _pick(s, ROW_TILE, 8)
    row = pl.BlockSpec((ts, d), lambda i: (i, 0))
    ins, specs = [x], [row]
    if post is not None:
        ins += list(post)
        specs += [row, _vec_spec(d), _vec_spec(d)]
    if pre is not None:
        ins += list(pre)
        specs += [_vec_spec(d)] * 3
    ins += list(deps)
    specs += [ANY_SPEC] * len(deps)
    outs, out_specs = [], []
    if post is not None:
        outs.append(jax.ShapeDtypeStruct((s, d), F32))
        out_specs.append(row)
    if pre is not None:
        outs.append(jax.ShapeDtypeStruct((s, d), ACT_DTYPE))
        out_specs.append(row)

    def body(*refs):
        refs = list(refs)
        xv = refs.pop(0)[...]
        if post is not None:
            y_ref, gp_ref, gt_ref = refs[:3]
            del refs[:3]
        if pre is not None:
            g_ref, sc_ref, sh_ref = refs[:3]
            del refs[:3]
        del refs[:len(deps)]
        if post is not None:
            yv = y_ref[...]
            xv = xv + gt_ref[...] * ((yv * _rms(yv)) * gp_ref[...])
            refs.pop(0)[...] = xv
        if pre is not None:
            hv = ((xv * _rms(xv)) * g_ref[...]) * (1.0 + sc_ref[...]) + sh_ref[...]
            refs.pop(0)[...] = hv.astype(ACT_DTYPE)

    return pl.pallas_call(
        body, name=name, grid=(s // ts,), in_specs=specs, out_specs=out_specs, out_shape=outs,
        compiler_params=_params("parallel"),
    )(*ins)


def _loss_junction(name, x, y, g_post, gt, target):
    s, d = x.shape
    ts = _pick(s, ROW_TILE, 8)
    row = pl.BlockSpec((ts, d), lambda i: (i, 0))

    def body(x_ref, y_ref, gpost_ref, gt_ref, t_ref, g_ref, dy_ref, loss_ref, sums_ref):
        @pl.when(pl.program_id(0) == 0)
        def _():
            loss_ref[...] = jnp.zeros_like(loss_ref)
            sums_ref[...] = jnp.zeros_like(sums_ref)

        yv = y_ref[...]
        r = _rms(yv)
        nrm = yv * r
        rn = nrm * gpost_ref[...]
        err = (x_ref[...] + gt_ref[...] * rn) - t_ref[...]
        loss_ref[...] += (0.5 / d) * jnp.sum(err * err)
        g = err * (1.0 / d)
        g_ref[...] = g
        sums_ref[3:4, :] += _colsum(g * rn)
        d_o = g * gt_ref[...]
        sums_ref[4:5, :] += _colsum(d_o * nrm)
        dn = d_o * gpost_ref[...]
        dy = r * (dn - nrm * jnp.mean(dn * nrm, axis=-1, keepdims=True))
        sums_ref[5:6, :] += _colsum(dy)
        dy_ref[...] = dy.astype(dy_ref.dtype)

    return pl.pallas_call(
        body, name=name, grid=(s // ts,),
        in_specs=[row, row, _vec_spec(d), _vec_spec(d), row],
        out_specs=[row, row, pl.BlockSpec((8, LANE), lambda i: (0, 0)), pl.BlockSpec((8, d), lambda i: (0, 0))],
        out_shape=[jax.ShapeDtypeStruct((s, d), F32), jax.ShapeDtypeStruct((s, d), ACT_DTYPE),
                   jax.ShapeDtypeStruct((8, LANE), F32), jax.ShapeDtypeStruct((8, d), F32)],
        compiler_params=_params("arbitrary"),
    )(x, y, g_post, gt, target)


def _residual_bwd(name, g_out, pre=None, post=None):
    s, d = g_out.shape
    ts = _pick(s, ROW_TILE, 8)
    row = pl.BlockSpec((ts, d), lambda i: (i, 0))
    ins, specs = [g_out], [row]
    outs, out_specs = [], []
    if pre is not None:
        ins += list(pre)
        specs += [row, row, _vec_spec(d), _vec_spec(d)]
        outs.append(jax.ShapeDtypeStruct((s, d), F32))
        out_specs.append(row)
    if post is not None:
        ins += list(post)
        specs += [row, _vec_spec(d), _vec_spec(d)]
        outs.append(jax.ShapeDtypeStruct((s, d), ACT_DTYPE))
        out_specs.append(row)
    outs.append(jax.ShapeDtypeStruct((8, d), F32))
    out_specs.append(pl.BlockSpec((8, d), lambda i: (0, 0)))

    def body(*refs):
        refs = list(refs)
        g = refs.pop(0)[...]
        if pre is not None:
            dh_ref, x_ref, gpre_ref, sc_ref = refs[:4]
            del refs[:4]
        if post is not None:
            y_ref, gpost_ref, gt_ref = refs[:3]
            del refs[:3]
        sums_ref = refs[-1]

        @pl.when(pl.program_id(0) == 0)
        def _():
            sums_ref[...] = jnp.zeros_like(sums_ref)

        if pre is not None:
            dh, xv = dh_ref[...], x_ref[...]
            r = _rms(xv)
            nrm = xv * r
            d_rn = dh * (1.0 + sc_ref[...])
            sums_ref[0:1, :] += _colsum(dh)
            sums_ref[1:2, :] += _colsum(dh * (nrm * gpre_ref[...]))
            sums_ref[2:3, :] += _colsum(d_rn * nrm)
            dn = d_rn * gpre_ref[...]
            g = g + r * (dn - nrm * jnp.mean(dn * nrm, axis=-1, keepdims=True))
            refs.pop(0)[...] = g
        if post is not None:
            yv = y_ref[...]
            r = _rms(yv)
            nrm = yv * r
            sums_ref[3:4, :] += _colsum(g * (nrm * gpost_ref[...]))
            d_o = g * gt_ref[...]
            sums_ref[4:5, :] += _colsum(d_o * nrm)
            dn = d_o * gpost_ref[...]
            dy = r * (dn - nrm * jnp.mean(dn * nrm, axis=-1, keepdims=True))
            sums_ref[5:6, :] += _colsum(dy)
            refs.pop(0)[...] = dy.astype(ACT_DTYPE)

    return pl.pallas_call(
        body, name=name, grid=(s // ts,), in_specs=specs, out_specs=out_specs, out_shape=outs,
        compiler_params=_params("arbitrary"),
    )(*ins)


def _chunks(d):
    cw = min(CONV_CHUNK, d)
    return [(c * cw, cw) for c in range(d // cw)]


def _glu_of(u_ref, d):
    return u_ref[:, :d].astype(F32) * jax.nn.sigmoid(u_ref[:, d:].astype(F32))


def _gated_input(z_ref, d):
    return z_ref[:, d:2 * d].astype(F32) * z_ref[:, 2 * d:].astype(F32)


def _shift_rows(buf, sh, c0, cw, rows):
    for r in range(1, 8):
        sh[r - 1] = buf[r:r + rows, c0:c0 + cw]


def _window(buf, sh, c0, cw, offset, ts):
    q, r = divmod(offset, 8)
    if r == 0:
        return buf[8 * q:8 * q + ts, c0:c0 + cw]
    return sh[r - 1, 8 * q:8 * q + ts, :]


def _glu_conv_ln_swish(name, u, dw, dwb, ln_g, ln_b):
    s, d2 = u.shape
    d = d2 // 2
    taps = dw.shape[0]
    ts, halo = _pick(s, CONV_TILE, 8), CONV_HALO
    lead = halo - (taps - 1)
    per = ts // halo
    cw = min(CONV_CHUNK, d)

    def body(cur_ref, prev_ref, dw_ref, dwb_ref, g_ref, b_ref, cv_ref, v_ref, buf, sh):
        i = pl.program_id(0)
        buf[0:halo, :] = jnp.where(i > 0, _glu_of(prev_ref, d), 0.0)
        buf[halo:, :] = _glu_of(cur_ref, d)
        for c0, _ in _chunks(d):
            _shift_rows(buf, sh, c0, cw, ts + halo - 8)
            acc = jnp.zeros((ts, cw), F32)
            for k in range(taps):
                acc += dw_ref[k:k + 1, c0:c0 + cw] * _window(buf, sh, c0, cw, lead + k, ts)
            cv_ref[:, c0:c0 + cw] = acc + dwb_ref[:, c0:c0 + cw]
        cv = cv_ref[...]
        mu = jnp.mean(cv, axis=-1, keepdims=True)
        xc = cv - mu
        var = jnp.mean(xc * xc, axis=-1, keepdims=True)
        ln = (xc * lax.rsqrt(var + LN_EPS)) * g_ref[...] + b_ref[...]
        v_ref[...] = (ln * jax.nn.sigmoid(ln)).astype(v_ref.dtype)

    row = pl.BlockSpec((ts, d), lambda i: (i, 0))
    return pl.pallas_call(
        body, name=name, grid=(s // ts,),
        in_specs=[pl.BlockSpec((ts, d2), lambda i: (i, 0)),
                  pl.BlockSpec((halo, d2), lambda i: (jnp.maximum(i * per - 1, 0), 0)),
                  pl.BlockSpec((taps, d), lambda i: (0, 0)), _vec_spec(d), _vec_spec(d), _vec_spec(d)],
        out_specs=[row, row],
        out_shape=[jax.ShapeDtypeStruct((s, d), F32), jax.ShapeDtypeStruct((s, d), ACT_DTYPE)],
        scratch_shapes=[pltpu.VMEM((ts + halo, d), F32), pltpu.VMEM((7, ts + halo - 8, cw), F32)],
        compiler_params=_params("parallel"),
    )(u, u, dw, dwb, ln_g, ln_b)


def _ln_swish_bwd(name, dv, cv, ln_g, ln_b, deps=()):
    s, d = cv.shape
    ts = _pick(s, ROW_TILE, 8)

    def body(dv_ref, cv_ref, g_ref, b_ref, *rest):
        dcv_ref, sums_ref = rest[len(deps):]

        @pl.when(pl.program_id(0) == 0)
        def _():
            sums_ref[...] = jnp.zeros_like(sums_ref)

        cv = cv_ref[...]
        mu = jnp.mean(cv, axis=-1, keepdims=True)
        xc = cv - mu
        rstd = lax.rsqrt(jnp.mean(xc * xc, axis=-1, keepdims=True) + LN_EPS)
        nhat = xc * rstd
        ln = nhat * g_ref[...] + b_ref[...]
        sg = jax.nn.sigmoid(ln)
        dl = dv_ref[...] * (sg * (1.0 + ln * (1.0 - sg)))
        sums_ref[0:1, :] += _colsum(dl * nhat)
        sums_ref[1:2, :] += _colsum(dl)
        dn = dl * g_ref[...]
        dcv = rstd * (dn - jnp.mean(dn, axis=-1, keepdims=True)
                      - nhat * jnp.mean(dn * nhat, axis=-1, keepdims=True))
        sums_ref[2:3, :] += _colsum(dcv)
        dcv_ref[...] = dcv

    row = pl.BlockSpec((ts, d), lambda i: (i, 0))
    return pl.pallas_call(
        body, name=name, grid=(s // ts,),
        in_specs=[row, row, _vec_spec(d), _vec_spec(d)] + [ANY_SPEC] * len(deps),
        out_specs=[row, pl.BlockSpec((8, d), lambda i: (0, 0))],
        out_shape=[jax.ShapeDtypeStruct((s, d), F32), jax.ShapeDtypeStruct((8, d), F32)],
        compiler_params=_params("arbitrary"),
    )(dv, cv, ln_g, ln_b, *deps)


def _conv_glu_bwd(name, dcv, u, dw):
    s, d = dcv.shape
    taps = dw.shape[0]
    taps8 = -(-taps // 8) * 8
    ts, halo = _pick(s, CONV_TILE, 8), CONV_HALO
    lead = halo - (taps - 1)
    per = ts // halo
    n_tiles = s // ts
    cw = min(CONV_CHUNK, d)

    def body(dcv_ref, next_ref, u_ref, prev_ref, dw_ref, du_ref, ddw_ref, sums_ref, nbuf, pbuf, nsh, psh):
        i = pl.program_id(0)

        @pl.when(i == 0)
        def _():
            ddw_ref[...] = jnp.zeros_like(ddw_ref)
            sums_ref[...] = jnp.zeros_like(sums_ref)

        nbuf[0:ts, :] = dcv_ref[...]
        nbuf[ts:, :] = jnp.where(i < n_tiles - 1, next_ref[...], 0.0)
        pbuf[0:halo, :] = jnp.where(i > 0, _glu_of(prev_ref, d), 0.0)
        pbuf[halo:, :] = _glu_of(u_ref, d)
        for c0, _ in _chunks(d):
            _shift_rows(nbuf, nsh, c0, cw, ts + halo - 8)
            _shift_rows(pbuf, psh, c0, cw, ts + halo - 8)
            dcv = dcv_ref[:, c0:c0 + cw]
            dglu = jnp.zeros((ts, cw), F32)
            for k in range(taps):
                dglu += dw_ref[k:k + 1, c0:c0 + cw] * _window(nbuf, nsh, c0, cw, taps - 1 - k, ts)
                ddw_ref[k:k + 1, c0:c0 + cw] += _colsum(dcv * _window(pbuf, psh, c0, cw, lead + k, ts))
            a = u_ref[:, c0:c0 + cw].astype(F32)
            sg = jax.nn.sigmoid(u_ref[:, d + c0:d + c0 + cw].astype(F32))
            da = dglu * sg
            dg = dglu * a * (sg * (1.0 - sg))
            du_ref[:, c0:c0 + cw] = da.astype(du_ref.dtype)
            du_ref[:, d + c0:d + c0 + cw] = dg.astype(du_ref.dtype)
            sums_ref[0:1, c0:c0 + cw] += _colsum(da)
            sums_ref[0:1, d + c0:d + c0 + cw] += _colsum(dg)

    row = pl.BlockSpec((ts, d), lambda i: (i, 0))
    wide = pl.BlockSpec((ts, 2 * d), lambda i: (i, 0))
    return pl.pallas_call(
        body, name=name, grid=(n_tiles,),
        in_specs=[row, pl.BlockSpec((halo, d), lambda i: (jnp.minimum((i + 1) * per, s // halo - 1), 0)),
                  wide, pl.BlockSpec((halo, 2 * d), lambda i: (jnp.maximum(i * per - 1, 0), 0)),
                  pl.BlockSpec((taps, d), lambda i: (0, 0))],
        out_specs=[wide, pl.BlockSpec((taps8, d), lambda i: (0, 0)), pl.BlockSpec((8, 2 * d), lambda i: (0, 0))],
        out_shape=[jax.ShapeDtypeStruct((s, 2 * d), ACT_DTYPE), jax.ShapeDtypeStruct((taps8, d), F32),
                   jax.ShapeDtypeStruct((8, 2 * d), F32)],
        scratch_shapes=[pltpu.VMEM((ts + halo, d), F32), pltpu.VMEM((ts + halo, d), F32),
                        pltpu.VMEM((7, ts + halo - 8, cw), F32), pltpu.VMEM((7, ts + halo - 8, cw), F32)],
        compiler_params=_params("arbitrary"),
    )(dcv, dcv, u, u, dw)


def _short_conv_fwd(name, z, w):
    s, d3 = z.shape
    d = d3 // 3
    taps = w.shape[0]
    ts, halo = _pick(s, CONV_TILE, 8), SHORT_HALO
    lead = halo - (taps - 1)
    per = ts // halo

    def body(z_ref, prev_ref, w_ref, q_ref, pbuf):
        i = pl.program_id(0)
        pbuf[0:halo, :] = jnp.where(i > 0, _gated_input(prev_ref, d), 0.0)
        pbuf[halo:, :] = _gated_input(z_ref, d)
        for c0, cw in _chunks(d):
            acc = jnp.zeros((ts, cw), F32)
            for k in range(taps):
                acc += w_ref[k:k + 1, c0:c0 + cw] * pbuf[lead + k:lead + k + ts, c0:c0 + cw]
            q_ref[:, c0:c0 + cw] = (z_ref[:, c0:c0 + cw].astype(F32) * acc).astype(q_ref.dtype)

    return pl.pallas_call(
        body, name=name, grid=(s // ts,),
        in_specs=[pl.BlockSpec((ts, d3), lambda i: (i, 0)),
                  pl.BlockSpec((halo, d3), lambda i: (jnp.maximum(i * per - 1, 0), 0)),
                  pl.BlockSpec((taps, d), lambda i: (0, 0))],
        out_specs=pl.BlockSpec((ts, d), lambda i: (i, 0)),
        out_shape=jax.ShapeDtypeStruct((s, d), ACT_DTYPE),
        scratch_shapes=[pltpu.VMEM((ts + halo, d), F32)],
        compiler_params=_params("parallel"),
    )(z, z, w)


def _short_conv_bwd(name, dq, z, w, deps=()):
    s, d3 = z.shape
    d = d3 // 3
    taps = w.shape[0]
    ts, halo = _pick(s, CONV_TILE, 8), SHORT_HALO
    lead = halo - (taps - 1)
    per = ts // halo
    n_tiles = s // ts

    def body(dq_ref, dqn_ref, z_ref, zp_ref, zn_ref, w_ref, *rest):
        dz_ref, sums_ref, pbuf, ubuf = rest[len(deps):]
        i = pl.program_id(0)

        @pl.when(i == 0)
        def _():
            sums_ref[...] = jnp.zeros_like(sums_ref)

        pbuf[0:halo, :] = jnp.where(i > 0, _gated_input(zp_ref, d), 0.0)
        pbuf[halo:, :] = _gated_input(z_ref, d)
        ubuf[0:ts, :] = dq_ref[...] * z_ref[:, 0:d].astype(F32)
        ubuf[ts:, :] = jnp.where(i < n_tiles - 1, dqn_ref[...] * zn_ref[:, 0:d].astype(F32), 0.0)
        for c0, cw in _chunks(d):
            du = ubuf[0:ts, c0:c0 + cw]
            conv = jnp.zeros((ts, cw), F32)
            dp = jnp.zeros((ts, cw), F32)
            for k in range(taps):
                wk = w_ref[k:k + 1, c0:c0 + cw]
                shifted = pbuf[lead + k:lead + k + ts, c0:c0 + cw]
                conv += wk * shifted
                dp += wk * ubuf[taps - 1 - k:taps - 1 - k + ts, c0:c0 + cw]
                sums_ref[k:k + 1, c0:c0 + cw] += _colsum(du * shifted)
            dz_ref[:, c0:c0 + cw] = (dq_ref[:, c0:c0 + cw] * conv).astype(dz_ref.dtype)
            dz_ref[:, d + c0:d + c0 + cw] = (dp * z_ref[:, 2 * d + c0:2 * d + c0 + cw].astype(F32)).astype(dz_ref.dtype)
            dz_ref[:, 2 * d + c0:2 * d + c0 + cw] = (dp * z_ref[:, d + c0:d + c0 + cw].astype(F32)).astype(dz_ref.dtype)

    last = s // halo - 1
    return pl.pallas_call(
        body, name=name, grid=(n_tiles,),
        in_specs=[pl.BlockSpec((ts, d), lambda i: (i, 0)),
                  pl.BlockSpec((halo, d), lambda i: (jnp.minimum((i + 1) * per, last), 0)),
                  pl.BlockSpec((ts, d3), lambda i: (i, 0)),
                  pl.BlockSpec((halo, d3), lambda i: (jnp.maximum(i * per - 1, 0), 0)),
                  pl.BlockSpec((halo, d3), lambda i: (jnp.minimum((i + 1) * per, last), 0)),
                  pl.BlockSpec((taps, d), lambda i: (0, 0))] + [ANY_SPEC] * len(deps),
        out_specs=[pl.BlockSpec((ts, d3), lambda i: (i, 0)), pl.BlockSpec((8, d), lambda i: (0, 0))],
        out_shape=[jax.ShapeDtypeStruct((s, d3), ACT_DTYPE), jax.ShapeDtypeStruct((8, d), F32)],
        scratch_shapes=[pltpu.VMEM((ts + halo, d), F32), pltpu.VMEM((ts + halo, d), F32)],
        compiler_params=_params("arbitrary"),
    )(dq, dq, z, z, z, w, *deps)


def _silu(v):
    return v * jax.nn.sigmoid(v)


def _modulation(name, c_all, mod_w, mod_b_cols):
    nl, d, n = mod_w.shape
    b = c_all.shape[0]
    tn = _pick(n, 512)

    def body(c_ref, w_ref, b_ref, o_ref):
        ca = _silu(c_ref[...]).astype(MXU_DTYPE)
        o_ref[...] = jnp.dot(ca, w_ref[...].astype(MXU_DTYPE), preferred_element_type=F32) + b_ref[...]

    return pl.pallas_call(
        body, name=name, grid=(nl, n // tn),
        in_specs=[pl.BlockSpec((b, d), lambda l, j: (0, 0)),
                  pl.BlockSpec((None, d, tn), lambda l, j: (l, 0, j)),
                  pl.BlockSpec((None, 1, tn), lambda l, j: (l, 0, j))],
        out_specs=pl.BlockSpec((None, b, tn), lambda l, j: (l, 0, j)),
        out_shape=jax.ShapeDtypeStruct((nl, b, n), F32),
        compiler_params=_params("parallel", "parallel"),
    )(c_all, mod_w, mod_b_cols.reshape(nl, 1, n))


def _adamw(g, w, m, v):
    m = ADAM_B1 * m + (1.0 - ADAM_B1) * g
    v = ADAM_B2 * v + (1.0 - ADAM_B2) * (g * g)
    m_hat = m / (1.0 - ADAM_B1 ** ADAM_STEP)
    v_hat = v / (1.0 - ADAM_B2 ** ADAM_STEP)
    delta = -ADAM_LR * (m_hat / (jnp.sqrt(v_hat) + ADAM_EPS) + ADAM_WD * w)
    return delta, m, v


def _write_update(g, w_ref, m_ref, v_ref, outs):
    delta, m, v = _adamw(g, w_ref[...], m_ref[...], v_ref[...])
    outs[0][...] = g
    outs[1][...] = delta
    outs[2][...] = m
    outs[3][...] = v


def _modulation_update(name, c_all_t, dmod, w, m, v, deps=()):
    nl, d, n = w.shape
    b = c_all_t.shape[1]
    tr = _pick(d, 256, 8)

    def body(c_ref, dm_ref, w_ref, m_ref, v_ref, *rest):
        outs = rest[len(deps):]
        ca = _silu(c_ref[...])
        dm = dm_ref[...]
        g = ca[:, 0:1] * dm[0:1, :]
        for i in range(1, b):
            g += ca[:, i:i + 1] * dm[i:i + 1, :]
        _write_update(g, w_ref, m_ref, v_ref, outs)

    blk = pl.BlockSpec((None, tr, n), lambda l, r: (l, r, 0))
    return pl.pallas_call(
        body, name=name, grid=(nl, d // tr),
        in_specs=[pl.BlockSpec((tr, b), lambda l, r: (r, 0)), pl.BlockSpec((None, b, n), lambda l, r: (l, 0, 0)),
                  blk, blk, blk] + [ANY_SPEC] * len(deps),
        out_specs=[blk] * 4, out_shape=[jax.ShapeDtypeStruct(w.shape, F32)] * 4,
        compiler_params=_params("parallel", "parallel"),
    )(c_all_t, dmod, w, m, v, *deps)


def _weight_update(name, parts, w, m, v):
    nl, rows, cols = w.shape
    tr = _pick(rows, max(8, (1 << 18) // cols), 8)

    def body(*refs):
        p_refs = refs[:nl]
        w_ref, m_ref, v_ref = refs[nl:nl + 3]
        outs = refs[nl + 3:]
        for q in range(nl):
            @pl.when(pl.program_id(0) == q)
            def _(q=q):
                g = p_refs[q][0].astype(F32)
                for k in range(1, N_CHIP):
                    g += p_refs[q][k].astype(F32)
                _write_update(g, w_ref, m_ref, v_ref, outs)

    def part_spec(q):
        return pl.BlockSpec((N_CHIP, tr, cols), lambda l, r: (0, jnp.where(l == q, r, 0), 0))

    blk = pl.BlockSpec((None, tr, cols), lambda l, r: (l, r, 0))
    return pl.pallas_call(
        body, name=name, grid=(nl, rows // tr),
        in_specs=[part_spec(q) for q in range(nl)] + [blk, blk, blk],
        out_specs=[blk] * 4, out_shape=[jax.ShapeDtypeStruct(w.shape, F32)] * 4,
        compiler_params=_params("arbitrary", "arbitrary"),
    )(*parts, w, m, v)


def _vector_update(name, gathered, w, m, v):
    _, rows, cols = gathered.shape
    rw = w.shape[0]

    def body(g_ref, w_ref, m_ref, v_ref, tot_ref, d_ref, m2_ref, v2_ref):
        tot = g_ref[0]
        for k in range(1, N_DEV):
            tot += g_ref[k]
        tot_ref[...] = tot
        delta, m2, v2 = _adamw(tot[0:rw], w_ref[...], m_ref[...], v_ref[...])
        d_ref[...] = delta
        m2_ref[...] = m2
        v2_ref[...] = v2

    vm = pl.BlockSpec(memory_space=pltpu.VMEM)
    return pl.pallas_call(
        body, name=name, in_specs=[vm] * 4, out_specs=[vm] * 4,
        out_shape=[jax.ShapeDtypeStruct((rows, cols), F32)] + [jax.ShapeDtypeStruct((rw, cols), F32)] * 3,
        compiler_params=pltpu.CompilerParams(vmem_limit_bytes=V7X_VMEM_LIMIT_BYTES),
    )(gathered, w, m, v)


def _plain_update(name, g, w, m, v):
    def body(g_ref, w_ref, m_ref, v_ref, d_ref, m2_ref, v2_ref):
        delta, m2, v2 = _adamw(g_ref[...], w_ref[...], m_ref[...], v_ref[...])
        d_ref[...] = delta
        m2_ref[...] = m2
        v2_ref[...] = v2

    vm = pl.BlockSpec(memory_space=pltpu.VMEM)
    return pl.pallas_call(
        body, name=name, in_specs=[vm] * 4, out_specs=[vm] * 3,
        out_shape=[jax.ShapeDtypeStruct(w.shape, F32)] * 3,
    )(g, w, m, v)


def _rows(a, width, mult=8):
    r = a.reshape(-1, width)
    pad = -r.shape[0] % mult
    return jnp.pad(r, ((0, pad), (0, 0))) if pad else r


def _pack(blocks, width):
    parts, spans, at = [], [], 0
    for a in blocks:
        n = a.size // width
        p = _rows(a, width)
        parts.append(p)
        spans.append((at, n))
        at += p.shape[0]
    return jnp.concatenate(parts, axis=0), spans


def kernel(x, c, mod_w, mod_b, pre_mix_g, post_mix_g, pre_ffn_g, post_ffn_g, a_w1, a_b1, a_dw, a_dwb, a_ln_g, a_ln_b, a_w2, a_b2, b_w_in, b_conv, b_w_out, f_w1, f_w2, loss_target, m_mod_w, m_mod_b, m_pre_mix_g, m_post_mix_g, m_pre_ffn_g, m_post_ffn_g, m_a_w1, m_a_b1, m_a_dw, m_a_dwb, m_a_ln_g, m_a_ln_b, m_a_w2, m_a_b2, m_b_w_in, m_b_conv, m_b_w_out, m_f_w1, m_f_w2, v_mod_w, v_mod_b, v_pre_mix_g, v_post_mix_g, v_pre_ffn_g, v_post_ffn_g, v_a_w1, v_a_b1, v_a_dw, v_a_dwb, v_a_ln_g, v_a_ln_b, v_a_w2, v_a_b2, v_b_w_in, v_b_conv, v_b_w_out, v_f_w1, v_f_w2):
    depth, d = pre_mix_g.shape
    n_a, n_b = a_w1.shape[0], b_w_in.shape[0]
    s = x.shape[1]
    dsh = d // N_DEV
    taps_a, taps_b = a_dw.shape[1], b_conv.shape[1]
    px, py, pc = _pos()
    me = 4 * px + 2 * py + pc
    x0 = x.reshape(s, d)
    target = loss_target.reshape(s, d)

    me1 = jnp.reshape(me, (1,)).astype(jnp.int32)

    def members(i):
        j = i // 2
        mix = [(a_w1, j, True), (a_w2, j, False)] if i % 2 == 0 else [(b_w_in, j, True), (b_w_out, j, False)]
        return mix, [(f_w1, i, True), (f_w2, i, False)]

    placed = {(i, q): _place("place_l%d_%d" % (i, q), w, layer, me1, by_columns)
              for i in range(depth) for q, (w, layer, by_columns) in enumerate(sum(members(i), []))}

    def start_group(tag, group, after=()):
        return _gather_start(tag, [placed[key] for key in group], after)

    packed0, spans0 = _pack([c, a_dw, b_conv], dsh)
    got0 = _small_gather("gather_cond", packed0)

    def full_width(span):
        at, n = span
        return jnp.transpose(got0[:, at:at + n, :], (1, 0, 2)).reshape(n, d)

    c_all = got0[:, spans0[0][0]:spans0[0][0] + spans0[0][1], :].reshape(N_DEV, d)
    a_dw_full = full_width(spans0[1]).reshape(n_a, taps_a, d)
    b_conv_full = full_width(spans0[2]).reshape(n_b, taps_b, d)

    n_mod = mod_w.shape[2]
    mod_b_cols = lax.dynamic_slice_in_dim(mod_b, me * n_mod, n_mod, axis=1)
    mod_local = _modulation("modulation", c_all, mod_w, mod_b_cols)
    got1 = _small_gather("gather_mod", mod_local.reshape(depth * N_DEV, n_mod))
    mod_me = lax.dynamic_index_in_dim(got1.reshape(N_DEV, depth, N_DEV, n_mod), me, axis=2, keepdims=False)
    mod_me = jnp.transpose(mod_me, (1, 0, 2)).reshape(depth, 6, 1, d)

    def vec(a, i):
        return a[i].reshape(1, -1)

    def pass_groups(tag, handles, after):
        for q in range(len(handles)):
            handles[q] = _gather_pass("%s%d" % (tag, q), handles[q], after)
            after = handles[q]["token"]
        return after

    def wait_groups(tag, handles, after):
        return sum([_gather_wait("%s%d" % (tag, q), hq, after) for q, hq in enumerate(handles)], [])

    mix_copy = [start_group("l0m", [(0, 0)], (got1,))]
    mix_late = [start_group("l0n", [(0, 1)], (mix_copy[0]["token"],))]
    ffn_copy = [start_group("l0u", [(0, 2)], (mix_late[0]["token"],))]
    ffn_copy.append(start_group("l0d", [(0, 3)], (ffn_copy[0]["token"],)))
    saved = []
    xs = x0
    h = _residual_fwd("fwd_in", xs, pre=(vec(pre_mix_g, 0), mod_me[0, 1], mod_me[0, 0]),
                      deps=[hq["token"] for hq in ffn_copy] + [buf for (i, _), buf in placed.items() if i > 0])[0]
    w_mix = wait_groups("l0m", mix_copy, pass_groups("l0m", mix_copy, h))
    for i in range(depth):
        j = i // 2
        sh_m, sc_m, gt_m, sh_f, sc_f, gt_f = [mod_me[i, q] for q in range(6)]
        keep = {"x_mix": xs, "h_mix": h}
        early = i > 0
        w1g, w2g = w_mix if early else (w_mix[0], None)
        if i % 2 == 0:
            u = _mm_cols("a_w1", h, w1g, F32, _store_bias, bias=vec(a_b1, j))[0]
            deps = [pass_groups("l%df" % i, ffn_copy, u)] if early else []
            if not early:
                w2g, = wait_groups("l0n", mix_late, pass_groups("l0n", mix_late, u))
            cv, v = _glu_conv_ln_swish("a_conv", u, a_dw_full[j], vec(a_dwb, j), vec(a_ln_g, j), vec(a_ln_b, j))
            y = _mm_rows("a_w2", v, w2g, F32, bias=vec(a_b2, j), deps=deps)
            keep.update(u=u, cv=cv, v=v)
        else:
            z = _mm_cols("b_w_in", h, w1g, F32)[0]
            deps = [pass_groups("l%df" % i, ffn_copy, z)] if early else []
            q = _short_conv_fwd("b_conv", z, b_conv_full[j])
            y = _mm_rows("b_w_out", q, w2g, F32, deps=deps)
            keep.update(z=z, q=q)
        now = ffn_copy if early else ffn_copy[:1]
        after = y if early else pass_groups("l%df" % i, now, y)
        deps = []
        if i + 1 < depth:
            mix_copy = [start_group("l%dm" % (i + 1), [(i + 1, 0), (i + 1, 1)], (after,))]
            ffn_next = [start_group("l%df" % (i + 1), [(i + 1, 2), (i + 1, 3)], (mix_copy[0]["token"],))]
            deps = [mix_copy[0]["token"], ffn_next[0]["token"]]
        landed = wait_groups("l%df" % i, now, after)
        fw1g = landed[0]
        xs, h = _residual_fwd("fwd_mid", xs, post=(y, vec(post_mix_g, i), gt_m),
                              pre=(vec(pre_ffn_g, i), sc_f, sh_f))
        keep.update(x_ffn=xs, h_ffn=h, y_mix=y)
        act, hid = _mm_cols("f_w1", h, fw1g, ACT_DTYPE, _store_relu2, n_out=2, deps=deps)
        if early:
            fw2g = landed[1]
        else:
            late = ffn_copy[1:]
            fw2g, = wait_groups("l0g", late, pass_groups("l0g", late, act))
        deps = [pass_groups("l%dm" % (i + 1), mix_copy, act)] if i + 1 < depth else []
        y = _mm_rows("f_w2", act, fw2g, F32, deps=deps)
        keep.update(w1g=w1g, w2g=w2g, fw1g=fw1g, fw2g=fw2g, act=act, hid=hid, y_ffn=y)
        saved.append(keep)
        if i + 1 < depth:
            w_mix = wait_groups("l%dm" % (i + 1), mix_copy, y)
            ffn_copy = ffn_next
            nxt = mod_me[i + 1]
            xs, h = _residual_fwd("fwd_next", xs, post=(y, vec(post_ffn_g, i), gt_f),
                                  pre=(vec(pre_mix_g, i + 1), nxt[1], nxt[0]))
        else:
            g, dy, loss_blk, sums = _loss_junction("loss_junction", xs, y, vec(post_ffn_g, i), gt_f, target)

    where = jnp.stack([pc, 2 * px + py]).astype(jnp.int32)
    land = {"a_w1": [None] * n_a, "a_w2": [None] * n_a, "b_w_in": [None] * n_b, "b_w_out": [None] * n_b,
            "f_w1": [None] * depth, "f_w2": [None] * depth}
    reductions = []

    def reduce_group(tag, slots, swap, after):
        grads, theirs = _sibling_wait(tag, swap, after)
        pairs = [_chip_partial("chip_partial_%s_%d" % (tag, q), gq, tq, where)
                 for q, (gq, tq) in enumerate(zip(grads, theirs))]
        handle = _reduce_start(tag, [p for p, _ in pairs], [l for _, l in pairs])
        reductions.append((tag, slots, handle))
        return handle["token"]

    zero_vec = jnp.zeros((d,), F32)
    dmod = [[zero_vec] * 6 for _ in range(depth)]
    small = {name: [None] * depth for name in ("pre_mix_g", "post_mix_g", "pre_ffn_g", "post_ffn_g")}
    small_a = {name: [None] * n_a for name in ("a_b1", "a_dwb", "a_ln_g", "a_ln_b", "a_b2", "a_dw")}
    small_b = {"b_conv": [None] * n_b}

    deps = []
    for i in reversed(range(depth)):
        j = i // 2
        kp = saved[i]
        sh_m, sc_m, gt_m, sh_f, sc_f, gt_f = [mod_me[i, q] for q in range(6)]
        dmod[i][5] = sums[3]
        small["post_ffn_g"][i] = sums[4]
        dhid = _mm_rows_t("f_w2_t", dy, kp["fw2g"], ACT_DTYPE, _store_relu2_grad, extra=kp["hid"], deps=deps)
        deps = []
        if i + 1 < depth:
            deps = [reduce_group(*last_swap, dhid)]
        g_fw2 = _grad_rows("f_w2_grad", kp["act"], dy, f_w2.shape[1])
        dh = _mm_cols_t("f_w1_t", dhid, kp["fw1g"], F32, deps=deps)
        g_fw1 = _grad_cols("f_w1_grad", kp["h_ffn"], dhid, f_w1.shape[2])
        ffn_swap = _sibling_start("r%df" % i, [g_fw1, g_fw2])
        deps = [ffn_swap["token"]]
        g, dy, sums = _residual_bwd("bwd_mid", g, pre=(dh, kp["x_ffn"], vec(pre_ffn_g, i), sc_f),
                                    post=(kp["y_mix"], vec(post_mix_g, i), gt_m))
        dmod[i][3], dmod[i][4], dmod[i][2] = sums[0], sums[1], sums[3]
        small["pre_ffn_g"][i], small["post_mix_g"][i] = sums[2], sums[4]
        if i % 2 == 0:
            small_a["a_b2"][j] = sums[5]
            dv = _mm_rows_t("a_w2_t", dy, kp["w2g"], F32, deps=deps)
            g_w2 = _grad_rows("a_w2_grad", kp["v"], dy, a_w2.shape[1])
            deps = [reduce_group("r%df" % i, (("f_w1", i), ("f_w2", i)), ffn_swap, g_w2)]
            dcv, lsum = _ln_swish_bwd("a_ln_bwd", dv, kp["cv"], vec(a_ln_g, j), vec(a_ln_b, j), deps=deps)
            small_a["a_ln_g"][j], small_a["a_ln_b"][j], small_a["a_dwb"][j] = lsum[0], lsum[1], lsum[2]
            du, ddw, usum = _conv_glu_bwd("a_conv_bwd", dcv, kp["u"], a_dw_full[j])
            small_a["a_dw"][j], small_a["a_b1"][j] = ddw[:taps_a], usum[0]
            dh = _mm_cols_t("a_w1_t", du, kp["w1g"], F32)
            g_w1 = _grad_cols("a_w1_grad", kp["h_mix"], du, a_w1.shape[2])
            names = ("a_w1", "a_w2")
        else:
            dq = _mm_rows_t("b_w_out_t", dy, kp["w2g"], F32, deps=deps)
            g_w2 = _grad_rows("b_w_out_grad", kp["q"], dy, b_w_out.shape[1])
            deps = [reduce_group("r%df" % i, (("f_w1", i), ("f_w2", i)), ffn_swap, g_w2)]
            dz, wsum = _short_conv_bwd("b_conv_bwd", dq, kp["z"], b_conv_full[j], deps=deps)
            small_b["b_conv"][j] = wsum[:taps_b]
            dh = _mm_cols_t("b_w_in_t", dz, kp["w1g"], F32)
            g_w1 = _grad_cols("b_w_in_grad", kp["h_mix"], dz, b_w_in.shape[2])
            names = ("b_w_in", "b_w_out")
        slots = ((names[0], j), (names[1], j))
        mix_swap = _sibling_start("r%dm" % i, [g_w1, g_w2])
        deps = [mix_swap["token"]]
        if i > 0:
            prev = saved[i - 1]
            g, dy, sums = _residual_bwd("bwd_next", g, pre=(dh, kp["x_mix"], vec(pre_mix_g, i), sc_m),
                                        post=(prev["y_ffn"], vec(post_ffn_g, i - 1), mod_me[i - 1, 5]))
        else:
            g, sums = _residual_bwd("bwd_in", g, pre=(dh, kp["x_mix"], vec(pre_mix_g, i), sc_m))
        dmod[i][0], dmod[i][1] = sums[0], sums[1]
        small["pre_mix_g"][i] = sums[2]
        last_swap = ("r%dm" % i, slots, mix_swap)
    grad_x = g.reshape(x.shape)

    reps = [jnp.stack([jnp.stack(r) for r in dmod]),
            jnp.stack(small["pre_mix_g"]), jnp.stack(small["post_mix_g"]),
            jnp.stack(small["pre_ffn_g"]), jnp.stack(small["post_ffn_g"]),
            jnp.stack(small_a["a_b1"]), jnp.stack(small_a["a_dwb"]), jnp.stack(small_a["a_ln_g"]),
            jnp.stack(small_a["a_ln_b"]), jnp.stack(small_a["a_b2"])]
    rep_w = [mod_b, pre_mix_g, post_mix_g, pre_ffn_g, post_ffn_g, a_b1, a_dwb, a_ln_g, a_ln_b, a_b2]
    rep_m = [m_mod_b, m_pre_mix_g, m_post_mix_g, m_pre_ffn_g, m_post_ffn_g, m_a_b1, m_a_dwb, m_a_ln_g, m_a_ln_b, m_a_b2]
    rep_v = [v_mod_b, v_pre_mix_g, v_post_mix_g, v_pre_ffn_g, v_post_ffn_g, v_a_b1, v_a_dwb, v_a_ln_g, v_a_ln_b, v_a_b2]
    w_small, spans = _pack(rep_w, d)
    m_small, _ = _pack(rep_m, d)
    v_small, _ = _pack(rep_v, d)
    loss_row = jnp.pad(loss_blk[0:1, 0:1], ((0, 0), (0, d - 1)))
    part_small, spans_g = _pack(reps + [jnp.stack(small_a["a_dw"]), jnp.stack(small_b["b_conv"]), loss_row], d)
    small_copy = _gather_start("small", [_place("place_small", part_small[None], 0, me1, False, F32)],
                               (last_swap[2]["token"],))
    after = reduce_group(*last_swap, small_copy["token"])
    big = {"f_w1": (f_w1, m_f_w1, v_f_w1), "f_w2": (f_w2, m_f_w2, v_f_w2),
           "b_w_in": (b_w_in, m_b_w_in, v_b_w_in), "b_w_out": (b_w_out, m_b_w_out, v_b_w_out),
           "a_w1": (a_w1, m_a_w1, v_a_w1), "a_w2": (a_w2, m_a_w2, v_a_w2)}
    big_out = {}

    def update_complete():
        nonlocal after
        for k, wmv in big.items():
            if k not in big_out and all(b is not None for b in land[k]):
                big_out[k] = _weight_update("update_" + k, land[k], *wmv)
                after = big_out[k][3]

    def wait_reduction(tag, slots, handle):
        nonlocal after
        landed = _reduce_wait(tag, handle, after)
        after = landed[0]
        for (key, idx), buf in zip(slots, landed):
            land[key][idx] = buf

    for entry in reductions[:-1]:
        wait_reduction(*entry)
    update_complete()
    small_copy = _gather_pass("small", small_copy, after)
    got2, = _gather_wait("small", small_copy, small_copy["token"])
    total, d_small, m2_small, v2_small = _vector_update("vector_update", got2, w_small, m_small, v_small)

    def unpack(buf, span, like):
        return buf[span[0]:span[0] + span[1]].reshape(like.shape)

    rep_out = [[unpack(buf, sp, w) for sp, w in zip(spans, rep_w)] for buf in (total, d_small, m2_small, v2_small)]
    loss = total[spans_g[-1][0], 0]

    conv_g = [lax.dynamic_slice_in_dim(total[sp[0]:sp[0] + sp[1]], me * dsh, dsh, axis=1)
              for sp in spans_g[len(reps):len(reps) + 2]]
    gc, _ = _pack(conv_g, dsh)
    wc, spans_c = _pack([a_dw, b_conv], dsh)
    mc, _ = _pack([m_a_dw, m_b_conv], dsh)
    vc, _ = _pack([v_a_dw, v_b_conv], dsh)
    conv_out = [gc] + list(_plain_update("conv_update", gc, wc, mc, vc))
    conv_out = [[unpack(buf, sp, w) for sp, w in zip(spans_c, (a_dw, b_conv))] for buf in conv_out]

    at, n = spans_g[0]
    dmod_all = got2[:, at:at + n, :].reshape(N_DEV, depth, 6 * d)
    dmod_cols = jnp.transpose(lax.dynamic_slice_in_dim(dmod_all, me * n_mod, n_mod, axis=2), (1, 0, 2))
    mod_out = _modulation_update("modulation_update", jnp.transpose(c_all), dmod_cols, mod_w, m_mod_w, v_mod_w)
    after = mod_out[3]
    wait_reduction(*reductions[-1])
    update_complete()

    def family(q):
        rep = dict(zip(("mod_b", "pre_mix_g", "post_mix_g", "pre_ffn_g", "post_ffn_g",
                        "a_b1", "a_dwb", "a_ln_g", "a_ln_b", "a_b2"), rep_out[q]))
        return (mod_out[q], rep["mod_b"], rep["pre_mix_g"], rep["post_mix_g"], rep["pre_ffn_g"], rep["post_ffn_g"],
                big_out["a_w1"][q], rep["a_b1"], conv_out[q][0], rep["a_dwb"], rep["a_ln_g"], rep["a_ln_b"],
                big_out["a_w2"][q], rep["a_b2"], big_out["b_w_in"][q], conv_out[q][1], big_out["b_w_out"][q],
                big_out["f_w1"][q], big_out["f_w2"][q])

    return (loss, grad_x, *family(0), *family(1), *family(2), *family(3))
```

```python
import jax
import jax.numpy as jnp
from jax import lax
from jax.experimental import pallas as pl
from jax.experimental.pallas import tpu as pltpu

MXU_DTYPE = jnp.bfloat16
ACT_DTYPE = jnp.bfloat16
COMM_DTYPE = jnp.bfloat16

N_DEV = 8
N_CHIP = 4
RMS_EPS = 1e-6
LN_EPS = 1e-5
ADAM_LR = 0.001
ADAM_B1 = 0.9
ADAM_B2 = 0.999
ADAM_EPS = 1e-08
ADAM_WD = 0.01
ADAM_STEP = 10

V7X_VMEM_LIMIT_BYTES = 56 * 1024 * 1024
LANE = 128
ROW_TILE = 256
CONV_TILE = 128
CONV_HALO = 32
SHORT_HALO = 8
CONV_CHUNK = 256
MM_TM, MM_TN, MM_TK = 1024, 1024, 2048
MM_RESIDENT = 4 * 1024 * 1024

F32 = jnp.float32
MESH = pl.DeviceIdType.MESH


def _pick(dim, pref, mult=LANE):
    if dim <= pref:
        return dim
    t = pref - pref % mult
    while dim % t:
        t -= mult
    return t


def _params(*sem):
    return pltpu.CompilerParams(dimension_semantics=sem, vmem_limit_bytes=V7X_VMEM_LIMIT_BYTES)


def _pos():
    return lax.axis_index("x"), lax.axis_index("y"), lax.axis_index("c")


def _flip(v, bit):
    return 1 - v if bit else v


def _small_gather(name, v, deps=()):
    rows, cols = v.shape
    n_deps = len(deps)

    def body(v_ref, *rest):
        out_ref, send_sems, recv_sems, local_sem = rest[n_deps:]
        x, y, c = _pos()
        me, sibling = (x, y, c), (x, y, 1 - c)
        chips = [(1 - x, y), (x, 1 - y), (1 - x, 1 - y)]

        def block(px, py, pc):
            return out_ref.at[4 * px + 2 * py + pc]

        def copy(k, owner, to, src=None):
            return pltpu.make_async_remote_copy(
                src_ref=block(*owner) if src is None else src, dst_ref=block(*owner), send_sem=send_sems.at[k],
                recv_sem=recv_sems.at[k], device_id=to, device_id_type=MESH)

        mine = pltpu.make_async_copy(v_ref, block(*me), local_sem)
        mine.start()
        first = [copy(0, me, sibling, src=v_ref)]
        first += [copy(1 + j, me, (*chip, c), src=v_ref) for j, chip in enumerate(chips)]
        for cp in first:
            cp.start()
        passed = [copy(4 + j, (*chip, c), sibling) for j, chip in enumerate(chips)]
        for j, chip in enumerate(chips):
            copy(1 + j, (*chip, c), me).wait_recv()
            passed[j].start()
        copy(0, sibling, me).wait_recv()
        for j, chip in enumerate(chips):
            copy(4 + j, (*chip, 1 - c), me).wait_recv()
        for cp in first + passed:
            cp.wait_send()
        mine.wait()

    return pl.pallas_call(
        body, name=name,
        out_shape=jax.ShapeDtypeStruct((N_DEV, rows, cols), v.dtype),
        in_specs=[pl.BlockSpec(memory_space=pltpu.VMEM)] + [pl.BlockSpec(memory_space=pl.ANY)] * n_deps,
        out_specs=pl.BlockSpec(memory_space=pltpu.VMEM),
        scratch_shapes=[pltpu.SemaphoreType.DMA((N_DEV - 1,)), pltpu.SemaphoreType.DMA((N_DEV - 1,)),
                        pltpu.SemaphoreType.DMA],
        compiler_params=pltpu.CompilerParams(vmem_limit_bytes=V7X_VMEM_LIMIT_BYTES),
    )(v, *deps)


HBM_SPEC = pl.BlockSpec(memory_space=pltpu.HBM)
SEM_SPEC = pl.BlockSpec(memory_space=pltpu.SEMAPHORE)
ANY_SPEC = pl.BlockSpec(memory_space=pl.ANY)
TOKEN = jax.ShapeDtypeStruct((8, LANE), F32)
TOKEN_SPEC = pl.BlockSpec(memory_space=pltpu.VMEM)
SPLIT_COPY = pltpu.CompilerParams(has_side_effects=pltpu.SideEffectType.DATAFLOW_SIDE_EFFECTING)


def _hbm(a):
    return pltpu.with_memory_space_constraint(a, pltpu.HBM)


def _dma_sems(n):
    return pltpu.SemaphoreType.DMA((n,))


def _place(name, w, layer, me, by_columns, dtype=None):
    dtype = COMM_DTYPE if dtype is None else dtype
    _, rows, cols = w.shape
    tr = _pick(rows, max(8, (1 << 20) // cols), 8)

    def body(me_ref, w_ref, o_ref):
        o_ref[...] = w_ref[...].astype(o_ref.dtype)

    if by_columns:
        out_spec = pl.BlockSpec((tr, cols), lambda r, me_ref: (r, me_ref[0]))
        out_shape = jax.ShapeDtypeStruct((rows, N_DEV * cols), dtype)
    else:
        out_spec = pl.BlockSpec((None, tr, cols), lambda r, me_ref: (me_ref[0], r, 0))
        out_shape = jax.ShapeDtypeStruct((N_DEV, rows, cols), dtype)
    return pl.pallas_call(
        body, name=name,
        grid_spec=pltpu.PrefetchScalarGridSpec(
            num_scalar_prefetch=1, grid=(rows // tr,),
            in_specs=[pl.BlockSpec((None, tr, cols), lambda r, me_ref: (layer, r, 0))],
            out_specs=out_spec),
        out_shape=out_shape, compiler_params=_params("parallel"),
    )(me, w)


def _block_copy(buf, owner, to, send_sem, recv_sem):
    dev = 4 * owner[0] + 2 * owner[1] + owner[2]
    if len(buf.shape) == 3:
        blk = buf.at[dev]
    else:
        n = buf.shape[1] // N_DEV
        blk = buf.at[:, pl.ds(pl.multiple_of(dev * n, n), n)]
    return pltpu.make_async_remote_copy(src_ref=blk, dst_ref=blk, send_sem=send_sem, recv_sem=recv_sem,
                                        device_id=to, device_id_type=MESH)


def _other_chips(x, y):
    return [(1 - x, y), (x, 1 - y), (1 - x, 1 - y)]


def _gather_start(tag, bufs, after=()):
    n = len(bufs)
    n_in = n + len(after)

    def body(*refs):
        b = refs[:n]
        send, recv_ici, recv_sib = refs[n_in:n_in + 3]
        token = refs[-1]
        x, y, c = _pos()
        me = (x, y, c)
        for i in range(n):
            _block_copy(b[i], me, (x, y, 1 - c), send.at[4 * i], recv_sib.at[i]).start()
            for j, chip in enumerate(_other_chips(x, y)):
                _block_copy(b[i], me, (*chip, c), send.at[4 * i + 1 + j], recv_ici.at[3 * i + j]).start()
        token[...] = jnp.zeros_like(token)

    outs = pl.pallas_call(
        body, name="gather_start_" + tag,
        out_shape=(_dma_sems(4 * n), _dma_sems(3 * n), _dma_sems(n),
                   *[pltpu.HBM(b.shape, b.dtype) for b in bufs], TOKEN),
        in_specs=[HBM_SPEC] * n + [ANY_SPEC] * len(after),
        out_specs=(SEM_SPEC, SEM_SPEC, SEM_SPEC, *[HBM_SPEC] * n, TOKEN_SPEC),
        input_output_aliases={i: 3 + i for i in range(n)}, compiler_params=SPLIT_COPY,
    )(*[_hbm(b) for b in bufs], *after)
    return dict(send=outs[0], recv_ici=outs[1], recv_sib=outs[2], bufs=list(outs[3:3 + n]), token=outs[-1])


def _gather_pass(tag, h, after):
    bufs = h["bufs"]
    n = len(bufs)

    def body(*refs):
        b = refs[:n]
        recv_ici = refs[n]
        fsend, frecv = refs[n + 2:n + 4]
        token = refs[-1]
        x, y, c = _pos()
        for i in range(n):
            for j, chip in enumerate(_other_chips(x, y)):
                _block_copy(b[i], (*chip, c), (x, y, c), fsend.at[3 * i + j], recv_ici.at[3 * i + j]).wait_recv()
                _block_copy(b[i], (*chip, c), (x, y, 1 - c), fsend.at[3 * i + j], frecv.at[3 * i + j]).start()
        token[...] = jnp.zeros_like(token)

    outs = pl.pallas_call(
        body, name="gather_pass_" + tag,
        out_shape=(_dma_sems(3 * n), _dma_sems(3 * n), *[pltpu.HBM(b.shape, b.dtype) for b in bufs], TOKEN),
        in_specs=[HBM_SPEC] * n + [SEM_SPEC, ANY_SPEC],
        out_specs=(SEM_SPEC, SEM_SPEC, *[HBM_SPEC] * n, TOKEN_SPEC),
        input_output_aliases={i: 2 + i for i in range(n)}, compiler_params=SPLIT_COPY,
    )(*bufs, h["recv_ici"], after)
    return dict(h, fsend=outs[0], frecv=outs[1], bufs=list(outs[2:2 + n]), token=outs[-1])


def _gather_wait(tag, h, after):
    bufs = h["bufs"]
    n = len(bufs)

    def body(*refs):
        b = refs[:n]
        send, recv_sib, fsend, frecv = refs[n:n + 4]
        x, y, c = _pos()
        me, sibling = (x, y, c), (x, y, 1 - c)
        for i in range(n):
            _block_copy(b[i], sibling, me, send.at[4 * i], recv_sib.at[i]).wait_recv()
            for j, chip in enumerate(_other_chips(x, y)):
                _block_copy(b[i], (*chip, 1 - c), me, fsend.at[3 * i + j], frecv.at[3 * i + j]).wait_recv()
            for k in range(4):
                _block_copy(b[i], me, sibling, send.at[4 * i + k], recv_sib.at[i]).wait_send()
            for j, chip in enumerate(_other_chips(x, y)):
                _block_copy(b[i], (*chip, c), sibling, fsend.at[3 * i + j], frecv.at[3 * i + j]).wait_send()

    outs = pl.pallas_call(
        body, name="gather_wait_" + tag,
        out_shape=tuple(pltpu.HBM(b.shape, b.dtype) for b in bufs),
        in_specs=[HBM_SPEC] * n + [SEM_SPEC] * 4 + [ANY_SPEC], out_specs=tuple([HBM_SPEC] * n),
        input_output_aliases={i: i for i in range(n)}, compiler_params=SPLIT_COPY,
    )(*bufs, h["send"], h["recv_sib"], h["fsend"], h["frecv"], after)
    return list(outs)


def _sibling_copy(grad, land, send_sem, recv_sem):
    x, y, c = _pos()
    return pltpu.make_async_remote_copy(
        src_ref=grad.at[:, 1 - c], dst_ref=land, send_sem=send_sem, recv_sem=recv_sem,
        device_id=(x, y, 1 - c), device_id_type=MESH)


def _sibling_start(tag, grads):
    n = len(grads)
    landings = [lax.empty((N_CHIP,) + g.shape[2:], g.dtype) for g in grads]

    def body(*refs):
        g, land = refs[:n], refs[n:2 * n]
        send, recv = refs[2 * n:2 * n + 2]
        token = refs[-1]
        for i in range(n):
            _sibling_copy(g[i], land[i], send.at[i], recv.at[i]).start()
        token[...] = jnp.zeros_like(token)

    both = list(grads) + landings
    outs = pl.pallas_call(
        body, name="sibling_start_" + tag,
        out_shape=(_dma_sems(n), _dma_sems(n), *[pltpu.HBM(b.shape, b.dtype) for b in both], TOKEN),
        in_specs=[HBM_SPEC] * (2 * n), out_specs=(SEM_SPEC, SEM_SPEC, *[HBM_SPEC] * (2 * n), TOKEN_SPEC),
        input_output_aliases={i: 2 + i for i in range(2 * n)}, compiler_params=SPLIT_COPY,
    )(*[_hbm(b) for b in both])
    return dict(send=outs[0], recv=outs[1], grads=list(outs[2:2 + n]), landings=list(outs[2 + n:2 + 2 * n]),
                token=outs[-1])


def _sibling_wait(tag, h, after):
    n = len(h["grads"])

    def body(*refs):
        g, land = refs[:n], refs[n:2 * n]
        send, recv = refs[2 * n:2 * n + 2]
        for i in range(n):
            _sibling_copy(g[i], land[i], send.at[i], recv.at[i]).wait()

    both = h["grads"] + h["landings"]
    outs = pl.pallas_call(
        body, name="sibling_wait_" + tag,
        out_shape=tuple(pltpu.HBM(b.shape, b.dtype) for b in both),
        in_specs=[HBM_SPEC] * (2 * n) + [SEM_SPEC, SEM_SPEC, ANY_SPEC], out_specs=tuple([HBM_SPEC] * (2 * n)),
        input_output_aliases={i: i for i in range(2 * n)}, compiler_params=SPLIT_COPY,
    )(*both, h["send"], h["recv"], after)
    return list(outs[:n]), list(outs[n:])


def _reduce_copy(part, land, x, y, c, k, src_chip, send_sem, recv_sem):
    px, py = _flip(x, k >> 1 & 1), _flip(y, k & 1)
    return pltpu.make_async_remote_copy(src_ref=part.at[2 * px + py], dst_ref=land.at[src_chip], send_sem=send_sem,
                                        recv_sem=recv_sem, device_id=(px, py, c), device_id_type=MESH)


def _reduce_start(tag, partials, landings):
    n = len(partials)

    def body(*refs):
        p, land = refs[:n], refs[n:2 * n]
        send, recv = refs[2 * n:2 * n + 2]
        token = refs[-1]
        x, y, c = _pos()
        for i in range(n):
            for k in range(1, N_CHIP):
                _reduce_copy(p[i], land[i], x, y, c, k, 2 * x + y, send.at[3 * i + k - 1], recv.at[3 * i + k - 1]).start()
        token[...] = jnp.zeros_like(token)

    both = list(partials) + list(landings)
    outs = pl.pallas_call(
        body, name="reduce_start_" + tag,
        out_shape=(_dma_sems(3 * n), _dma_sems(3 * n), *[pltpu.HBM(b.shape, b.dtype) for b in both], TOKEN),
        in_specs=[HBM_SPEC] * (2 * n), out_specs=(SEM_SPEC, SEM_SPEC, *[HBM_SPEC] * (2 * n), TOKEN_SPEC),
        input_output_aliases={i: 2 + i for i in range(2 * n)}, compiler_params=SPLIT_COPY,
    )(*[_hbm(b) for b in both])
    return dict(send=outs[0], recv=outs[1], partials=list(outs[2:2 + n]), landings=list(outs[2 + n:2 + 2 * n]),
                token=outs[-1])


def _reduce_wait(tag, h, after):
    n = len(h["partials"])

    def body(*refs):
        p, land = refs[:n], refs[n:2 * n]
        send, recv = refs[2 * n:2 * n + 2]
        x, y, c = _pos()
        for i in range(n):
            for k in range(1, N_CHIP):
                src_chip = 2 * _flip(x, k >> 1 & 1) + _flip(y, k & 1)
                cp = _reduce_copy(p[i], land[i], x, y, c, k, src_chip, send.at[3 * i + k - 1], recv.at[3 * i + k - 1])
                cp.wait_recv()
                cp.wait_send()

    both = h["partials"] + h["landings"]
    outs = pl.pallas_call(
        body, name="reduce_wait_" + tag,
        out_shape=tuple(pltpu.HBM(b.shape, b.dtype) for b in both),
        in_specs=[HBM_SPEC] * (2 * n) + [SEM_SPEC, SEM_SPEC, ANY_SPEC], out_specs=tuple([HBM_SPEC] * (2 * n)),
        input_output_aliases={i: i for i in range(2 * n)}, compiler_params=SPLIT_COPY,
    )(*both, h["send"], h["recv"], after)
    return list(outs[n:])


def _chip_partial(name, grad, theirs, where):
    _, _, rows, cols = grad.shape
    tr = _pick(rows, max(8, (1 << 20) // cols), 8)
    landing = lax.empty(theirs.shape, theirs.dtype)

    def body(where_ref, g_ref, t_ref, land_in, o_ref, land_ref):
        total = (g_ref[...].astype(F32) + t_ref[...].astype(F32)).astype(o_ref.dtype)
        o_ref[...] = total

        @pl.when(pl.program_id(1) == where_ref[1])
        def _():
            land_ref[...] = total

    return pl.pallas_call(
        body, name=name,
        grid_spec=pltpu.PrefetchScalarGridSpec(
            num_scalar_prefetch=1, grid=(rows // tr, N_CHIP),
            in_specs=[pl.BlockSpec((None, None, tr, cols), lambda r, k, w: (k, w[0], r, 0)),
                      pl.BlockSpec((None, tr, cols), lambda r, k, w: (k, r, 0)), ANY_SPEC],
            out_specs=[pl.BlockSpec((None, tr, cols), lambda r, k, w: (k, r, 0)),
                       pl.BlockSpec((None, tr, cols), lambda r, k, w: (w[1], r, 0))]),
        out_shape=[jax.ShapeDtypeStruct(theirs.shape, theirs.dtype)] * 2,
        input_output_aliases={3: 1},
        compiler_params=_params("parallel", "arbitrary"),
    )(where, grad, theirs, landing)


def _matmul(name, a, b, a_spec, b_spec, grid, acc_shape, out_shape, out_specs, epilogue,
            ta=False, tb=False, extras=(), extra_specs=(), deps=()):
    nk = grid[2]
    n_extra = len(extras)
    n_in = 2 + n_extra + len(deps)
    dims = (((0,) if ta else (1,), (1,) if tb else (0,)), ((), ()))

    def body(*refs):
        a_ref, b_ref = refs[:2]
        extra_refs = refs[2:2 + n_extra]
        out_refs = refs[n_in:n_in + len(out_shape)]

        def product():
            return lax.dot_general(a_ref[...].astype(MXU_DTYPE), b_ref[...].astype(MXU_DTYPE), dims,
                                   preferred_element_type=F32)

        if nk == 1:
            epilogue(product(), extra_refs, out_refs)
            return
        acc_ref = refs[-1]
        k = pl.program_id(2)

        @pl.when(k == 0)
        def _():
            acc_ref[...] = product()

        if nk > 2:
            @pl.when((k > 0) & (k < nk - 1))
            def _():
                acc_ref[...] += product()

        @pl.when(k == nk - 1)
        def _():
            epilogue(acc_ref[...] + product(), extra_refs, out_refs)

    return pl.pallas_call(
        body, name=name, grid=grid,
        in_specs=[a_spec, b_spec, *extra_specs, *[ANY_SPEC] * len(deps)], out_specs=out_specs, out_shape=out_shape,
        scratch_shapes=[pltpu.VMEM(acc_shape, F32)] if nk > 1 else [],
        compiler_params=_params("parallel", "parallel", "arbitrary"),
    )(a, b, *extras, *deps)


def _store(acc, extra_refs, out_refs):
    out_refs[0][...] = acc.astype(out_refs[0].dtype)


def _store_bias(acc, extra_refs, out_refs):
    out_refs[0][...] = (acc + extra_refs[0][...]).astype(out_refs[0].dtype)


def _store_relu2(acc, extra_refs, out_refs):
    r = jnp.maximum(acc, 0.0)
    out_refs[0][...] = (r * r).astype(out_refs[0].dtype)
    out_refs[1][...] = acc.astype(out_refs[1].dtype)


def _store_relu2_grad(acc, extra_refs, out_refs):
    hid = extra_refs[0][...].astype(F32)
    out_refs[0][...] = (acc * (2.0 * jnp.maximum(hid, 0.0))).astype(out_refs[0].dtype)


def _mm_cols(name, a, wc, out_dtype, epilogue=_store, bias=None, n_out=1, deps=()):
    s, kdim = a.shape
    n = wc.shape[1]
    tm, tn, tk = _pick(s, MM_TM), _pick(n, MM_TN), _pick(kdim, MM_TK)
    if kdim <= MM_TK:
        tn = _pick(n, 2 * MM_TN)
    extras, extra_specs = (), ()
    if bias is not None:
        extras, extra_specs = (bias,), (pl.BlockSpec((1, tn), lambda i, j, k: (0, j)),)
    out = jax.ShapeDtypeStruct((s, n), out_dtype)
    spec = pl.BlockSpec((tm, tn), lambda i, j, k: (i, j))
    return _matmul(
        name, a, wc,
        pl.BlockSpec((tm, tk), lambda i, j, k: (i, k)),
        pl.BlockSpec((tk, tn), lambda i, j, k: (k, j)),
        (s // tm, n // tn, kdim // tk), (tm, tn),
        [out] * n_out, [spec] * n_out, epilogue, extras=extras, extra_specs=extra_specs, deps=deps)


def _mm_rows(name, a, wr, out_dtype, bias=None, deps=()):
    s, kdim = a.shape
    w = wr.reshape(kdim, wr.shape[2])
    n = w.shape[1]
    tm, tn, tk = _pick(s, MM_TM), _pick(n, MM_TN), _pick(kdim, MM_TK)
    if kdim * n <= MM_RESIDENT:
        tm, tn = _pick(s, MM_TM // 2), n
    extras, extra_specs, epilogue = (), (), _store
    if bias is not None:
        extras, extra_specs, epilogue = (bias,), (pl.BlockSpec((1, tn), lambda i, j, k: (0, j)),), _store_bias
    return _matmul(
        name, a, w,
        pl.BlockSpec((tm, tk), lambda i, j, k: (i, k)),
        pl.BlockSpec((tk, tn), lambda i, j, k: (k, j)),
        (s // tm, n // tn, kdim // tk), (tm, tn),
        [jax.ShapeDtypeStruct((s, n), out_dtype)], [pl.BlockSpec((tm, tn), lambda i, j, k: (i, j))],
        epilogue, extras=extras, extra_specs=extra_specs, deps=deps)[0]


def _mm_cols_t(name, dy, wc, out_dtype, deps=()):
    s, n = dy.shape
    kdim = wc.shape[0]
    tm, tn, tk = _pick(s, MM_TM), _pick(kdim, MM_TN), _pick(n, MM_TK)
    return _matmul(
        name, dy, wc,
        pl.BlockSpec((tm, tk), lambda i, j, k: (i, k)),
        pl.BlockSpec((tn, tk), lambda i, j, k: (j, k)),
        (s // tm, kdim // tn, n // tk), (tm, tn),
        [jax.ShapeDtypeStruct((s, kdim), out_dtype)], [pl.BlockSpec((tm, tn), lambda i, j, k: (i, j))],
        _store, tb=True, deps=deps)[0]


def _mm_rows_t(name, dy, wr, out_dtype, epilogue=_store, extra=None, deps=()):
    s, n = dy.shape
    w = wr.reshape(-1, n)
    kdim = w.shape[0]
    tm, tn, tk = _pick(s, MM_TM), _pick(kdim, MM_TN), _pick(n, MM_TK)
    if kdim * n <= MM_RESIDENT:
        tm, tn = _pick(s, MM_TM // 2), kdim
    elif n <= MM_TK:
        tn = _pick(kdim, 2 * MM_TN)
    extras, extra_specs = (), ()
    if extra is not None:
        extras, extra_specs = (extra,), (pl.BlockSpec((tm, tn), lambda i, j, k: (i, j)),)
    return _matmul(
        name, dy, w,
        pl.BlockSpec((tm, tk), lambda i, j, k: (i, k)),
        pl.BlockSpec((tn, tk), lambda i, j, k: (j, k)),
        (s // tm, kdim // tn, n // tk), (tm, tn),
        [jax.ShapeDtypeStruct((s, kdim), out_dtype)], [pl.BlockSpec((tm, tn), lambda i, j, k: (i, j))],
        epilogue, tb=True, extras=extras, extra_specs=extra_specs, deps=deps)[0]


def _grad_cols(name, h, dy, n):
    s, kdim = h.shape
    tm, tn, tk = _pick(kdim, MM_TM), _pick(n, MM_TN), _pick(s, MM_TK)
    per = n // tn
    return _matmul(
        name, h, dy,
        pl.BlockSpec((tk, tm), lambda i, j, k: (k, i)),
        pl.BlockSpec((tk, tn), lambda i, j, k: (k, j)),
        (kdim // tm, N_DEV * per, s // tk), (tm, tn),
        [jax.ShapeDtypeStruct((N_DEV, kdim, n), COMM_DTYPE)],
        [pl.BlockSpec((None, tm, tn), lambda i, j, k: (j // per, i, j % per))],
        _store, ta=True)[0].reshape(N_CHIP, 2, kdim, n)


def _grad_rows(name, v, dy, kk):
    s, kdim = v.shape
    n = dy.shape[1]
    tm, tn, tk = _pick(kdim, MM_TM), _pick(n, MM_TN), _pick(s, MM_TK)
    return _matmul(
        name, v, dy,
        pl.BlockSpec((tk, tm), lambda i, j, k: (k, i)),
        pl.BlockSpec((tk, tn), lambda i, j, k: (k, j)),
        (kdim // tm, n // tn, s // tk), (tm, tn),
        [jax.ShapeDtypeStruct((kdim, n), COMM_DTYPE)],
        [pl.BlockSpec((tm, tn), lambda i, j, k: (i, j))],
        _store, ta=True)[0].reshape(N_CHIP, 2, kk, n)


def _rms(v):
    return lax.rsqrt(jnp.mean(v * v, axis=-1, keepdims=True) + RMS_EPS)


def _colsum(v):
    return jnp.sum(v, axis=0, keepdims=True)


def _vec_spec(width):
    return pl.BlockSpec((1, width), lambda i: (0, 0))


def _residual_fwd(name, x, post=None, pre=None, deps=()):
    s, d = x.shape
    ts = _pick(s, ROW_TILE, 8)
    row = pl.BlockSpec((ts, d), lambda i: (i, 0))
    ins, specs = [x], [row]
    if post is not None:
        ins += list(post)
        specs += [row, _vec_spec(d), _vec_spec(d)]
    if pre is not None:
        ins += list(pre)
        specs += [_vec_spec(d)] * 3
    ins += list(deps)
    specs += [ANY_SPEC] * len(deps)
    outs, out_specs = [], []
    if post is not None:
        outs.append(jax.ShapeDtypeStruct((s, d), F32))
        out_specs.append(row)
    if pre is not None:
        outs.append(jax.ShapeDtypeStruct((s, d), ACT_DTYPE))
        out_specs.append(row)

    def body(*refs):
        refs = list(refs)
        xv = refs.pop(0)[...]
        if post is not None:
            y_ref, gp_ref, gt_ref = refs[:3]
            del refs[:3]
        if pre is not None:
            g_ref, sc_ref, sh_ref = refs[:3]
            del refs[:3]
        del refs[:len(deps)]
        if post is not None:
            yv = y_ref[...]
            xv = xv + gt_ref[...] * ((yv * _rms(yv)) * gp_ref[...])
            refs.pop(0)[...] = xv
        if pre is not None:
            hv = ((xv * _rms(xv)) * g_ref[...]) * (1.0 + sc_ref[...]) + sh_ref[...]
            refs.pop(0)[...] = hv.astype(ACT_DTYPE)

    return pl.pallas_call(
        body, name=name, grid=(s // ts,), in_specs=specs, out_specs=out_specs, out_shape=outs,
        compiler_params=_params("parallel"),
    )(*ins)


def _loss_junction(name, x, y, g_post, gt, target):
    s, d = x.shape
    ts = _pick(s, ROW_TILE, 8)
    row = pl.BlockSpec((ts, d), lambda i: (i, 0))

    def body(x_ref, y_ref, gpost_ref, gt_ref, t_ref, g_ref, dy_ref, loss_ref, sums_ref):
        @pl.when(pl.program_id(0) == 0)
        def _():
            loss_ref[...] = jnp.zeros_like(loss_ref)
            sums_ref[...] = jnp.zeros_like(sums_ref)

        yv = y_ref[...]
        r = _rms(yv)
        nrm = yv * r
        rn = nrm * gpost_ref[...]
        err = (x_ref[...] + gt_ref[...] * rn) - t_ref[...]
        loss_ref[...] += (0.5 / d) * jnp.sum(err * err)
        g = err * (1.0 / d)
        g_ref[...] = g
        sums_ref[3:4, :] += _colsum(g * rn)
        d_o = g * gt_ref[...]
        sums_ref[4:5, :] += _colsum(d_o * nrm)
        dn = d_o * gpost_ref[...]
        dy = r * (dn - nrm * jnp.mean(dn * nrm, axis=-1, keepdims=True))
        sums_ref[5:6, :] += _colsum(dy)
        dy_ref[...] = dy.astype(dy_ref.dtype)

    return pl.pallas_call(
        body, name=name, grid=(s // ts,),
        in_specs=[row, row, _vec_spec(d), _vec_spec(d), row],
        out_specs=[row, row, pl.BlockSpec((8, LANE), lambda i: (0, 0)), pl.BlockSpec((8, d), lambda i: (0, 0))],
        out_shape=[jax.ShapeDtypeStruct((s, d), F32), jax.ShapeDtypeStruct((s, d), ACT_DTYPE),
                   jax.ShapeDtypeStruct((8, LANE), F32), jax.ShapeDtypeStruct((8, d), F32)],
        compiler_params=_params("arbitrary"),
    )(x, y, g_post, gt, target)


def _residual_bwd(name, g_out, pre=None, post=None):
    s, d = g_out.shape
    ts = _pick(s, ROW_TILE, 8)
    row = pl.BlockSpec((ts, d), lambda i: (i, 0))
    ins, specs = [g_out], [row]
    outs, out_specs = [], []
    if pre is not None:
        ins += list(pre)
        specs += [row, row, _vec_spec(d), _vec_spec(d)]
        outs.append(jax.ShapeDtypeStruct((s, d), F32))
        out_specs.append(row)
    if post is not None:
        ins += list(post)
        specs += [row, _vec_spec(d), _vec_spec(d)]
        outs.append(jax.ShapeDtypeStruct((s, d), ACT_DTYPE))
        out_specs.append(row)
    outs.append(jax.ShapeDtypeStruct((8, d), F32))
    out_specs.append(pl.BlockSpec((8, d), lambda i: (0, 0)))

    def body(*refs):
        refs = list(refs)
        g = refs.pop(0)[...]
        if pre is not None:
            dh_ref, x_ref, gpre_ref, sc_ref = refs[:4]
            del refs[:4]
        if post is not None:
            y_ref, gpost_ref, gt_ref = refs[:3]
            del refs[:3]
        sums_ref = refs[-1]

        @pl.when(pl.program_id(0) == 0)
        def _():
            sums_ref[...] = jnp.zeros_like(sums_ref)

        if pre is not None:
            dh, xv = dh_ref[...], x_ref[...]
            r = _rms(xv)
            nrm = xv * r
            d_rn = dh * (1.0 + sc_ref[...])
            sums_ref[0:1, :] += _colsum(dh)
            sums_ref[1:2, :] += _colsum(dh * (nrm * gpre_ref[...]))
            sums_ref[2:3, :] += _colsum(d_rn * nrm)
            dn = d_rn * gpre_ref[...]
            g = g + r * (dn - nrm * jnp.mean(dn * nrm, axis=-1, keepdims=True))
            refs.pop(0)[...] = g
        if post is not None:
            yv = y_ref[...]
            r = _rms(yv)
            nrm = yv * r
            sums_ref[3:4, :] += _colsum(g * (nrm * gpost_ref[...]))
            d_o = g * gt_ref[...]
            sums_ref[4:5, :] += _colsum(d_o * nrm)
            dn = d_o * gpost_ref[...]
            dy = r * (dn - nrm * jnp.mean(dn * nrm, axis=-1, keepdims=True))
            sums_ref[5:6, :] += _colsum(dy)
            refs.pop(0)[...] = dy.astype(ACT_DTYPE)

    return pl.pallas_call(
        body, name=name, grid=(s // ts,), in_specs=specs, out_specs=out_specs, out_shape=outs,
        compiler_params=_params("arbitrary"),
    )(*ins)


def _chunks(d):
    cw = min(CONV_CHUNK, d)
    return [(c * cw, cw) for c in range(d // cw)]


def _glu_of(u_ref, d):
    return u_ref[:, :d].astype(F32) * jax.nn.sigmoid(u_ref[:, d:].astype(F32))


def _gated_input(z_ref, d):
    return z_ref[:, d:2 * d].astype(F32) * z_ref[:, 2 * d:].astype(F32)


def _shift_rows(buf, sh, c0, cw, rows):
    for r in range(1, 8):
        sh[r - 1] = buf[r:r + rows, c0:c0 + cw]


def _window(buf, sh, c0, cw, offset, ts):
    q, r = divmod(offset, 8)
    if r == 0:
        return buf[8 * q:8 * q + ts, c0:c0 + cw]
    return sh[r - 1, 8 * q:8 * q + ts, :]


def _glu_conv_ln_swish(name, u, dw, dwb, ln_g, ln_b):
    s, d2 = u.shape
    d = d2 // 2
    taps = dw.shape[0]
    ts, halo = _pick(s, CONV_TILE, 8), CONV_HALO
    lead = halo - (taps - 1)
    per = ts // halo
    cw = min(CONV_CHUNK, d)

    def body(cur_ref, prev_ref, dw_ref, dwb_ref, g_ref, b_ref, cv_ref, v_ref, buf, sh):
        i = pl.program_id(0)
        buf[0:halo, :] = jnp.where(i > 0, _glu_of(prev_ref, d), 0.0)
        buf[halo:, :] = _glu_of(cur_ref, d)
        for c0, _ in _chunks(d):
            _shift_rows(buf, sh, c0, cw, ts + halo - 8)
            acc = jnp.zeros((ts, cw), F32)
            for k in range(taps):
                acc += dw_ref[k:k + 1, c0:c0 + cw] * _window(buf, sh, c0, cw, lead + k, ts)
            cv_ref[:, c0:c0 + cw] = acc + dwb_ref[:, c0:c0 + cw]
        cv = cv_ref[...]
        mu = jnp.mean(cv, axis=-1, keepdims=True)
        xc = cv - mu
        var = jnp.mean(xc * xc, axis=-1, keepdims=True)
        ln = (xc * lax.rsqrt(var + LN_EPS)) * g_ref[...] + b_ref[...]
        v_ref[...] = (ln * jax.nn.sigmoid(ln)).astype(v_ref.dtype)

    row = pl.BlockSpec((ts, d), lambda i: (i, 0))
    return pl.pallas_call(
        body, name=name, grid=(s // ts,),
        in_specs=[pl.BlockSpec((ts, d2), lambda i: (i, 0)),
                  pl.BlockSpec((halo, d2), lambda i: (jnp.maximum(i * per - 1, 0), 0)),
                  pl.BlockSpec((taps, d), lambda i: (0, 0)), _vec_spec(d), _vec_spec(d), _vec_spec(d)],
        out_specs=[row, row],
        out_shape=[jax.ShapeDtypeStruct((s, d), F32), jax.ShapeDtypeStruct((s, d), ACT_DTYPE)],
        scratch_shapes=[pltpu.VMEM((ts + halo, d), F32), pltpu.VMEM((7, ts + halo - 8, cw), F32)],
        compiler_params=_params("parallel"),
    )(u, u, dw, dwb, ln_g, ln_b)


def _ln_swish_bwd(name, dv, cv, ln_g, ln_b, deps=()):
    s, d = cv.shape
    ts = _pick(s, ROW_TILE, 8)

    def body(dv_ref, cv_ref, g_ref, b_ref, *rest):
        dcv_ref, sums_ref = rest[len(deps):]

        @pl.when(pl.program_id(0) == 0)
        def _():
            sums_ref[...] = jnp.zeros_like(sums_ref)

        cv = cv_ref[...]
        mu = jnp.mean(cv, axis=-1, keepdims=True)
        xc = cv - mu
        rstd = lax.rsqrt(jnp.mean(xc * xc, axis=-1, keepdims=True) + LN_EPS)
        nhat = xc * rstd
        ln = nhat * g_ref[...] + b_ref[...]
        sg = jax.nn.sigmoid(ln)
        dl = dv_ref[...] * (sg * (1.0 + ln * (1.0 - sg)))
        sums_ref[0:1, :] += _colsum(dl * nhat)
        sums_ref[1:2, :] += _colsum(dl)
        dn = dl * g_ref[...]
        dcv = rstd * (dn - jnp.mean(dn, axis=-1, keepdims=True)
                      - nhat * jnp.mean(dn * nhat, axis=-1, keepdims=True))
        sums_ref[2:3, :] += _colsum(dcv)
        dcv_ref[...] = dcv

    row = pl.BlockSpec((ts, d), lambda i: (i, 0))
    return pl.pallas_call(
        body, name=name, grid=(s // ts,),
        in_specs=[row, row, _vec_spec(d), _vec_spec(d)] + [ANY_SPEC] * len(deps),
        out_specs=[row, pl.BlockSpec((8, d), lambda i: (0, 0))],
        out_shape=[jax.ShapeDtypeStruct((s, d), F32), jax.ShapeDtypeStruct((8, d), F32)],
        compiler_params=_params("arbitrary"),
    )(dv, cv, ln_g, ln_b, *deps)


def _conv_glu_bwd(name, dcv, u, dw):
    s, d = dcv.shape
    taps = dw.shape[0]
    taps8 = -(-taps // 8) * 8
    ts, halo = _pick(s, CONV_TILE, 8), CONV_HALO
    lead = halo - (taps - 1)
    per = ts // halo
    n_tiles = s // ts
    cw = min(CONV_CHUNK, d)

    def body(dcv_ref, next_ref, u_ref, prev_ref, dw_ref, du_ref, ddw_ref, sums_ref, nbuf, pbuf, nsh, psh):
        i = pl.program_id(0)

        @pl.when(i == 0)
        def _():
            ddw_ref[...] = jnp.zeros_like(ddw_ref)
            sums_ref[...] = jnp.zeros_like(sums_ref)

        nbuf[0:ts, :] = dcv_ref[...]
        nbuf[ts:, :] = jnp.where(i < n_tiles - 1, next_ref[...], 0.0)
        pbuf[0:halo, :] = jnp.where(i > 0, _glu_of(prev_ref, d), 0.0)
        pbuf[halo:, :] = _glu_of(u_ref, d)
        for c0, _ in _chunks(d):
            _shift_rows(nbuf, nsh, c0, cw, ts + halo - 8)
            _shift_rows(pbuf, psh, c0, cw, ts + halo - 8)
            dcv = dcv_ref[:, c0:c0 + cw]
            dglu = jnp.zeros((ts, cw), F32)
            for k in range(taps):
                dglu += dw_ref[k:k + 1, c0:c0 + cw] * _window(nbuf, nsh, c0, cw, taps - 1 - k, ts)
                ddw_ref[k:k + 1, c0:c0 + cw] += _colsum(dcv * _window(pbuf, psh, c0, cw, lead + k, ts))
            a = u_ref[:, c0:c0 + cw].astype(F32)
            sg = jax.nn.sigmoid(u_ref[:, d + c0:d + c0 + cw].astype(F32))
            da = dglu * sg
            dg = dglu * a * (sg * (1.0 - sg))
            du_ref[:, c0:c0 + cw] = da.astype(du_ref.dtype)
            du_ref[:, d + c0:d + c0 + cw] = dg.astype(du_ref.dtype)
            sums_ref[0:1, c0:c0 + cw] += _colsum(da)
            sums_ref[0:1, d + c0:d + c0 + cw] += _colsum(dg)

    row = pl.BlockSpec((ts, d), lambda i: (i, 0))
    wide = pl.BlockSpec((ts, 2 * d), lambda i: (i, 0))
    return pl.pallas_call(
        body, name=name, grid=(n_tiles,),
        in_specs=[row, pl.BlockSpec((halo, d), lambda i: (jnp.minimum((i + 1) * per, s // halo - 1), 0)),
                  wide, pl.BlockSpec((halo, 2 * d), lambda i: (jnp.maximum(i * per - 1, 0), 0)),
                  pl.BlockSpec((taps, d), lambda i: (0, 0))],
        out_specs=[wide, pl.BlockSpec((taps8, d), lambda i: (0, 0)), pl.BlockSpec((8, 2 * d), lambda i: (0, 0))],
        out_shape=[jax.ShapeDtypeStruct((s, 2 * d), ACT_DTYPE), jax.ShapeDtypeStruct((taps8, d), F32),
                   jax.ShapeDtypeStruct((8, 2 * d), F32)],
        scratch_shapes=[pltpu.VMEM((ts + halo, d), F32), pltpu.VMEM((ts + halo, d), F32),
                        pltpu.VMEM((7, ts + halo - 8, cw), F32), pltpu.VMEM((7, ts + halo - 8, cw), F32)],
        compiler_params=_params("arbitrary"),
    )(dcv, dcv, u, u, dw)


def _short_conv_fwd(name, z, w):
    s, d3 = z.shape
    d = d3 // 3
    taps = w.shape[0]
    ts, halo = _pick(s, CONV_TILE, 8), SHORT_HALO
    lead = halo - (taps - 1)
    per = ts // halo

    def body(z_ref, prev_ref, w_ref, q_ref, pbuf):
        i = pl.program_id(0)
        pbuf[0:halo, :] = jnp.where(i > 0, _gated_input(prev_ref, d), 0.0)
        pbuf[halo:, :] = _gated_input(z_ref, d)
        for c0, cw in _chunks(d):
            acc = jnp.zeros((ts, cw), F32)
            for k in range(taps):
                acc += w_ref[k:k + 1, c0:c0 + cw] * pbuf[lead + k:lead + k + ts, c0:c0 + cw]
            q_ref[:, c0:c0 + cw] = (z_ref[:, c0:c0 + cw].astype(F32) * acc).astype(q_ref.dtype)

    return pl.pallas_call(
        body, name=name, grid=(s // ts,),
        in_specs=[pl.BlockSpec((ts, d3), lambda i: (i, 0)),
                  pl.BlockSpec((halo, d3), lambda i: (jnp.maximum(i * per - 1, 0), 0)),
                  pl.BlockSpec((taps, d), lambda i: (0, 0))],
        out_specs=pl.BlockSpec((ts, d), lambda i: (i, 0)),
        out_shape=jax.ShapeDtypeStruct((s, d), ACT_DTYPE),
        scratch_shapes=[pltpu.VMEM((ts + halo, d), F32)],
        compiler_params=_params("parallel"),
    )(z, z, w)


def _short_conv_bwd(name, dq, z, w, deps=()):
    s, d3 = z.shape
    d = d3 // 3
    taps = w.shape[0]
    ts, halo = _pick(s, CONV_TILE, 8), SHORT_HALO
    lead = halo - (taps - 1)
    per = ts // halo
    n_tiles = s // ts

    def body(dq_ref, dqn_ref, z_ref, zp_ref, zn_ref, w_ref, *rest):
        dz_ref, sums_ref, pbuf, ubuf = rest[len(deps):]
        i = pl.program_id(0)

        @pl.when(i == 0)
        def _():
            sums_ref[...] = jnp.zeros_like(sums_ref)

        pbuf[0:halo, :] = jnp.where(i > 0, _gated_input(zp_ref, d), 0.0)
        pbuf[halo:, :] = _gated_input(z_ref, d)
        ubuf[0:ts, :] = dq_ref[...] * z_ref[:, 0:d].astype(F32)
        ubuf[ts:, :] = jnp.where(i < n_tiles - 1, dqn_ref[...] * zn_ref[:, 0:d].astype(F32), 0.0)
        for c0, cw in _chunks(d):
            du = ubuf[0:ts, c0:c0 + cw]
            conv = jnp.zeros((ts, cw), F32)
            dp = jnp.zeros((ts, cw), F32)
            for k in range(taps):
                wk = w_ref[k:k + 1, c0:c0 + cw]
                shifted = pbuf[lead + k:lead + k + ts, c0:c0 + cw]
                conv += wk * shifted
                dp += wk * ubuf[taps - 1 - k:taps - 1 - k + ts, c0:c0 + cw]
                sums_ref[k:k + 1, c0:c0 + cw] += _colsum(du * shifted)
            dz_ref[:, c0:c0 + cw] = (dq_ref[:, c0:c0 + cw] * conv).astype(dz_ref.dtype)
            dz_ref[:, d + c0:d + c0 + cw] = (dp * z_ref[:, 2 * d + c0:2 * d + c0 + cw].astype(F32)).astype(dz_ref.dtype)
            dz_ref[:, 2 * d + c0:2 * d + c0 + cw] = (dp * z_ref[:, d + c0:d + c0 + cw].astype(F32)).astype(dz_ref.dtype)

    last = s // halo - 1
    return pl.pallas_call(
        body, name=name, grid=(n_tiles,),
        in_specs=[pl.BlockSpec((ts, d), lambda i: (i, 0)),
                  pl.BlockSpec((halo, d), lambda i: (jnp.minimum((i + 1) * per, last), 0)),
                  pl.BlockSpec((ts, d3), lambda i: (i, 0)),
                  pl.BlockSpec((halo, d3), lambda i: (jnp.maximum(i * per - 1, 0), 0)),
                  pl.BlockSpec((halo, d3), lambda i: (jnp.minimum((i + 1) * per, last), 0)),
                  pl.BlockSpec((taps, d), lambda i: (0, 0))] + [ANY_SPEC] * len(deps),
        out_specs=[pl.BlockSpec((ts, d3), lambda i: (i, 0)), pl.BlockSpec((8, d), lambda i: (0, 0))],
        out_shape=[jax.ShapeDtypeStruct((s, d3), ACT_DTYPE), jax.ShapeDtypeStruct((8, d), F32)],
        scratch_shapes=[pltpu.VMEM((ts + halo, d), F32), pltpu.VMEM((ts + halo, d), F32)],
        compiler_params=_params("arbitrary"),
    )(dq, dq, z, z, z, w, *deps)


def _silu(v):
    return v * jax.nn.sigmoid(v)


def _modulation(name, c_all, mod_w, mod_b_cols):
    nl, d, n = mod_w.shape
    b = c_all.shape[0]
    tn = _pick(n, 512)

    def body(c_ref, w_ref, b_ref, o_ref):
        ca = _silu(c_ref[...]).astype(MXU_DTYPE)
        o_ref[...] = jnp.dot(ca, w_ref[...].astype(MXU_DTYPE), preferred_element_type=F32) + b_ref[...]

    return pl.pallas_call(
        body, name=name, grid=(nl, n // tn),
        in_specs=[pl.BlockSpec((b, d), lambda l, j: (0, 0)),
                  pl.BlockSpec((None, d, tn), lambda l, j: (l, 0, j)),
                  pl.BlockSpec((None, 1, tn), lambda l, j: (l, 0, j))],
        out_specs=pl.BlockSpec((None, b, tn), lambda l, j: (l, 0, j)),
        out_shape=jax.ShapeDtypeStruct((nl, b, n), F32),
        compiler_params=_params("parallel", "parallel"),
    )(c_all, mod_w, mod_b_cols.reshape(nl, 1, n))


def _adamw(g, w, m, v):
    m = ADAM_B1 * m + (1.0 - ADAM_B1) * g
    v = ADAM_B2 * v + (1.0 - ADAM_B2) * (g * g)
    m_hat = m / (1.0 - ADAM_B1 ** ADAM_STEP)
    v_hat = v / (1.0 - ADAM_B2 ** ADAM_STEP)
    delta = -ADAM_LR * (m_hat / (jnp.sqrt(v_hat) + ADAM_EPS) + ADAM_WD * w)
    return delta, m, v


def _write_update(g, w_ref, m_ref, v_ref, outs):
    delta, m, v = _adamw(g, w_ref[...], m_ref[...], v_ref[...])
    outs[0][...] = g
    outs[1][...] = delta
    outs[2][...] = m
    outs[3][...] = v


def _modulation_update(name, c_all_t, dmod, w, m, v, deps=()):
    nl, d, n = w.shape
    b = c_all_t.shape[1]
    tr = _pick(d, 256, 8)

    def body(c_ref, dm_ref, w_ref, m_ref, v_ref, *rest):
        outs = rest[len(deps):]
        ca = _silu(c_ref[...])
        dm = dm_ref[...]
        g = ca[:, 0:1] * dm[0:1, :]
        for i in range(1, b):
            g += ca[:, i:i + 1] * dm[i:i + 1, :]
        _write_update(g, w_ref, m_ref, v_ref, outs)

    blk = pl.BlockSpec((None, tr, n), lambda l, r: (l, r, 0))
    return pl.pallas_call(
        body, name=name, grid=(nl, d // tr),
        in_specs=[pl.BlockSpec((tr, b), lambda l, r: (r, 0)), pl.BlockSpec((None, b, n), lambda l, r: (l, 0, 0)),
                  blk, blk, blk] + [ANY_SPEC] * len(deps),
        out_specs=[blk] * 4, out_shape=[jax.ShapeDtypeStruct(w.shape, F32)] * 4,
        compiler_params=_params("parallel", "parallel"),
    )(c_all_t, dmod, w, m, v, *deps)


def _weight_update(name, parts, w, m, v):
    nl, rows, cols = w.shape
    tr = _pick(rows, max(8, (1 << 18) // cols), 8)

    def body(*refs):
        p_refs = refs[:nl]
        w_ref, m_ref, v_ref = refs[nl:nl + 3]
        outs = refs[nl + 3:]
        for q in range(nl):
            @pl.when(pl.program_id(0) == q)
            def _(q=q):
                g = p_refs[q][0].astype(F32)
                for k in range(1, N_CHIP):
                    g += p_refs[q][k].astype(F32)
                _write_update(g, w_ref, m_ref, v_ref, outs)

    def part_spec(q):
        return pl.BlockSpec((N_CHIP, tr, cols), lambda l, r: (0, jnp.where(l == q, r, 0), 0))

    blk = pl.BlockSpec((None, tr, cols), lambda l, r: (l, r, 0))
    return pl.pallas_call(
        body, name=name, grid=(nl, rows // tr),
        in_specs=[part_spec(q) for q in range(nl)] + [blk, blk, blk],
        out_specs=[blk] * 4, out_shape=[jax.ShapeDtypeStruct(w.shape, F32)] * 4,
        compiler_params=_params("arbitrary", "arbitrary"),
    )(*parts, w, m, v)


def _vector_update(name, gathered, w, m, v):
    _, rows, cols = gathered.shape
    rw = w.shape[0]

    def body(g_ref, w_ref, m_ref, v_ref, tot_ref, d_ref, m2_ref, v2_ref):
        tot = g_ref[0]
        for k in range(1, N_DEV):
            tot += g_ref[k]
        tot_ref[...] = tot
        delta, m2, v2 = _adamw(tot[0:rw], w_ref[...], m_ref[...], v_ref[...])
        d_ref[...] = delta
        m2_ref[...] = m2
        v2_ref[...] = v2

    vm = pl.BlockSpec(memory_space=pltpu.VMEM)
    return pl.pallas_call(
        body, name=name, in_specs=[vm] * 4, out_specs=[vm] * 4,
        out_shape=[jax.ShapeDtypeStruct((rows, cols), F32)] + [jax.ShapeDtypeStruct((rw, cols), F32)] * 3,
        compiler_params=pltpu.CompilerParams(vmem_limit_bytes=V7X_VMEM_LIMIT_BYTES),
    )(gathered, w, m, v)


def _plain_update(name, g, w, m, v):
    def body(g_ref, w_ref, m_ref, v_ref, d_ref, m2_ref, v2_ref):
        delta, m2, v2 = _adamw(g_ref[...], w_ref[...], m_ref[...], v_ref[...])
        d_ref[...] = delta
        m2_ref[...] = m2
        v2_ref[...] = v2

    vm = pl.BlockSpec(memory_space=pltpu.VMEM)
    return pl.pallas_call(
        body, name=name, in_specs=[vm] * 4, out_specs=[vm] * 3,
        out_shape=[jax.ShapeDtypeStruct(w.shape, F32)] * 3,
    )(g, w, m, v)


def _rows(a, width, mult=8):
    r = a.reshape(-1, width)
    pad = -r.shape[0] % mult
    return jnp.pad(r, ((0, pad), (0, 0))) if pad else r


def _pack(blocks, width):
    parts, spans, at = [], [], 0
    for a in blocks:
        n = a.size // width
        p = _rows(a, width)
        parts.append(p)
        spans.append((at, n))
        at += p.shape[0]
    return jnp.concatenate(parts, axis=0), spans


def kernel(x, c, mod_w, mod_b, pre_mix_g, post_mix_g, pre_ffn_g, post_ffn_g, a_w1, a_b1, a_dw, a_dwb, a_ln_g, a_ln_b, a_w2, a_b2, b_w_in, b_conv, b_w_out, f_w1, f_w2, loss_target, m_mod_w, m_mod_b, m_pre_mix_g, m_post_mix_g, m_pre_ffn_g, m_post_ffn_g, m_a_w1, m_a_b1, m_a_dw, m_a_dwb, m_a_ln_g, m_a_ln_b, m_a_w2, m_a_b2, m_b_w_in, m_b_conv, m_b_w_out, m_f_w1, m_f_w2, v_mod_w, v_mod_b, v_pre_mix_g, v_post_mix_g, v_pre_ffn_g, v_post_ffn_g, v_a_w1, v_a_b1, v_a_dw, v_a_dwb, v_a_ln_g, v_a_ln_b, v_a_w2, v_a_b2, v_b_w_in, v_b_conv, v_b_w_out, v_f_w1, v_f_w2):
    depth, d = pre_mix_g.shape
    n_a, n_b = a_w1.shape[0], b_w_in.shape[0]
    s = x.shape[1]
    dsh = d // N_DEV
    taps_a, taps_b = a_dw.shape[1], b_conv.shape[1]
    px, py, pc = _pos()
    me = 4 * px + 2 * py + pc
    x0 = x.reshape(s, d)
    target = loss_target.reshape(s, d)

    me1 = jnp.reshape(me, (1,)).astype(jnp.int32)

    def members(i):
        j = i // 2
        mix = [(a_w1, j, True), (a_w2, j, False)] if i % 2 == 0 else [(b_w_in, j, True), (b_w_out, j, False)]
        return mix, [(f_w1, i, True), (f_w2, i, False)]

    placed = {(i, q): _place("place_l%d_%d" % (i, q), w, layer, me1, by_columns)
              for i in range(depth) for q, (w, layer, by_columns) in enumerate(sum(members(i), []))}

    def start_group(tag, group, after=()):
        return _gather_start(tag, [placed[key] for key in group], after)

    packed0, spans0 = _pack([c, a_dw, b_conv], dsh)
    got0 = _small_gather("gather_cond", packed0)

    def full_width(span):
        at, n = span
        return jnp.transpose(got0[:, at:at + n, :], (1, 0, 2)).reshape(n, d)

    c_all = got0[:, spans0[0][0]:spans0[0][0] + spans0[0][1], :].reshape(N_DEV, d)
    a_dw_full = full_width(spans0[1]).reshape(n_a, taps_a, d)
    b_conv_full = full_width(spans0[2]).reshape(n_b, taps_b, d)

    n_mod = mod_w.shape[2]
    mod_b_cols = lax.dynamic_slice_in_dim(mod_b, me * n_mod, n_mod, axis=1)
    mod_local = _modulation("modulation", c_all, mod_w, mod_b_cols)
    got1 = _small_gather("gather_mod", mod_local.reshape(depth * N_DEV, n_mod))
    mod_me = lax.dynamic_index_in_dim(got1.reshape(N_DEV, depth, N_DEV, n_mod), me, axis=2, keepdims=False)
    mod_me = jnp.transpose(mod_me, (1, 0, 2)).reshape(depth, 6, 1, d)

    def vec(a, i):
        return a[i].reshape(1, -1)

    def pass_groups(tag, handles, after):
        for q in range(len(handles)):
            handles[q] = _gather_pass("%s%d" % (tag, q), handles[q], after)
            after = handles[q]["token"]
        return after

    def wait_groups(tag, handles, after):
        return sum([_gather_wait("%s%d" % (tag, q), hq, after) for q, hq in enumerate(handles)], [])

    mix_copy = [start_group("l0m", [(0, 0)], (got1,))]
    mix_late = [start_group("l0n", [(0, 1)], (mix_copy[0]["token"],))]
    ffn_copy = [start_group("l0u", [(0, 2)], (mix_late[0]["token"],))]
    ffn_copy.append(start_group("l0d", [(0, 3)], (ffn_copy[0]["token"],)))
    saved = []
    xs = x0
    h = _residual_fwd("fwd_in", xs, pre=(vec(pre_mix_g, 0), mod_me[0, 1], mod_me[0, 0]),
                      deps=[hq["token"] for hq in ffn_copy] + [buf for (i, _), buf in placed.items() if i > 0])[0]
    w_mix = wait_groups("l0m", mix_copy, pass_groups("l0m", mix_copy, h))
    for i in range(depth):
        j = i // 2
        sh_m, sc_m, gt_m, sh_f, sc_f, gt_f = [mod_me[i, q] for q in range(6)]
        keep = {"x_mix": xs, "h_mix": h}
        early = i > 0
        w1g, w2g = w_mix if early else (w_mix[0], None)
        if i % 2 == 0:
            u = _mm_cols("a_w1", h, w1g, F32, _store_bias, bias=vec(a_b1, j))[0]
            deps = [pass_groups("l%df" % i, ffn_copy, u)] if early else []
            if not early:
                w2g, = wait_groups("l0n", mix_late, pass_groups("l0n", mix_late, u))
            cv, v = _glu_conv_ln_swish("a_conv", u, a_dw_full[j], vec(a_dwb, j), vec(a_ln_g, j), vec(a_ln_b, j))
            y = _mm_rows("a_w2", v, w2g, F32, bias=vec(a_b2, j), deps=deps)
            keep.update(u=u, cv=cv, v=v)
        else:
            z = _mm_cols("b_w_in", h, w1g, F32)[0]
            deps = [pass_groups("l%df" % i, ffn_copy, z)] if early else []
            q = _short_conv_fwd("b_conv", z, b_conv_full[j])
            y = _mm_rows("b_w_out", q, w2g, F32, deps=deps)
            keep.update(z=z, q=q)
        after = y if early else pass_groups("l%df" % i, ffn_copy, y)
        deps = []
        if i + 1 < depth:
            mix_copy = [start_group("l%dm" % (i + 1), [(i + 1, 0), (i + 1, 1)])]
            ffn_next = [start_group("l%df" % (i + 1), [(i + 1, 2), (i + 1, 3)])]
            deps = [mix_copy[0]["token"], ffn_next[0]["token"]]
        fw1g, fw2g = wait_groups("l%df" % i, ffn_copy, after)
        keep.update(w1g=w1g, w2g=w2g, fw1g=fw1g, fw2g=fw2g, y_mix=y)
        xs, h = _residual_fwd("fwd_mid", xs, post=(y, vec(post_mix_g, i), gt_m),
                              pre=(vec(pre_ffn_g, i), sc_f, sh_f))
        keep.update(x_ffn=xs, h_ffn=h)
        act, hid = _mm_cols("f_w1", h, fw1g, ACT_DTYPE, _store_relu2, n_out=2, deps=deps)
        deps = [pass_groups("l%dm" % (i + 1), mix_copy, act)] if i + 1 < depth else []
        y = _mm_rows("f_w2", act, fw2g, F32, deps=deps)
        keep.update(act=act, hid=hid, y_ffn=y)
        saved.append(keep)
        if i + 1 < depth:
            w_mix = wait_groups("l%dm" % (i + 1), mix_copy, y)
            ffn_copy = ffn_next
            nxt = mod_me[i + 1]
            xs, h = _residual_fwd("fwd_next", xs, post=(y, vec(post_ffn_g, i), gt_f),
                                  pre=(vec(pre_mix_g, i + 1), nxt[1], nxt[0]))
        else:
            g, dy, loss_blk, sums = _loss_junction("loss_junction", xs, y, vec(post_ffn_g, i), gt_f, target)

    where = jnp.stack([pc, 2 * px + py]).astype(jnp.int32)
    land = {"a_w1": [None] * n_a, "a_w2": [None] * n_a, "b_w_in": [None] * n_b, "b_w_out": [None] * n_b,
            "f_w1": [None] * depth, "f_w2": [None] * depth}
    reductions = []

    def reduce_group(tag, slots, swap, after):
        grads, theirs = _sibling_wait(tag, swap, after)
        pairs = [_chip_partial("chip_partial_%s_%d" % (tag, q), gq, tq, where)
                 for q, (gq, tq) in enumerate(zip(grads, theirs))]
        handle = _reduce_start(tag, [p for p, _ in pairs], [l for _, l in pairs])
        reductions.append((tag, slots, handle))
        return handle["token"]

    zero_vec = jnp.zeros((d,), F32)
    dmod = [[zero_vec] * 6 for _ in range(depth)]
    small = {name: [None] * depth for name in ("pre_mix_g", "post_mix_g", "pre_ffn_g", "post_ffn_g")}
    small_a = {name: [None] * n_a for name in ("a_b1", "a_dwb", "a_ln_g", "a_ln_b", "a_b2", "a_dw")}
    small_b = {"b_conv": [None] * n_b}

    deps = []
    for i in reversed(range(depth)):
        j = i // 2
        kp = saved[i]
        sh_m, sc_m, gt_m, sh_f, sc_f, gt_f = [mod_me[i, q] for q in range(6)]
        dmod[i][5] = sums[3]
        small["post_ffn_g"][i] = sums[4]
        dhid = _mm_rows_t("f_w2_t", dy, kp["fw2g"], ACT_DTYPE, _store_relu2_grad, extra=kp["hid"], deps=deps)
        deps = []
        if i + 1 < depth:
            deps = [reduce_group("r%dm" % (i + 1), last_swap[0], last_swap[1], dhid)]
        g_fw2 = _grad_rows("f_w2_grad", kp["act"], dy, f_w2.shape[1])
        dh = _mm_cols_t("f_w1_t", dhid, kp["fw1g"], F32, deps=deps)
        g_fw1 = _grad_cols("f_w1_grad", kp["h_ffn"], dhid, f_w1.shape[2])
        ffn_swap = _sibling_start("r%df" % i, [g_fw1, g_fw2])
        deps = [ffn_swap["token"]]
        g, dy, sums = _residual_bwd("bwd_mid", g, pre=(dh, kp["x_ffn"], vec(pre_ffn_g, i), sc_f),
                                    post=(kp["y_mix"], vec(post_mix_g, i), gt_m))
        dmod[i][3], dmod[i][4], dmod[i][2] = sums[0], sums[1], sums[3]
        small["pre_ffn_g"][i], small["post_mix_g"][i] = sums[2], sums[4]
        if i % 2 == 0:
            small_a["a_b2"][j] = sums[5]
            dv = _mm_rows_t("a_w2_t", dy, kp["w2g"], F32, deps=deps)
            g_w2 = _grad_rows("a_w2_grad", kp["v"], dy, a_w2.shape[1])
            deps = [reduce_group("r%df" % i, (("f_w1", i), ("f_w2", i)), ffn_swap, g_w2)]
            dcv, lsum = _ln_swish_bwd("a_ln_bwd", dv, kp["cv"], vec(a_ln_g, j), vec(a_ln_b, j), deps=deps)
            small_a["a_ln_g"][j], small_a["a_ln_b"][j], small_a["a_dwb"][j] = lsum[0], lsum[1], lsum[2]
            du, ddw, usum = _conv_glu_bwd("a_conv_bwd", dcv, kp["u"], a_dw_full[j])
            small_a["a_dw"][j], small_a["a_b1"][j] = ddw[:taps_a], usum[0]
            dh = _mm_cols_t("a_w1_t", du, kp["w1g"], F32)
            g_w1 = _grad_cols("a_w1_grad", kp["h_mix"], du, a_w1.shape[2])
            names = ("a_w1", "a_w2")
        else:
            dq = _mm_rows_t("b_w_out_t", dy, kp["w2g"], F32, deps=deps)
            g_w2 = _grad_rows("b_w_out_grad", kp["q"], dy, b_w_out.shape[1])
            deps = [reduce_group("r%df" % i, (("f_w1", i), ("f_w2", i)), ffn_swap, g_w2)]
            dz, wsum = _short_conv_bwd("b_conv_bwd", dq, kp["z"], b_conv_full[j], deps=deps)
            small_b["b_conv"][j] = wsum[:taps_b]
            dh = _mm_cols_t("b_w_in_t", dz, kp["w1g"], F32)
            g_w1 = _grad_cols("b_w_in_grad", kp["h_mix"], dz, b_w_in.shape[2])
            names = ("b_w_in", "b_w_out")
        mix_swap = _sibling_start("r%dm" % i, [g_w1, g_w2])
        deps = [mix_swap["token"]]
        if i > 0:
            prev = saved[i - 1]
            g, dy, sums = _residual_bwd("bwd_next", g, pre=(dh, kp["x_mix"], vec(pre_mix_g, i), sc_m),
                                        post=(prev["y_ffn"], vec(post_ffn_g, i - 1), mod_me[i - 1, 5]))
        else:
            g, sums = _residual_bwd("bwd_in", g, pre=(dh, kp["x_mix"], vec(pre_mix_g, i), sc_m))
        dmod[i][0], dmod[i][1] = sums[0], sums[1]
        small["pre_mix_g"][i] = sums[2]
        last_swap = (((names[0], j), (names[1], j)), mix_swap)
    grad_x = g.reshape(x.shape)

    reps = [jnp.stack([jnp.stack(r) for r in dmod]),
            jnp.stack(small["pre_mix_g"]), jnp.stack(small["post_mix_g"]),
            jnp.stack(small["pre_ffn_g"]), jnp.stack(small["post_ffn_g"]),
            jnp.stack(small_a["a_b1"]), jnp.stack(small_a["a_dwb"]), jnp.stack(small_a["a_ln_g"]),
            jnp.stack(small_a["a_ln_b"]), jnp.stack(small_a["a_b2"])]
    rep_w = [mod_b, pre_mix_g, post_mix_g, pre_ffn_g, post_ffn_g, a_b1, a_dwb, a_ln_g, a_ln_b, a_b2]
    rep_m = [m_mod_b, m_pre_mix_g, m_post_mix_g, m_pre_ffn_g, m_post_ffn_g, m_a_b1, m_a_dwb, m_a_ln_g, m_a_ln_b, m_a_b2]
    rep_v = [v_mod_b, v_pre_mix_g, v_post_mix_g, v_pre_ffn_g, v_post_ffn_g, v_a_b1, v_a_dwb, v_a_ln_g, v_a_ln_b, v_a_b2]
    w_small, spans = _pack(rep_w, d)
    m_small, _ = _pack(rep_m, d)
    v_small, _ = _pack(rep_v, d)
    loss_row = jnp.pad(loss_blk[0:1, 0:1], ((0, 0), (0, d - 1)))
    part_small, spans_g = _pack(reps + [jnp.stack(small_a["a_dw"]), jnp.stack(small_b["b_conv"]), loss_row], d)
    small_copy = _gather_start("small", [_place("place_small", part_small[None], 0, me1, False, F32)],
                               (last_swap[1]["token"],))
    after = reduce_group("r0m", last_swap[0], last_swap[1], small_copy["token"])
    big = {"f_w1": (f_w1, m_f_w1, v_f_w1), "f_w2": (f_w2, m_f_w2, v_f_w2),
           "b_w_in": (b_w_in, m_b_w_in, v_b_w_in), "b_w_out": (b_w_out, m_b_w_out, v_b_w_out),
           "a_w1": (a_w1, m_a_w1, v_a_w1), "a_w2": (a_w2, m_a_w2, v_a_w2)}
    big_out = {}

    def update_complete():
        nonlocal after
        for k, wmv in big.items():
            if k not in big_out and all(b is not None for b in land[k]):
                big_out[k] = _weight_update("update_" + k, land[k], *wmv)
                after = big_out[k][3]

    def wait_reduction(tag, slots, handle):
        nonlocal after
        landed = _reduce_wait(tag, handle, after)
        after = landed[0]
        for (key, idx), buf in zip(slots, landed):
            land[key][idx] = buf

    for entry in reductions[:-1]:
        wait_reduction(*entry)
    update_complete()
    small_copy = _gather_pass("small", small_copy, after)
    got2, = _gather_wait("small", small_copy, small_copy["token"])
    total, d_small, m2_small, v2_small = _vector_update("vector_update", got2, w_small, m_small, v_small)

    def unpack(buf, span, like):
        return buf[span[0]:span[0] + span[1]].reshape(like.shape)

    rep_out = [[unpack(buf, sp, w) for sp, w in zip(spans, rep_w)] for buf in (total, d_small, m2_small, v2_small)]
    loss = total[spans_g[-1][0], 0]

    conv_g = [lax.dynamic_slice_in_dim(total[sp[0]:sp[0] + sp[1]], me * dsh, dsh, axis=1)
              for sp in spans_g[len(reps):len(reps) + 2]]
    gc, _ = _pack(conv_g, dsh)
    wc, spans_c = _pack([a_dw, b_conv], dsh)
    mc, _ = _pack([m_a_dw, m_b_conv], dsh)
    vc, _ = _pack([v_a_dw, v_b_conv], dsh)
    conv_out = [gc] + list(_plain_update("conv_update", gc, wc, mc, vc))
    conv_out = [[unpack(buf, sp, w) for sp, w in zip(spans_c, (a_dw, b_conv))] for buf in conv_out]

    at, n = spans_g[0]
    dmod_all = got2[:, at:at + n, :].reshape(N_DEV, depth, 6 * d)
    dmod_cols = jnp.transpose(lax.dynamic_slice_in_dim(dmod_all, me * n_mod, n_mod, axis=2), (1, 0, 2))
    mod_out = _modulation_update("modulation_update", jnp.transpose(c_all), dmod_cols, mod_w, m_mod_w, v_mod_w)
    after = mod_out[3]
    wait_reduction(*reductions[-1])
    update_complete()

    def family(q):
        rep = dict(zip(("mod_b", "pre_mix_g", "post_mix_g", "pre_ffn_g", "post_ffn_g",
                        "a_b1", "a_dwb", "a_ln_g", "a_ln_b", "a_b2"), rep_out[q]))
        return (mod_out[q], rep["mod_b"], rep["pre_mix_g"], rep["post_mix_g"], rep["pre_ffn_g"], rep["post_ffn_g"],
                big_out["a_w1"][q], rep["a_b1"], conv_out[q][0], rep["a_dwb"], rep["a_ln_g"], rep["a_ln_b"],
                big_out["a_w2"][q], rep["a_b2"], big_out["b_w_in"][q], conv_out[q][1], big_out["b_w_out"][q],
                big_out["f_w1"][q], big_out["f_w2"][q])

    return (loss, grad_x, *family(0), *family(1), *family(2), *family(3))
```
